```python
import math
import jax, jax.numpy as jnp
from jax import lax
import numpy as np

D_MODEL = 1024
BATCH = 4
SEQ = 4096
DEPTH = 2

CHUNK = 64
Q_BLOCK = 128
N_HEADS = 8
QK_NOPE_DIM = 128
QK_ROPE_DIM = 64
V_HEAD_DIM = 128
Q_LORA_RANK = 512
KV_LORA_RANK = 256
ROPE_THETA = 10000.0
MAX_POS_OFFSET = 16384
CONV_WIDTH = 3
D_FF = ((8 * D_MODEL // 3 + 255) // 256) * 256
RMS_EPS = 1e-6
N_MLA_LAYERS = (DEPTH + 1) // 2
N_CONV_LAYERS = DEPTH // 2
MLA_IN_DIM = Q_LORA_RANK + KV_LORA_RANK + QK_ROPE_DIM

kernel_name = "hybrid_mla_shortconv_swiglu_trunk"


def rms_norm(x, g):
    xf = x.astype(jnp.float32)
    y = xf * lax.rsqrt(jnp.mean(xf * xf, axis=-1, keepdims=True) + RMS_EPS)
    return (y * g.astype(jnp.float32)).astype(x.dtype)


def rope_tables(positions):
    inv_freq = 1.0 / (ROPE_THETA ** (jnp.arange(0, QK_ROPE_DIM, 2, dtype=jnp.float32) / QK_ROPE_DIM))
    ang = positions.astype(jnp.float32)[..., None] * inv_freq
    return jnp.cos(ang), jnp.sin(ang)


def apply_rope(x, cos, sin):
    xf = x.astype(jnp.float32)
    x1, x2 = jnp.split(xf, 2, axis=-1)
    out = jnp.concatenate([x1 * cos - x2 * sin, x1 * sin + x2 * cos], axis=-1)
    return out.astype(x.dtype)


def chunk_causal_mla_attention(q_n, q_r, k_n, k_r, v):
    bsz, seq, h, _ = q_n.shape
    nb = seq // Q_BLOCK
    scale = 1.0 / math.sqrt(QK_NOPE_DIM + QK_ROPE_DIM)
    qn_b = q_n.reshape(bsz, nb, Q_BLOCK, h, QK_NOPE_DIM).transpose(1, 0, 2, 3, 4)
    qr_b = q_r.reshape(bsz, nb, Q_BLOCK, h, QK_ROPE_DIM).transpose(1, 0, 2, 3, 4)
    k_chunk = jnp.arange(seq) // CHUNK

    def one_block(args):
        qn, qr, blk = args
        q_chunk = (blk * Q_BLOCK + jnp.arange(Q_BLOCK)) // CHUNK
        s = (jnp.einsum('bqhd,bkhd->bhqk', qn, k_n).astype(jnp.float32)
             + jnp.einsum('bqhd,bkd->bhqk', qr, k_r).astype(jnp.float32)) * scale
        mask = k_chunk[None, :] <= q_chunk[:, None]
        s = jnp.where(mask[None, None], s, jnp.float32(-1e30))
        p = jax.nn.softmax(s, axis=-1).astype(v.dtype)
        return jnp.einsum('bhqk,bkhd->bqhd', p, v)

    out = lax.map(one_block, (qn_b, qr_b, jnp.arange(nb)))
    return out.transpose(1, 0, 2, 3, 4).reshape(bsz, seq, h, V_HEAD_DIM)


def mla_mixer(h, positions, w_in, g_cq, g_ckv, w_uq, w_ukv, w_o):
    bsz, seq, _ = h.shape
    proj = h @ w_in
    c_q = proj[..., :Q_LORA_RANK]
    c_kv = proj[..., Q_LORA_RANK:Q_LORA_RANK + KV_LORA_RANK]
    k_r = proj[..., Q_LORA_RANK + KV_LORA_RANK:]
    c_q = rms_norm(c_q, g_cq)
    c_kv = rms_norm(c_kv, g_ckv)
    q = (c_q @ w_uq).reshape(bsz, seq, N_HEADS, QK_NOPE_DIM + QK_ROPE_DIM)
    q_n, q_r = q[..., :QK_NOPE_DIM], q[..., QK_NOPE_DIM:]
    kv = (c_kv @ w_ukv).reshape(bsz, seq, N_HEADS, QK_NOPE_DIM + V_HEAD_DIM)
    k_n, v = kv[..., :QK_NOPE_DIM], kv[..., QK_NOPE_DIM:]
    cos, sin = rope_tables(positions)
    q_r = apply_rope(q_r, cos[:, :, None, :], sin[:, :, None, :])
    k_r = apply_rope(k_r, cos, sin)
    attn = chunk_causal_mla_attention(q_n, q_r, k_n, k_r, v)
    return attn.reshape(bsz, seq, N_HEADS * V_HEAD_DIM) @ w_o


def short_conv_mixer(h, w_in, conv_w, w_out):
    bcx = h @ w_in
    b_gate = bcx[..., :D_MODEL]
    c_gate = bcx[..., D_MODEL:2 * D_MODEL]
    xp = bcx[..., 2 * D_MODEL:]
    u = c_gate * xp
    u_conv = lax.conv_general_dilated(
        u, conv_w[:, None, :].astype(u.dtype), window_strides=(1,),
        padding=[(CONV_WIDTH - 1, 0)], dimension_numbers=('NWC', 'WIO', 'NWC'),
        feature_group_count=D_MODEL)
    return (b_gate * u_conv) @ w_out


def swiglu(h, w_gate, w_up, w_down):
    return (jax.nn.silu(h @ w_gate) * (h @ w_up)) @ w_down


def setup_inputs(seed: int = 0) -> dict:
    key = jax.random.key(seed)
    ks = jax.random.split(key, 24)

    def w(k, shape, fan_in):
        return jax.random.normal(k, shape, jnp.float32) * (fan_in ** -0.5)

    def gain(k, shape):
        return 1.0 + 0.05 * jax.random.normal(k, shape, jnp.float32)

    x = jax.random.normal(ks[0], (BATCH, SEQ, D_MODEL), jnp.float32)
    offset = jax.random.randint(ks[1], (BATCH, 1), 0, MAX_POS_OFFSET, dtype=jnp.int32)
    positions = (offset + jnp.arange(SEQ, dtype=jnp.int32)[None, :]).astype(jnp.int32)
    L_a, L_b = N_MLA_LAYERS, N_CONV_LAYERS
    return {
        "x": x,
        "positions": positions,
        "mla_norm": gain(ks[2], (L_a, D_MODEL)),
        "mla_w_in": w(ks[3], (L_a, D_MODEL, MLA_IN_DIM), D_MODEL),
        "mla_g_cq": gain(ks[4], (L_a, Q_LORA_RANK)),
        "mla_g_ckv": gain(ks[5], (L_a, KV_LORA_RANK)),
        "mla_w_uq": w(ks[6], (L_a, Q_LORA_RANK, N_HEADS * (QK_NOPE_DIM + QK_ROPE_DIM)), Q_LORA_RANK),
        "mla_w_ukv": w(ks[7], (L_a, KV_LORA_RANK, N_HEADS * (QK_NOPE_DIM + V_HEAD_DIM)), KV_LORA_RANK),
        "mla_w_o": w(ks[8], (L_a, N_HEADS * V_HEAD_DIM, D_MODEL), N_HEADS * V_HEAD_DIM),
        "conv_norm": gain(ks[9], (L_b, D_MODEL)),
        "conv_w_in": w(ks[10], (L_b, D_MODEL, 3 * D_MODEL), D_MODEL),
        "conv_w": w(ks[11], (L_b, CONV_WIDTH, D_MODEL), CONV_WIDTH),
        "conv_w_out": w(ks[12], (L_b, D_MODEL, D_MODEL), D_MODEL),
        "ffn_norm": gain(ks[13], (DEPTH, D_MODEL)),
        "ffn_w_gate": w(ks[14], (DEPTH, D_MODEL, D_FF), D_MODEL),
        "ffn_w_up": w(ks[15], (DEPTH, D_MODEL, D_FF), D_MODEL),
        "ffn_w_down": w(ks[16], (DEPTH, D_FF, D_MODEL), D_FF),
        "final_norm": gain(ks[17], (D_MODEL,)),
    }


def reference(x, positions, mla_norm, mla_w_in, mla_g_cq, mla_g_ckv, mla_w_uq,
              mla_w_ukv, mla_w_o, conv_norm, conv_w_in, conv_w, conv_w_out,
              ffn_norm, ffn_w_gate, ffn_w_up, ffn_w_down, final_norm):
    h = x
    for i in range(DEPTH):
        j = i // 2
        if i % 2 == 0:
            h = h + mla_mixer(rms_norm(h, mla_norm[j]), positions, mla_w_in[j],
                              mla_g_cq[j], mla_g_ckv[j], mla_w_uq[j], mla_w_ukv[j], mla_w_o[j])
        else:
            h = h + short_conv_mixer(rms_norm(h, conv_norm[j]), conv_w_in[j],
                                     conv_w[j], conv_w_out[j])
        h = h + swiglu(rms_norm(h, ffn_norm[i]), ffn_w_gate[i], ffn_w_up[i], ffn_w_down[i])
    return rms_norm(h, final_norm)
```

```python
import functools
import math

import jax
import jax.numpy as jnp
import numpy as np
from jax import lax
from jax.experimental import pallas as pl
from jax.experimental.pallas import tpu as pltpu

D_MODEL = 1024
CHUNK = 64
N_HEADS = 8
QK_NOPE_DIM = 128
QK_ROPE_DIM = 64
V_HEAD_DIM = 128
Q_LORA_RANK = 512
KV_LORA_RANK = 256
ROPE_THETA = 10000.0
CONV_WIDTH = 3
RMS_EPS = 1e-6

LANES = 128
SUBLANES = 8
MXU_COLS = 256
VMEM_LIMIT_BYTES = 56 * 1024 * 1024

TOKEN_TILE = 512
ATTN_TILE = 512
FF_CHUNK = MXU_COLS
CONV_CHUNK = MXU_COLS

BF16 = jnp.bfloat16
F32 = jnp.float32


def _dot(a, b):
    return jnp.dot(a, b, preferred_element_type=F32)


def _rms(x, g):
    return x * lax.rsqrt(jnp.mean(x * x, axis=-1, keepdims=True) + RMS_EPS) * g


def _const_spec(shape):
    return pl.BlockSpec(shape, lambda *_: (0,) * len(shape), pipeline_mode=pl.Buffered(1))


def _params(semantics):
    return pltpu.CompilerParams(dimension_semantics=semantics, vmem_limit_bytes=VMEM_LIMIT_BYTES)


def _mla_pre_kernel(x_ref, pos_ref, g_ref, w_in_ref, gcq_ref, gckv_ref, wqn_ref, wqr_ref,
                    wkn_ref, wv_ref, invf_ref, qn_ref, qr_ref, kn_ref, kr_ref, v_ref, *, scale):
    tm = x_ref.shape[0]
    xn = _rms(x_ref[...], g_ref[...]).astype(BF16)
    proj = _dot(xn, w_in_ref[...])
    cq = _rms(proj[:, :Q_LORA_RANK], gcq_ref[...]).astype(BF16)
    ckv = _rms(proj[:, Q_LORA_RANK:Q_LORA_RANK + KV_LORA_RANK], gckv_ref[...]).astype(BF16)
    kr = proj[:, Q_LORA_RANK + KV_LORA_RANK:]

    ang = pos_ref[...] * invf_ref[...]
    cos = jnp.cos(ang)
    sin = jnp.sin(ang)
    lane = lax.broadcasted_iota(jnp.int32, (tm, LANES), 1)
    first_half = (lane % QK_ROPE_DIM) < (QK_ROPE_DIM // 2)
    low_block = lane < QK_ROPE_DIM
    sin_signed = jnp.where(first_half, -sin, sin)

    def rope(t):
        partner = jnp.where(first_half,
                            pltpu.roll(t, LANES - QK_ROPE_DIM // 2, 1),
                            pltpu.roll(t, QK_ROPE_DIM // 2, 1))
        return t * cos + partner * sin_signed

    kr_ref[...] = rope(kr).astype(BF16)

    qn = _dot(cq, wqn_ref[...]) * scale
    for h in range(N_HEADS):
        qn_ref[h] = qn[:, h * LANES:(h + 1) * LANES].astype(BF16)

    qr = _dot(cq, wqr_ref[...])
    for p in range(N_HEADS // 2):
        grp = rope(qr[:, p * LANES:(p + 1) * LANES]) * scale
        qr_ref[2 * p] = jnp.where(low_block, grp, 0.0).astype(BF16)
        qr_ref[2 * p + 1] = jnp.where(low_block, pltpu.roll(grp, QK_ROPE_DIM, 1), 0.0).astype(BF16)

    kn = _dot(ckv, wkn_ref[...])
    vv = _dot(ckv, wv_ref[...])
    for h in range(N_HEADS):
        kn_ref[h] = kn[:, h * LANES:(h + 1) * LANES].astype(BF16)
        v_ref[h] = vv[:, h * LANES:(h + 1) * LANES].astype(BF16)


def _mla_pre(x2, pos, g, w_in, gcq, gckv, wqn, wqr, wkn, wv, invf):
    t = x2.shape[0]
    tm = TOKEN_TILE
    scale = 1.0 / math.sqrt(QK_NOPE_DIM + QK_ROPE_DIM)
    head_out = jax.ShapeDtypeStruct((N_HEADS, t, LANES), BF16)
    head_spec = pl.BlockSpec((N_HEADS, tm, LANES), lambda i: (0, i, 0))
    return pl.pallas_call(
        functools.partial(_mla_pre_kernel, scale=scale),
        grid=(t // tm,),
        in_specs=[
            pl.BlockSpec((tm, D_MODEL), lambda i: (i, 0)),
            pl.BlockSpec((tm, 1), lambda i: (i, 0)),
            _const_spec(g.shape), _const_spec(w_in.shape), _const_spec(gcq.shape),
            _const_spec(gckv.shape), _const_spec(wqn.shape), _const_spec(wqr.shape),
            _const_spec(wkn.shape), _const_spec(wv.shape), _const_spec(invf.shape),
        ],
        out_specs=[head_spec, head_spec, head_spec,
                   pl.BlockSpec((tm, LANES), lambda i: (i, 0)), head_spec],
        out_shape=[head_out, head_out, head_out, jax.ShapeDtypeStruct((t, LANES), BF16), head_out],
        compiler_params=_params(("parallel",)),
        name="mla_pre",
    )(x2, pos, g, w_in, gcq, gckv, wqn, wqr, wkn, wv, invf)


def _attn_kernel(qn_ref, qr_ref, kn_ref, kr_ref, v_ref, o_ref):
    seq = qn_ref.shape[0]
    ta = ATTN_TILE
    row_chunk = lax.broadcasted_iota(jnp.int32, (ta, ta), 0) // CHUNK
    col_chunk = lax.broadcasted_iota(jnp.int32, (ta, ta), 1) // CHUNK
    diag_mask = col_chunk <= row_chunk

    def tile_scores(q, j):
        ks = pl.ds(pl.multiple_of(j * ta, ta), ta)
        k = jnp.concatenate([kn_ref[ks, :], kr_ref[ks, :]], axis=1)
        s = lax.dot_general(q, k, (((1,), (1,)), ((), ())), preferred_element_type=F32)
        return s, v_ref[ks, :]

    def online_update(carry, s, v):
        m, l, acc = carry
        m_new = jnp.maximum(m, jnp.max(s, axis=1, keepdims=True))
        alpha = jnp.exp(m - m_new)
        p = jnp.exp(s - m_new)
        l_new = alpha * l + jnp.sum(p, axis=1, keepdims=True)
        acc_new = alpha * acc + _dot(p.astype(BF16), v)
        return m_new, l_new, acc_new

    def q_tile(i, _):
        qs = pl.ds(pl.multiple_of(i * ta, ta), ta)
        q = jnp.concatenate([qn_ref[qs, :], qr_ref[qs, :]], axis=1)

        def kv_step(j, carry):
            s, v = tile_scores(q, j)
            return online_update(carry, s, v)

        init = (jnp.full((ta, 1), -1e30, F32), jnp.zeros((ta, 1), F32), jnp.zeros((ta, V_HEAD_DIM), F32))
        carry = lax.fori_loop(0, i, kv_step, init)
        s, v = tile_scores(q, i)
        _, l, acc = online_update(carry, jnp.where(diag_mask, s, -1e30), v)
        o_ref[qs, :] = (acc / l).astype(o_ref.dtype)
        return 0

    lax.fori_loop(0, seq // ta, q_tile, 0)


def _attention(qn, qr, kn, kr, v, bsz, seq):
    t = bsz * seq
    head_spec = pl.BlockSpec((None, seq, LANES), lambda b, h: (h, b, 0))
    return pl.pallas_call(
        _attn_kernel,
        grid=(bsz, N_HEADS),
        in_specs=[head_spec, head_spec, head_spec,
                  pl.BlockSpec((seq, LANES), lambda b, h: (b, 0)), head_spec],
        out_specs=head_spec,
        out_shape=jax.ShapeDtypeStruct((N_HEADS, t, LANES), BF16),
        compiler_params=_params(("parallel", "parallel")),
        name="attention",
    )(qn, qr, kn, kr, v)


def _swiglu_residual(h1, g, wg_ref, wu_ref, wd_ref, act_ref):
    d_ff = wg_ref.shape[1]
    hn = _rms(h1, g).astype(BF16)
    for c in range(d_ff // FF_CHUNK):
        cs = slice(c * FF_CHUNK, (c + 1) * FF_CHUNK)
        gate = _dot(hn, wg_ref[:, cs])
        up = _dot(hn, wu_ref[:, cs])
        act_ref[:, cs] = (jax.nn.silu(gate) * up).astype(BF16)
    return h1 + _dot(act_ref[...], wd_ref[...])


def _wo_ffn_kernel(a_ref, h_ref, wo_ref, g_ref, wg_ref, wu_ref, wd_ref, out_ref, act_ref):
    a = jnp.concatenate([a_ref[h] for h in range(N_HEADS)], axis=1)
    h1 = h_ref[...] + _dot(a, wo_ref[...])
    out_ref[...] = _swiglu_residual(h1, g_ref[...], wg_ref, wu_ref, wd_ref, act_ref)


def _wo_ffn(attn, h, wo, g, wg, wu, wd):
    t = h.shape[0]
    tm = TOKEN_TILE
    return pl.pallas_call(
        _wo_ffn_kernel,
        grid=(t // tm,),
        in_specs=[
            pl.BlockSpec((N_HEADS, tm, LANES), lambda i: (0, i, 0)),
            pl.BlockSpec((tm, D_MODEL), lambda i: (i, 0)),
            _const_spec(wo.shape), _const_spec(g.shape), _const_spec(wg.shape),
            _const_spec(wu.shape), _const_spec(wd.shape),
        ],
        out_specs=pl.BlockSpec((tm, D_MODEL), lambda i: (i, 0)),
        out_shape=jax.ShapeDtypeStruct((t, D_MODEL), F32),
        scratch_shapes=[pltpu.VMEM((tm, wg.shape[1]), BF16)],
        compiler_params=_params(("parallel",)),
        name="wo_ffn",
    )(attn, h, wo, g, wg, wu, wd)


def _conv_ffn_kernel(h_ref, gc_ref, win_ref, cw_ref, wout_ref, gf_ref, wg_ref, wu_ref, wd_ref,
                     gfin_ref, out_ref, ubuf_ref, gated_ref, act_ref, *, tiles_per_seq):
    tm = h_ref.shape[0]
    pad = SUBLANES

    @pl.when(pl.program_id(0) % tiles_per_seq == 0)
    def _():
        ubuf_ref[0:pad, :] = jnp.zeros((pad, D_MODEL), F32)

    @pl.when(pl.program_id(0) % tiles_per_seq != 0)
    def _():
        ubuf_ref[0:pad, :] = ubuf_ref[tm:tm + pad, :]

    h = h_ref[...]
    hn = _rms(h, gc_ref[...]).astype(BF16)
    for c in range(D_MODEL // CONV_CHUNK):
        cs = slice(c * CONV_CHUNK, (c + 1) * CONV_CHUNK)
        b_gate = _dot(hn, win_ref[:, c * CONV_CHUNK:(c + 1) * CONV_CHUNK])
        c_gate = _dot(hn, win_ref[:, D_MODEL + c * CONV_CHUNK:D_MODEL + (c + 1) * CONV_CHUNK])
        xp = _dot(hn, win_ref[:, 2 * D_MODEL + c * CONV_CHUNK:2 * D_MODEL + (c + 1) * CONV_CHUNK])
        u = c_gate * xp
        ubuf_ref[pad:pad + tm, cs] = u
        u_conv = (cw_ref[0:1, cs] * ubuf_ref[pad - 2:pad - 2 + tm, cs]
                  + cw_ref[1:2, cs] * ubuf_ref[pad - 1:pad - 1 + tm, cs]
                  + cw_ref[2:3, cs] * u)
        gated_ref[:, cs] = (b_gate * u_conv).astype(BF16)

    h1 = h + _dot(gated_ref[...], wout_ref[...])
    h2 = _swiglu_residual(h1, gf_ref[...], wg_ref, wu_ref, wd_ref, act_ref)
    out_ref[...] = _rms(h2, gfin_ref[...])


def _conv_ffn(h, gc, win, cw, wout, gf, wg, wu, wd, gfin, seq):
    t = h.shape[0]
    tm = TOKEN_TILE
    return pl.pallas_call(
        functools.partial(_conv_ffn_kernel, tiles_per_seq=seq // tm),
        grid=(t // tm,),
        in_specs=[
            pl.BlockSpec((tm, D_MODEL), lambda i: (i, 0)),
            _const_spec(gc.shape), _const_spec(win.shape), _const_spec(cw.shape),
            _const_spec(wout.shape), _const_spec(gf.shape), _const_spec(wg.shape),
            _const_spec(wu.shape), _const_spec(wd.shape), _const_spec(gfin.shape),
        ],
        out_specs=pl.BlockSpec((tm, D_MODEL), lambda i: (i, 0)),
        out_shape=jax.ShapeDtypeStruct((t, D_MODEL), F32),
        scratch_shapes=[
            pltpu.VMEM((tm + SUBLANES, D_MODEL), F32),
            pltpu.VMEM((tm, D_MODEL), BF16),
            pltpu.VMEM((tm, wg.shape[1]), BF16),
        ],
        compiler_params=_params(("arbitrary",)),
        name="conv_ffn",
    )(h, gc, win, cw, wout, gf, wg, wu, wd, gfin)


def _rope_inv_freq_lanes():
    inv = 1.0 / (ROPE_THETA ** (np.arange(0, QK_ROPE_DIM, 2, dtype=np.float32) / np.float32(QK_ROPE_DIM)))
    inv = inv.astype(np.float32)
    return jnp.asarray(np.tile(inv, LANES // (QK_ROPE_DIM // 2))[None, :])


def kernel(x, positions, mla_norm, mla_w_in, mla_g_cq, mla_g_ckv, mla_w_uq, mla_w_ukv, mla_w_o,
           conv_norm, conv_w_in, conv_w, conv_w_out, ffn_norm, ffn_w_gate, ffn_w_up, ffn_w_down,
           final_norm):
    bsz, seq, d = x.shape
    t = bsz * seq
    assert d == D_MODEL and seq % TOKEN_TILE == 0 and seq % ATTN_TILE == 0 and ATTN_TILE % CHUNK == 0
    x2 = x.reshape(t, d)
    pos = positions.reshape(t, 1).astype(F32)

    w_in = jnp.pad(mla_w_in[0], ((0, 0), (0, LANES - QK_ROPE_DIM))).astype(BF16)
    w_uq = mla_w_uq[0].reshape(Q_LORA_RANK, N_HEADS, QK_NOPE_DIM + QK_ROPE_DIM)
    wqn = w_uq[:, :, :QK_NOPE_DIM].reshape(Q_LORA_RANK, N_HEADS * QK_NOPE_DIM).astype(BF16)
    wqr = w_uq[:, :, QK_NOPE_DIM:].reshape(Q_LORA_RANK, N_HEADS * QK_ROPE_DIM).astype(BF16)
    w_ukv = mla_w_ukv[0].reshape(KV_LORA_RANK, N_HEADS, QK_NOPE_DIM + V_HEAD_DIM)
    wkn = w_ukv[:, :, :QK_NOPE_DIM].reshape(KV_LORA_RANK, N_HEADS * QK_NOPE_DIM).astype(BF16)
    wv = w_ukv[:, :, QK_NOPE_DIM:].reshape(KV_LORA_RANK, N_HEADS * V_HEAD_DIM).astype(BF16)

    qn, qr, kn, kr, v = _mla_pre(
        x2, pos, mla_norm[0][None, :], w_in, mla_g_cq[0][None, :], mla_g_ckv[0][None, :],
        wqn, wqr, wkn, wv, _rope_inv_freq_lanes())
    attn = _attention(qn, qr, kn, kr, v, bsz, seq)
    h = _wo_ffn(attn, x2, mla_w_o[0].astype(BF16), ffn_norm[0][None, :],
                ffn_w_gate[0].astype(BF16), ffn_w_up[0].astype(BF16), ffn_w_down[0].astype(BF16))
    out = _conv_ffn(h, conv_norm[0][None, :], conv_w_in[0].astype(BF16), conv_w[0],
                    conv_w_out[0].astype(BF16), ffn_norm[1][None, :],
                    ffn_w_gate[1].astype(BF16), ffn_w_up[1].astype(BF16), ffn_w_down[1].astype(BF16),
                    final_norm[None, :], seq)
    return out.reshape(bsz, seq, d)
```

```python
import functools
import math

import jax
import jax.numpy as jnp
import numpy as np
from jax import lax
from jax.experimental import pallas as pl
from jax.experimental.pallas import tpu as pltpu

D_MODEL = 1024
CHUNK = 64
N_HEADS = 8
QK_NOPE_DIM = 128
QK_ROPE_DIM = 64
V_HEAD_DIM = 128
Q_LORA_RANK = 512
KV_LORA_RANK = 256
ROPE_THETA = 10000.0
CONV_WIDTH = 3
RMS_EPS = 1e-6

LANES = 128
SUBLANES = 8
MXU_COLS = 256
VMEM_LIMIT_BYTES = 56 * 1024 * 1024

TOKEN_TILE = 512
ATTN_TILE = 512
FF_CHUNK = MXU_COLS
CONV_CHUNK = MXU_COLS

BF16 = jnp.bfloat16
F32 = jnp.float32


def _dot(a, b):
    return jnp.dot(a, b, preferred_element_type=F32)


def _rms(x, g):
    return x * lax.rsqrt(jnp.mean(x * x, axis=-1, keepdims=True) + RMS_EPS) * g


def _const_spec(shape):
    return pl.BlockSpec(shape, lambda *_: (0,) * len(shape), pipeline_mode=pl.Buffered(1))


def _params(semantics):
    return pltpu.CompilerParams(dimension_semantics=semantics, vmem_limit_bytes=VMEM_LIMIT_BYTES)


def _mla_pre_kernel(x_ref, pos_ref, g_ref, w_in_ref, gcq_ref, gckv_ref, wqn_ref, wqr_ref,
                    wkn_ref, wv_ref, invf_ref, qn_ref, qr_ref, kn_ref, kr_ref, v_ref, *, scale):
    tm = x_ref.shape[0]
    xn = _rms(x_ref[...], g_ref[...]).astype(BF16)
    proj = _dot(xn, w_in_ref[...])
    cq = _rms(proj[:, :Q_LORA_RANK], gcq_ref[...]).astype(BF16)
    ckv = _rms(proj[:, Q_LORA_RANK:Q_LORA_RANK + KV_LORA_RANK], gckv_ref[...]).astype(BF16)
    kr = proj[:, Q_LORA_RANK + KV_LORA_RANK:]

    ang = pos_ref[...] * invf_ref[...]
    cos = jnp.cos(ang)
    sin = jnp.sin(ang)
    lane = lax.broadcasted_iota(jnp.int32, (tm, LANES), 1)
    first_half = (lane % QK_ROPE_DIM) < (QK_ROPE_DIM // 2)
    low_block = lane < QK_ROPE_DIM
    sin_signed = jnp.where(first_half, -sin, sin)

    def rope(t):
        partner = jnp.where(first_half,
                            pltpu.roll(t, LANES - QK_ROPE_DIM // 2, 1),
                            pltpu.roll(t, QK_ROPE_DIM // 2, 1))
        return t * cos + partner * sin_signed

    kr_ref[...] = rope(kr).astype(BF16)

    qn = _dot(cq, wqn_ref[...]) * scale
    for h in range(N_HEADS):
        qn_ref[h] = qn[:, h * LANES:(h + 1) * LANES].astype(BF16)

    qr = _dot(cq, wqr_ref[...])
    for p in range(N_HEADS // 2):
        grp = rope(qr[:, p * LANES:(p + 1) * LANES]) * scale
        qr_ref[2 * p] = jnp.where(low_block, grp, 0.0).astype(BF16)
        qr_ref[2 * p + 1] = jnp.where(low_block, pltpu.roll(grp, QK_ROPE_DIM, 1), 0.0).astype(BF16)

    kn = _dot(ckv, wkn_ref[...])
    vv = _dot(ckv, wv_ref[...])
    for h in range(N_HEADS):
        kn_ref[h] = kn[:, h * LANES:(h + 1) * LANES].astype(BF16)
        v_ref[h] = vv[:, h * LANES:(h + 1) * LANES].astype(BF16)


def _mla_pre(x2, pos, g, w_in, gcq, gckv, wqn, wqr, wkn, wv, invf):
    t = x2.shape[0]
    tm = TOKEN_TILE
    scale = math.log2(math.e) / math.sqrt(QK_NOPE_DIM + QK_ROPE_DIM)
    head_out = jax.ShapeDtypeStruct((N_HEADS, t, LANES), BF16)
    head_spec = pl.BlockSpec((N_HEADS, tm, LANES), lambda i: (0, i, 0))
    return pl.pallas_call(
        functools.partial(_mla_pre_kernel, scale=scale),
        grid=(t // tm,),
        in_specs=[
            pl.BlockSpec((tm, D_MODEL), lambda i: (i, 0)),
            pl.BlockSpec((tm, 1), lambda i: (i, 0)),
            _const_spec(g.shape), _const_spec(w_in.shape), _const_spec(gcq.shape),
            _const_spec(gckv.shape), _const_spec(wqn.shape), _const_spec(wqr.shape),
            _const_spec(wkn.shape), _const_spec(wv.shape), _const_spec(invf.shape),
        ],
        out_specs=[head_spec, head_spec, head_spec,
                   pl.BlockSpec((tm, LANES), lambda i: (i, 0)), head_spec],
        out_shape=[head_out, head_out, head_out, jax.ShapeDtypeStruct((t, LANES), BF16), head_out],
        compiler_params=_params(("parallel",)),
        name="mla_pre",
    )(x2, pos, g, w_in, gcq, gckv, wqn, wqr, wkn, wv, invf)


def _attn_kernel(qn_ref, qr_ref, kn_ref, kr_ref, v_ref, o_ref, m_a, m_b, acc_a, acc_b):
    seq = qn_ref.shape[0]
    ta = ATTN_TILE
    groups = ta // LANES
    row_chunk = lax.broadcasted_iota(jnp.int32, (ta, ta), 0) // CHUNK
    col_chunk = lax.broadcasted_iota(jnp.int32, (ta, ta), 1) // CHUNK
    diag_mask = col_chunk <= row_chunk
    ones = jnp.ones((ta, LANES), BF16)

    def tile_slice(i):
        return pl.ds(i * ta, ta)

    def kv_tile(j):
        ks = tile_slice(j)
        k = jnp.concatenate([kn_ref[ks, :], kr_ref[ks, :]], axis=1)
        v_ext = jnp.concatenate([v_ref[ks, :], ones], axis=1)
        return k, v_ext

    def step(qs, m_ref, acc_ref, k, v_ext, masked):
        q = jnp.concatenate([qn_ref[qs, :], qr_ref[qs, :]], axis=1)
        s = lax.dot_general(q, k, (((1,), (1,)), ((), ())), preferred_element_type=F32)
        if masked:
            s = jnp.where(diag_mask, s, -1e30)
        parts = [s[:, g * LANES:(g + 1) * LANES] for g in range(groups)]
        m_prev = m_ref[...]
        m_new = jnp.maximum(m_prev, jnp.max(functools.reduce(jnp.maximum, parts), axis=1, keepdims=True))
        alpha = jnp.exp2(m_prev - m_new)
        p = jnp.concatenate([jnp.exp2(part - m_new).astype(BF16) for part in parts], axis=1)
        acc_ref[...] = acc_ref[...] * jnp.concatenate([alpha, alpha], axis=1) + _dot(p, v_ext)
        m_ref[...] = m_new

    def finish(qs, acc_ref):
        acc = acc_ref[...]
        o_ref[qs, :] = (acc[:, :V_HEAD_DIM] / acc[:, V_HEAD_DIM:]).astype(o_ref.dtype)

    def q_pair(i, _):
        qs_a = tile_slice(2 * i)
        qs_b = tile_slice(2 * i + 1)
        for m_ref, acc_ref in ((m_a, acc_a), (m_b, acc_b)):
            m_ref[...] = jnp.full(m_ref.shape, -1e30, F32)
            acc_ref[...] = jnp.zeros(acc_ref.shape, F32)

        def kv_step(j, _):
            k, v_ext = kv_tile(j)
            step(qs_a, m_a, acc_a, k, v_ext, False)
            step(qs_b, m_b, acc_b, k, v_ext, False)
            return 0

        for j in range(2 * i):
            kv_step(j, 0)
        k, v_ext = kv_tile(2 * i)
        step(qs_a, m_a, acc_a, k, v_ext, True)
        step(qs_b, m_b, acc_b, k, v_ext, False)
        finish(qs_a, acc_a)
        k, v_ext = kv_tile(2 * i + 1)
        step(qs_b, m_b, acc_b, k, v_ext, True)
        finish(qs_b, acc_b)
        return 0

    for i in range(seq // (2 * ta)):
        q_pair(i, 0)


def _attention(qn, qr, kn, kr, v, bsz, seq):
    t = bsz * seq
    ta = ATTN_TILE
    head_spec = pl.BlockSpec((None, seq, LANES), lambda b, h: (h, b, 0))
    return pl.pallas_call(
        _attn_kernel,
        grid=(bsz, N_HEADS),
        in_specs=[head_spec, head_spec, head_spec,
                  pl.BlockSpec((seq, LANES), lambda b, h: (b, 0)), head_spec],
        out_specs=head_spec,
        out_shape=jax.ShapeDtypeStruct((N_HEADS, t, LANES), BF16),
        scratch_shapes=[pltpu.VMEM((ta, LANES), F32), pltpu.VMEM((ta, LANES), F32),
                        pltpu.VMEM((ta, 2 * LANES), F32), pltpu.VMEM((ta, 2 * LANES), F32)],
        compiler_params=_params(("parallel", "parallel")),
        name="attention",
    )(qn, qr, kn, kr, v)


def _swiglu_residual(h1, g, wg_ref, wu_ref, wd_ref, act_ref):
    d_ff = wg_ref.shape[1]
    hn = _rms(h1, g).astype(BF16)
    for c in range(d_ff // FF_CHUNK):
        cs = slice(c * FF_CHUNK, (c + 1) * FF_CHUNK)
        gate = _dot(hn, wg_ref[:, cs])
        up = _dot(hn, wu_ref[:, cs])
        act_ref[:, cs] = (jax.nn.silu(gate) * up).astype(BF16)
    return h1 + _dot(act_ref[...], wd_ref[...])


def _wo_ffn_kernel(a_ref, h_ref, wo_ref, g_ref, wg_ref, wu_ref, wd_ref, out_ref, act_ref):
    a = jnp.concatenate([a_ref[h] for h in range(N_HEADS)], axis=1)
    h1 = h_ref[...] + _dot(a, wo_ref[...])
    out_ref[...] = _swiglu_residual(h1, g_ref[...], wg_ref, wu_ref, wd_ref, act_ref)


def _wo_ffn(attn, h, wo, g, wg, wu, wd):
    t = h.shape[0]
    tm = TOKEN_TILE
    return pl.pallas_call(
        _wo_ffn_kernel,
        grid=(t // tm,),
        in_specs=[
            pl.BlockSpec((N_HEADS, tm, LANES), lambda i: (0, i, 0)),
            pl.BlockSpec((tm, D_MODEL), lambda i: (i, 0)),
            _const_spec(wo.shape), _const_spec(g.shape), _const_spec(wg.shape),
            _const_spec(wu.shape), _const_spec(wd.shape),
        ],
        out_specs=pl.BlockSpec((tm, D_MODEL), lambda i: (i, 0)),
        out_shape=jax.ShapeDtypeStruct((t, D_MODEL), F32),
        scratch_shapes=[pltpu.VMEM((tm, wg.shape[1]), BF16)],
        compiler_params=_params(("parallel",)),
        name="wo_ffn",
    )(attn, h, wo, g, wg, wu, wd)


def _conv_ffn_kernel(h_ref, gc_ref, win_ref, cw_ref, wout_ref, gf_ref, wg_ref, wu_ref, wd_ref,
                     gfin_ref, out_ref, ubuf_ref, gated_ref, act_ref, *, tiles_per_seq):
    tm = h_ref.shape[0]
    pad = SUBLANES

    @pl.when(pl.program_id(0) % tiles_per_seq == 0)
    def _():
        ubuf_ref[0:pad, :] = jnp.zeros((pad, D_MODEL), F32)

    @pl.when(pl.program_id(0) % tiles_per_seq != 0)
    def _():
        ubuf_ref[0:pad, :] = ubuf_ref[tm:tm + pad, :]

    h = h_ref[...]
    hn = _rms(h, gc_ref[...]).astype(BF16)
    for c in range(D_MODEL // CONV_CHUNK):
        cs = slice(c * CONV_CHUNK, (c + 1) * CONV_CHUNK)
        b_gate = _dot(hn, win_ref[:, c * CONV_CHUNK:(c + 1) * CONV_CHUNK])
        c_gate = _dot(hn, win_ref[:, D_MODEL + c * CONV_CHUNK:D_MODEL + (c + 1) * CONV_CHUNK])
        xp = _dot(hn, win_ref[:, 2 * D_MODEL + c * CONV_CHUNK:2 * D_MODEL + (c + 1) * CONV_CHUNK])
        u = c_gate * xp
        ubuf_ref[pad:pad + tm, cs] = u
        u_conv = (cw_ref[0:1, cs] * ubuf_ref[pad - 2:pad - 2 + tm, cs]
                  + cw_ref[1:2, cs] * ubuf_ref[pad - 1:pad - 1 + tm, cs]
                  + cw_ref[2:3, cs] * u)
        gated_ref[:, cs] = (b_gate * u_conv).astype(BF16)

    h1 = h + _dot(gated_ref[...], wout_ref[...])
    h2 = _swiglu_residual(h1, gf_ref[...], wg_ref, wu_ref, wd_ref, act_ref)
    out_ref[...] = _rms(h2, gfin_ref[...])


def _conv_ffn(h, gc, win, cw, wout, gf, wg, wu, wd, gfin, seq):
    t = h.shape[0]
    tm = TOKEN_TILE
    return pl.pallas_call(
        functools.partial(_conv_ffn_kernel, tiles_per_seq=seq // tm),
        grid=(t // tm,),
        in_specs=[
            pl.BlockSpec((tm, D_MODEL), lambda i: (i, 0)),
            _const_spec(gc.shape), _const_spec(win.shape), _const_spec(cw.shape),
            _const_spec(wout.shape), _const_spec(gf.shape), _const_spec(wg.shape),
            _const_spec(wu.shape), _const_spec(wd.shape), _const_spec(gfin.shape),
        ],
        out_specs=pl.BlockSpec((tm, D_MODEL), lambda i: (i, 0)),
        out_shape=jax.ShapeDtypeStruct((t, D_MODEL), F32),
        scratch_shapes=[
            pltpu.VMEM((tm + SUBLANES, D_MODEL), F32),
            pltpu.VMEM((tm, D_MODEL), BF16),
            pltpu.VMEM((tm, wg.shape[1]), BF16),
        ],
        compiler_params=_params(("arbitrary",)),
        name="conv_ffn",
    )(h, gc, win, cw, wout, gf, wg, wu, wd, gfin)


def _rope_inv_freq_lanes():
    inv = 1.0 / (ROPE_THETA ** (np.arange(0, QK_ROPE_DIM, 2, dtype=np.float32) / np.float32(QK_ROPE_DIM)))
    inv = inv.astype(np.float32)
    return jnp.asarray(np.tile(inv, LANES // (QK_ROPE_DIM // 2))[None, :])


def kernel(x, positions, mla_norm, mla_w_in, mla_g_cq, mla_g_ckv, mla_w_uq, mla_w_ukv, mla_w_o,
           conv_norm, conv_w_in, conv_w, conv_w_out, ffn_norm, ffn_w_gate, ffn_w_up, ffn_w_down,
           final_norm):
    bsz, seq, d = x.shape
    t = bsz * seq
    assert d == D_MODEL and seq % TOKEN_TILE == 0 and seq % (2 * ATTN_TILE) == 0 and ATTN_TILE % CHUNK == 0
    x2 = x.reshape(t, d)
    pos = positions.reshape(t, 1).astype(F32)

    w_in = jnp.pad(mla_w_in[0], ((0, 0), (0, LANES - QK_ROPE_DIM))).astype(BF16)
    w_uq = mla_w_uq[0].reshape(Q_LORA_RANK, N_HEADS, QK_NOPE_DIM + QK_ROPE_DIM)
    wqn = w_uq[:, :, :QK_NOPE_DIM].reshape(Q_LORA_RANK, N_HEADS * QK_NOPE_DIM).astype(BF16)
    wqr = w_uq[:, :, QK_NOPE_DIM:].reshape(Q_LORA_RANK, N_HEADS * QK_ROPE_DIM).astype(BF16)
    w_ukv = mla_w_ukv[0].reshape(KV_LORA_RANK, N_HEADS, QK_NOPE_DIM + V_HEAD_DIM)
    wkn = w_ukv[:, :, :QK_NOPE_DIM].reshape(KV_LORA_RANK, N_HEADS * QK_NOPE_DIM).astype(BF16)
    wv = w_ukv[:, :, QK_NOPE_DIM:].reshape(KV_LORA_RANK, N_HEADS * V_HEAD_DIM).astype(BF16)

    qn, qr, kn, kr, v = _mla_pre(
        x2, pos, mla_norm[0][None, :], w_in, mla_g_cq[0][None, :], mla_g_ckv[0][None, :],
        wqn, wqr, wkn, wv, _rope_inv_freq_lanes())
    attn = _attention(qn, qr, kn, kr, v, bsz, seq)
    h = _wo_ffn(attn, x2, mla_w_o[0].astype(BF16), ffn_norm[0][None, :],
                ffn_w_gate[0].astype(BF16), ffn_w_up[0].astype(BF16), ffn_w_down[0].astype(BF16))
    out = _conv_ffn(h, conv_norm[0][None, :], conv_w_in[0].astype(BF16), conv_w[0],
                    conv_w_out[0].astype(BF16), ffn_norm[1][None, :],
                    ffn_w_gate[1].astype(BF16), ffn_w_up[1].astype(BF16), ffn_w_down[1].astype(BF16),
                    final_norm[None, :], seq)
    return out.reshape(bsz, seq, d)
```

```python
import functools
import math

import jax
import jax.numpy as jnp
import numpy as np
from jax import lax
from jax.experimental import pallas as pl
from jax.experimental.pallas import tpu as pltpu

D_MODEL = 1024
CHUNK = 64
N_HEADS = 8
QK_NOPE_DIM = 128
QK_ROPE_DIM = 64
V_HEAD_DIM = 128
Q_LORA_RANK = 512
KV_LORA_RANK = 256
ROPE_THETA = 10000.0
CONV_WIDTH = 3
RMS_EPS = 1e-6

LANES = 128
SUBLANES = 8
BF16_SUBLANES = 16
MXU_COLS = 256
VMEM_LIMIT_BYTES = 56 * 1024 * 1024

TOKEN_TILE = 512
ATTN_TILE = 512
FF_CHUNK = MXU_COLS
CONV_CHUNK = MXU_COLS

BF16 = jnp.bfloat16
F32 = jnp.float32


def _dot(a, b):
    return jnp.dot(a, b, preferred_element_type=F32)


def _rms(x, g):
    return x * lax.rsqrt(jnp.mean(x * x, axis=-1, keepdims=True) + RMS_EPS) * g


def _const_spec(shape):
    return pl.BlockSpec(shape, lambda *_: (0,) * len(shape), pipeline_mode=pl.Buffered(1))


def _params(semantics):
    return pltpu.CompilerParams(dimension_semantics=semantics, vmem_limit_bytes=VMEM_LIMIT_BYTES)


def _cast_specs(w, layer, steps):
    _, rows, cols = w.shape
    out_shape = jax.ShapeDtypeStruct((rows, cols), BF16)
    if rows % (steps * BF16_SUBLANES) == 0:
        rb = rows // steps
        return (pl.BlockSpec((None, rb, cols), lambda i: (layer, i, 0)),
                pl.BlockSpec((rb, cols), lambda i: (i, 0)), out_shape, 1)
    per = steps // (cols // LANES)
    return (pl.BlockSpec((None, rows, LANES), lambda i: (layer, 0, i // per)),
            pl.BlockSpec((rows, LANES), lambda i: (0, i // per)), out_shape, per)


def _cast_side_work(cast_in_refs, cast_out_refs, periods):
    for src, dst, period in zip(cast_in_refs, cast_out_refs, periods):
        @pl.when(pl.program_id(0) % period == 0)
        def _(src=src, dst=dst):
            dst[...] = src[...].astype(BF16)


def _mla_pre_kernel(*refs, scale, cast_periods):
    n_cast = len(cast_periods)
    (x_ref, pos_ref, g_ref, w_in_ref, gcq_ref, gckv_ref, wqn_ref, wqr_ref,
     wkn_ref, wv_ref, invf_ref) = refs[:11]
    qn_ref, qr_ref, kn_ref, kr_ref, v_ref = refs[11 + n_cast:16 + n_cast]
    _cast_side_work(refs[11:11 + n_cast], refs[16 + n_cast:], cast_periods)
    tm = x_ref.shape[0]
    xn = _rms(x_ref[...], g_ref[...]).astype(BF16)
    proj = _dot(xn, w_in_ref[...])
    cq = _rms(proj[:, :Q_LORA_RANK], gcq_ref[...]).astype(BF16)
    ckv = _rms(proj[:, Q_LORA_RANK:Q_LORA_RANK + KV_LORA_RANK], gckv_ref[...]).astype(BF16)
    kr = proj[:, Q_LORA_RANK + KV_LORA_RANK:]

    ang = pos_ref[...] * invf_ref[...]
    cos = jnp.cos(ang)
    sin = jnp.sin(ang)
    lane = lax.broadcasted_iota(jnp.int32, (tm, LANES), 1)
    first_half = (lane % QK_ROPE_DIM) < (QK_ROPE_DIM // 2)
    low_block = lane < QK_ROPE_DIM
    sin_signed = jnp.where(first_half, -sin, sin)

    def rope(t):
        partner = jnp.where(first_half,
                            pltpu.roll(t, LANES - QK_ROPE_DIM // 2, 1),
                            pltpu.roll(t, QK_ROPE_DIM // 2, 1))
        return t * cos + partner * sin_signed

    kr_ref[...] = rope(kr).astype(BF16)

    qn = _dot(cq, wqn_ref[...]) * scale
    for h in range(N_HEADS):
        qn_ref[h] = qn[:, h * LANES:(h + 1) * LANES].astype(BF16)

    qr = _dot(cq, wqr_ref[...])
    for p in range(N_HEADS // 2):
        grp = rope(qr[:, p * LANES:(p + 1) * LANES]) * scale
        qr_ref[2 * p] = jnp.where(low_block, grp, 0.0).astype(BF16)
        qr_ref[2 * p + 1] = jnp.where(low_block, pltpu.roll(grp, QK_ROPE_DIM, 1), 0.0).astype(BF16)

    kn = _dot(ckv, wkn_ref[...])
    vv = _dot(ckv, wv_ref[...])
    for h in range(N_HEADS):
        kn_ref[h] = kn[:, h * LANES:(h + 1) * LANES].astype(BF16)
        v_ref[h] = vv[:, h * LANES:(h + 1) * LANES].astype(BF16)


def _mla_pre(x2, pos, g, w_in, gcq, gckv, wqn, wqr, wkn, wv, invf, casts):
    t = x2.shape[0]
    tm = TOKEN_TILE
    cast_specs = [_cast_specs(w, layer, t // tm) for w, layer in casts]
    scale = math.log2(math.e) / math.sqrt(QK_NOPE_DIM + QK_ROPE_DIM)
    head_out = jax.ShapeDtypeStruct((N_HEADS, t, LANES), BF16)
    head_spec = pl.BlockSpec((N_HEADS, tm, LANES), lambda i: (0, i, 0))
    return pl.pallas_call(
        functools.partial(_mla_pre_kernel, scale=scale, cast_periods=tuple(c[3] for c in cast_specs)),
        grid=(t // tm,),
        in_specs=[
            pl.BlockSpec((tm, D_MODEL), lambda i: (i, 0)),
            pl.BlockSpec((tm, 1), lambda i: (i, 0)),
            _const_spec(g.shape), _const_spec(w_in.shape), _const_spec(gcq.shape),
            _const_spec(gckv.shape), _const_spec(wqn.shape), _const_spec(wqr.shape),
            _const_spec(wkn.shape), _const_spec(wv.shape), _const_spec(invf.shape),
        ] + [c[0] for c in cast_specs],
        out_specs=[head_spec, head_spec, head_spec,
                   pl.BlockSpec((tm, LANES), lambda i: (i, 0)), head_spec] + [c[1] for c in cast_specs],
        out_shape=[head_out, head_out, head_out, jax.ShapeDtypeStruct((t, LANES), BF16), head_out]
                  + [c[2] for c in cast_specs],
        compiler_params=_params(("parallel",)),
        name="mla_pre",
    )(x2, pos, g, w_in, gcq, gckv, wqn, wqr, wkn, wv, invf, *[w for w, _ in casts])


def _attn_kernel(qn_ref, qr_ref, kn_ref, kr_ref, v_ref, o_ref, m_a, m_b, acc_a, acc_b):
    seq = qn_ref.shape[0]
    ta = ATTN_TILE
    groups = ta // LANES
    row_chunk = lax.broadcasted_iota(jnp.int32, (ta, ta), 0) // CHUNK
    col_chunk = lax.broadcasted_iota(jnp.int32, (ta, ta), 1) // CHUNK
    diag_mask = col_chunk <= row_chunk
    ones = jnp.ones((ta, LANES), BF16)

    def tile_slice(i):
        return pl.ds(i * ta, ta)

    def kv_tile(j):
        ks = tile_slice(j)
        k = jnp.concatenate([kn_ref[ks, :], kr_ref[ks, :]], axis=1)
        v_ext = jnp.concatenate([v_ref[ks, :], ones], axis=1)
        return k, v_ext

    def step(qs, m_ref, acc_ref, k, v_ext, masked):
        q = jnp.concatenate([qn_ref[qs, :], qr_ref[qs, :]], axis=1)
        s = lax.dot_general(q, k, (((1,), (1,)), ((), ())), preferred_element_type=F32)
        if masked:
            s = jnp.where(diag_mask, s, -1e30)
        parts = [s[:, g * LANES:(g + 1) * LANES] for g in range(groups)]
        m_prev = m_ref[...]
        m_new = jnp.maximum(m_prev, jnp.max(functools.reduce(jnp.maximum, parts), axis=1, keepdims=True))
        alpha = jnp.exp2(m_prev - m_new)
        p = jnp.concatenate([jnp.exp2(part - m_new).astype(BF16) for part in parts], axis=1)
        acc_ref[...] = acc_ref[...] * jnp.concatenate([alpha, alpha], axis=1) + _dot(p, v_ext)
        m_ref[...] = m_new

    def finish(qs, acc_ref):
        acc = acc_ref[...]
        o_ref[qs, :] = (acc[:, :V_HEAD_DIM] / acc[:, V_HEAD_DIM:]).astype(o_ref.dtype)

    def q_pair(i, _):
        qs_a = tile_slice(2 * i)
        qs_b = tile_slice(2 * i + 1)
        for m_ref, acc_ref in ((m_a, acc_a), (m_b, acc_b)):
            m_ref[...] = jnp.full(m_ref.shape, -1e30, F32)
            acc_ref[...] = jnp.zeros(acc_ref.shape, F32)

        def kv_step(j, _):
            k, v_ext = kv_tile(j)
            step(qs_a, m_a, acc_a, k, v_ext, False)
            step(qs_b, m_b, acc_b, k, v_ext, False)
            return 0

        for j in range(2 * i):
            kv_step(j, 0)
        k, v_ext = kv_tile(2 * i)
        step(qs_a, m_a, acc_a, k, v_ext, True)
        step(qs_b, m_b, acc_b, k, v_ext, False)
        finish(qs_a, acc_a)
        k, v_ext = kv_tile(2 * i + 1)
        step(qs_b, m_b, acc_b, k, v_ext, True)
        finish(qs_b, acc_b)
        return 0

    for i in range(seq // (2 * ta)):
        q_pair(i, 0)


def _attention(qn, qr, kn, kr, v, bsz, seq):
    t = bsz * seq
    ta = ATTN_TILE
    head_spec = pl.BlockSpec((None, seq, LANES), lambda b, h: (h, b, 0))
    return pl.pallas_call(
        _attn_kernel,
        grid=(bsz, N_HEADS),
        in_specs=[head_spec, head_spec, head_spec,
                  pl.BlockSpec((seq, LANES), lambda b, h: (b, 0)), head_spec],
        out_specs=head_spec,
        out_shape=jax.ShapeDtypeStruct((N_HEADS, t, LANES), BF16),
        scratch_shapes=[pltpu.VMEM((ta, LANES), F32), pltpu.VMEM((ta, LANES), F32),
                        pltpu.VMEM((ta, 2 * LANES), F32), pltpu.VMEM((ta, 2 * LANES), F32)],
        compiler_params=_params(("parallel", "parallel")),
        name="attention",
    )(qn, qr, kn, kr, v)


def _swiglu_residual(h1, g, wg_ref, wu_ref, wd_ref, act_ref):
    d_ff = wg_ref.shape[1]
    hn = _rms(h1, g).astype(BF16)
    for c in range(d_ff // FF_CHUNK):
        cs = slice(c * FF_CHUNK, (c + 1) * FF_CHUNK)
        gate = _dot(hn, wg_ref[:, cs])
        up = _dot(hn, wu_ref[:, cs])
        act_ref[:, cs] = (jax.nn.silu(gate) * up).astype(BF16)
    return h1 + _dot(act_ref[...], wd_ref[...])


def _wo_ffn_kernel(*refs, cast_periods):
    n_cast = len(cast_periods)
    a_ref, h_ref, wo_ref, g_ref, wg_ref, wu_ref, wd_ref = refs[:7]
    out_ref = refs[7 + n_cast]
    act_ref = refs[-1]
    _cast_side_work(refs[7:7 + n_cast], refs[8 + n_cast:-1], cast_periods)
    a = jnp.concatenate([a_ref[h] for h in range(N_HEADS)], axis=1)
    h1 = h_ref[...] + _dot(a, wo_ref[...])
    out_ref[...] = _swiglu_residual(h1, g_ref[...], wg_ref, wu_ref, wd_ref, act_ref)


def _wo_ffn(attn, h, wo, g, wg, wu, wd, casts):
    t = h.shape[0]
    tm = TOKEN_TILE
    cast_specs = [_cast_specs(w, layer, t // tm) for w, layer in casts]
    return pl.pallas_call(
        functools.partial(_wo_ffn_kernel, cast_periods=tuple(c[3] for c in cast_specs)),
        grid=(t // tm,),
        in_specs=[
            pl.BlockSpec((N_HEADS, tm, LANES), lambda i: (0, i, 0)),
            pl.BlockSpec((tm, D_MODEL), lambda i: (i, 0)),
            _const_spec(wo.shape), _const_spec(g.shape), _const_spec(wg.shape),
            _const_spec(wu.shape), _const_spec(wd.shape),
        ] + [c[0] for c in cast_specs],
        out_specs=[pl.BlockSpec((tm, D_MODEL), lambda i: (i, 0))] + [c[1] for c in cast_specs],
        out_shape=[jax.ShapeDtypeStruct((t, D_MODEL), F32)] + [c[2] for c in cast_specs],
        scratch_shapes=[pltpu.VMEM((tm, wg.shape[1]), BF16)],
        compiler_params=_params(("parallel",)),
        name="wo_ffn",
    )(attn, h, wo, g, wg, wu, wd, *[w for w, _ in casts])


def _conv_ffn_kernel(h_ref, gc_ref, win_ref, cw_ref, wout_ref, gf_ref, wg_ref, wu_ref, wd_ref,
                     gfin_ref, out_ref, ubuf_ref, gated_ref, act_ref, *, tiles_per_seq):
    tm = h_ref.shape[0]
    pad = SUBLANES

    @pl.when(pl.program_id(0) % tiles_per_seq == 0)
    def _():
        ubuf_ref[0:pad, :] = jnp.zeros((pad, D_MODEL), F32)

    @pl.when(pl.program_id(0) % tiles_per_seq != 0)
    def _():
        ubuf_ref[0:pad, :] = ubuf_ref[tm:tm + pad, :]

    h = h_ref[...]
    hn = _rms(h, gc_ref[...]).astype(BF16)
    for c in range(D_MODEL // CONV_CHUNK):
        cs = slice(c * CONV_CHUNK, (c + 1) * CONV_CHUNK)
        b_gate = _dot(hn, win_ref[:, c * CONV_CHUNK:(c + 1) * CONV_CHUNK])
        c_gate = _dot(hn, win_ref[:, D_MODEL + c * CONV_CHUNK:D_MODEL + (c + 1) * CONV_CHUNK])
        xp = _dot(hn, win_ref[:, 2 * D_MODEL + c * CONV_CHUNK:2 * D_MODEL + (c + 1) * CONV_CHUNK])
        u = c_gate * xp
        ubuf_ref[pad:pad + tm, cs] = u
        u_conv = (cw_ref[0:1, cs] * ubuf_ref[pad - 2:pad - 2 + tm, cs]
                  + cw_ref[1:2, cs] * ubuf_ref[pad - 1:pad - 1 + tm, cs]
                  + cw_ref[2:3, cs] * u)
        gated_ref[:, cs] = (b_gate * u_conv).astype(BF16)

    h1 = h + _dot(gated_ref[...], wout_ref[...])
    h2 = _swiglu_residual(h1, gf_ref[...], wg_ref, wu_ref, wd_ref, act_ref)
    out_ref[...] = _rms(h2, gfin_ref[...])


def _conv_ffn(h, gc, win, cw, wout, gf, wg, wu, wd, gfin, seq):
    t = h.shape[0]
    tm = TOKEN_TILE
    return pl.pallas_call(
        functools.partial(_conv_ffn_kernel, tiles_per_seq=seq // tm),
        grid=(t // tm,),
        in_specs=[
            pl.BlockSpec((tm, D_MODEL), lambda i: (i, 0)),
            _const_spec(gc.shape), _const_spec(win.shape), _const_spec(cw.shape),
            _const_spec(wout.shape), _const_spec(gf.shape), _const_spec(wg.shape),
            _const_spec(wu.shape), _const_spec(wd.shape), _const_spec(gfin.shape),
        ],
        out_specs=pl.BlockSpec((tm, D_MODEL), lambda i: (i, 0)),
        out_shape=jax.ShapeDtypeStruct((t, D_MODEL), F32),
        scratch_shapes=[
            pltpu.VMEM((tm + SUBLANES, D_MODEL), F32),
            pltpu.VMEM((tm, D_MODEL), BF16),
            pltpu.VMEM((tm, wg.shape[1]), BF16),
        ],
        compiler_params=_params(("arbitrary",)),
        name="conv_ffn",
    )(h, gc, win, cw, wout, gf, wg, wu, wd, gfin)


def _rope_inv_freq_lanes():
    inv = 1.0 / (ROPE_THETA ** (np.arange(0, QK_ROPE_DIM, 2, dtype=np.float32) / np.float32(QK_ROPE_DIM)))
    inv = inv.astype(np.float32)
    return jnp.asarray(np.tile(inv, LANES // (QK_ROPE_DIM // 2))[None, :])


def kernel(x, positions, mla_norm, mla_w_in, mla_g_cq, mla_g_ckv, mla_w_uq, mla_w_ukv, mla_w_o,
           conv_norm, conv_w_in, conv_w, conv_w_out, ffn_norm, ffn_w_gate, ffn_w_up, ffn_w_down,
           final_norm):
    bsz, seq, d = x.shape
    t = bsz * seq
    assert d == D_MODEL and seq % TOKEN_TILE == 0 and seq % (2 * ATTN_TILE) == 0 and ATTN_TILE % CHUNK == 0
    x2 = x.reshape(t, d)
    pos = positions.reshape(t, 1).astype(F32)

    w_in = jnp.pad(mla_w_in[0], ((0, 0), (0, LANES - QK_ROPE_DIM))).astype(BF16)
    w_uq = mla_w_uq[0].reshape(Q_LORA_RANK, N_HEADS, QK_NOPE_DIM + QK_ROPE_DIM)
    wqn = w_uq[:, :, :QK_NOPE_DIM].reshape(Q_LORA_RANK, N_HEADS * QK_NOPE_DIM).astype(BF16)
    wqr = w_uq[:, :, QK_NOPE_DIM:].reshape(Q_LORA_RANK, N_HEADS * QK_ROPE_DIM).astype(BF16)
    w_ukv = mla_w_ukv[0].reshape(KV_LORA_RANK, N_HEADS, QK_NOPE_DIM + V_HEAD_DIM)
    wkn = w_ukv[:, :, :QK_NOPE_DIM].reshape(KV_LORA_RANK, N_HEADS * QK_NOPE_DIM).astype(BF16)
    wv = w_ukv[:, :, QK_NOPE_DIM:].reshape(KV_LORA_RANK, N_HEADS * V_HEAD_DIM).astype(BF16)

    layer0 = [(mla_w_o, 0), (ffn_w_gate, 0), (ffn_w_up, 0), (ffn_w_down, 0)]
    layer1 = [(conv_w_in, 0), (conv_w_out, 0), (ffn_w_gate, 1), (ffn_w_up, 1), (ffn_w_down, 1)]
    qn, qr, kn, kr, v, wo_b, wg0_b, wu0_b, wd0_b = _mla_pre(
        x2, pos, mla_norm[0][None, :], w_in, mla_g_cq[0][None, :], mla_g_ckv[0][None, :],
        wqn, wqr, wkn, wv, _rope_inv_freq_lanes(), layer0)
    attn = _attention(qn, qr, kn, kr, v, bsz, seq)
    h, cwin_b, cwout_b, wg1_b, wu1_b, wd1_b = _wo_ffn(
        attn, x2, wo_b, ffn_norm[0][None, :], wg0_b, wu0_b, wd0_b, layer1)
    out = _conv_ffn(h, conv_norm[0][None, :], cwin_b, conv_w[0], cwout_b, ffn_norm[1][None, :],
                    wg1_b, wu1_b, wd1_b, final_norm[None, :], seq)
    return out.reshape(bsz, seq, d)
```

```python
import functools
import math

import jax
import jax.numpy as jnp
import numpy as np
from jax import lax
from jax.experimental import pallas as pl
from jax.experimental.pallas import tpu as pltpu

D_MODEL = 1024
CHUNK = 64
N_HEADS = 8
QK_NOPE_DIM = 128
QK_ROPE_DIM = 64
V_HEAD_DIM = 128
Q_LORA_RANK = 512
KV_LORA_RANK = 256
ROPE_THETA = 10000.0
CONV_WIDTH = 3
RMS_EPS = 1e-6

LANES = 128
SUBLANES = 8
BF16_SUBLANES = 16
MXU_COLS = 256
VMEM_LIMIT_BYTES = 56 * 1024 * 1024

TOKEN_TILE = 512
ATTN_TILE = 512
FF_CHUNK = MXU_COLS
CONV_CHUNK = MXU_COLS

BF16 = jnp.bfloat16
F32 = jnp.float32


def _dot(a, b):
    return jnp.dot(a, b, preferred_element_type=F32)


def _rms(x, g):
    return x * lax.rsqrt(jnp.mean(x * x, axis=-1, keepdims=True) + RMS_EPS) * g


def _const_spec(shape):
    return pl.BlockSpec(shape, lambda *_: (0,) * len(shape), pipeline_mode=pl.Buffered(1))


def _params(semantics):
    return pltpu.CompilerParams(dimension_semantics=semantics, vmem_limit_bytes=VMEM_LIMIT_BYTES)


def _cast_specs(w, layer, steps, step_of):
    _, rows, cols = w.shape
    out_shape = jax.ShapeDtypeStruct((rows, cols), BF16)
    if rows % (steps * BF16_SUBLANES) == 0:
        rb = rows // steps
        return (pl.BlockSpec((None, rb, cols), lambda *g: (layer, step_of(*g), 0)),
                pl.BlockSpec((rb, cols), lambda *g: (step_of(*g), 0)), out_shape, 1)
    per = steps // (cols // LANES)
    return (pl.BlockSpec((None, rows, LANES), lambda *g: (layer, 0, step_of(*g) // per)),
            pl.BlockSpec((rows, LANES), lambda *g: (0, step_of(*g) // per)), out_shape, per)


def _cast_side_work(step, cast_in_refs, cast_out_refs, periods):
    for src, dst, period in zip(cast_in_refs, cast_out_refs, periods):
        @pl.when(step % period == 0)
        def _(src=src, dst=dst):
            dst[...] = src[...].astype(BF16)


def _mla_pre_kernel(x_ref, pos_ref, g_ref, w_in_ref, gcq_ref, gckv_ref, wqn_ref, wqr_ref,
                    wkn_ref, wv_ref, invf_ref, qn_ref, qr_ref, kn_ref, kr_ref, v_ref, *, scale):
    tm = x_ref.shape[0]
    xn = _rms(x_ref[...], g_ref[...]).astype(BF16)
    proj = _dot(xn, w_in_ref[...])
    cq = _rms(proj[:, :Q_LORA_RANK], gcq_ref[...]).astype(BF16)
    ckv = _rms(proj[:, Q_LORA_RANK:Q_LORA_RANK + KV_LORA_RANK], gckv_ref[...]).astype(BF16)
    kr = proj[:, Q_LORA_RANK + KV_LORA_RANK:]

    ang = pos_ref[...] * invf_ref[...]
    cos = jnp.cos(ang)
    sin = jnp.sin(ang)
    lane = lax.broadcasted_iota(jnp.int32, (tm, LANES), 1)
    first_half = (lane % QK_ROPE_DIM) < (QK_ROPE_DIM // 2)
    sin_signed = jnp.where(first_half, -sin, sin)

    def rope(t):
        partner = jnp.where(first_half,
                            pltpu.roll(t, LANES - QK_ROPE_DIM // 2, 1),
                            pltpu.roll(t, QK_ROPE_DIM // 2, 1))
        return t * cos + partner * sin_signed

    kr_rot = rope(kr)
    kr_ref[0] = kr_rot.astype(BF16)
    kr_ref[1] = pltpu.roll(kr_rot, QK_ROPE_DIM, 1).astype(BF16)

    qn = _dot(cq, wqn_ref[...]) * scale
    for h in range(N_HEADS):
        qn_ref[h] = qn[:, h * LANES:(h + 1) * LANES].astype(BF16)

    qr = _dot(cq, wqr_ref[...])
    for p in range(N_HEADS // 2):
        qr_ref[p] = (rope(qr[:, p * LANES:(p + 1) * LANES]) * scale).astype(BF16)

    kn = _dot(ckv, wkn_ref[...])
    vv = _dot(ckv, wv_ref[...])
    for h in range(N_HEADS):
        kn_ref[h] = kn[:, h * LANES:(h + 1) * LANES].astype(BF16)
        v_ref[h] = vv[:, h * LANES:(h + 1) * LANES].astype(BF16)


def _mla_pre(x2, pos, g, w_in, gcq, gckv, wqn, wqr, wkn, wv, invf):
    t = x2.shape[0]
    tm = TOKEN_TILE
    scale = math.log2(math.e) / math.sqrt(QK_NOPE_DIM + QK_ROPE_DIM)

    def stacked(n):
        return (pl.BlockSpec((n, tm, LANES), lambda i: (0, i, 0)), jax.ShapeDtypeStruct((n, t, LANES), BF16))

    specs, shapes = zip(stacked(N_HEADS), stacked(N_HEADS // 2), stacked(N_HEADS), stacked(2), stacked(N_HEADS))
    return pl.pallas_call(
        functools.partial(_mla_pre_kernel, scale=scale),
        grid=(t // tm,),
        in_specs=[
            pl.BlockSpec((tm, D_MODEL), lambda i: (i, 0)),
            pl.BlockSpec((tm, 1), lambda i: (i, 0)),
            _const_spec(g.shape), _const_spec(w_in.shape), _const_spec(gcq.shape),
            _const_spec(gckv.shape), _const_spec(wqn.shape), _const_spec(wqr.shape),
            _const_spec(wkn.shape), _const_spec(wv.shape), _const_spec(invf.shape),
        ],
        out_specs=list(specs),
        out_shape=list(shapes),
        compiler_params=_params(("parallel",)),
        name="mla_pre",
    )(x2, pos, g, w_in, gcq, gckv, wqn, wqr, wkn, wv, invf)


def _attn_kernel(*refs, cast_periods):
    n_cast = len(cast_periods)
    qn_ref, qr_ref, kn_ref, kr_ref, v_ref = refs[:5]
    o_ref = refs[5 + n_cast]
    m_a, m_b, acc_a, acc_b = refs[-4:]
    step_id = pl.program_id(0) * pl.num_programs(1) + pl.program_id(1)
    _cast_side_work(step_id, refs[5:5 + n_cast], refs[6 + n_cast:-4], cast_periods)

    seq = qn_ref.shape[0]
    ta = ATTN_TILE
    groups = ta // LANES
    row_chunk = lax.broadcasted_iota(jnp.int32, (ta, ta), 0) // CHUNK
    col_chunk = lax.broadcasted_iota(jnp.int32, (ta, ta), 1) // CHUNK
    diag_mask = col_chunk <= row_chunk
    ones = jnp.ones((ta, LANES), BF16)

    def tile_slice(i):
        return pl.ds(i * ta, ta)

    def kv_tile(j):
        ks = tile_slice(j)
        k = jnp.concatenate([kn_ref[ks, :], kr_ref[ks, :]], axis=1)
        v_ext = jnp.concatenate([v_ref[ks, :], ones], axis=1)
        return k, v_ext

    def step(qs, m_ref, acc_ref, k, v_ext, masked):
        q = jnp.concatenate([qn_ref[qs, :], qr_ref[qs, :]], axis=1)
        s = lax.dot_general(q, k, (((1,), (1,)), ((), ())), preferred_element_type=F32)
        if masked:
            s = jnp.where(diag_mask, s, -1e30)
        parts = [s[:, g * LANES:(g + 1) * LANES] for g in range(groups)]
        m_prev = m_ref[...]
        m_new = jnp.maximum(m_prev, jnp.max(functools.reduce(jnp.maximum, parts), axis=1, keepdims=True))
        alpha = jnp.exp2(m_prev - m_new)
        p = jnp.concatenate([jnp.exp2(part - m_new).astype(BF16) for part in parts], axis=1)
        acc_ref[...] = acc_ref[...] * jnp.concatenate([alpha, alpha], axis=1) + _dot(p, v_ext)
        m_ref[...] = m_new

    def finish(qs, acc_ref):
        acc = acc_ref[...]
        o_ref[qs, :] = (acc[:, :V_HEAD_DIM] / acc[:, V_HEAD_DIM:]).astype(o_ref.dtype)

    def q_pair(i, _):
        qs_a = tile_slice(2 * i)
        qs_b = tile_slice(2 * i + 1)
        for m_ref, acc_ref in ((m_a, acc_a), (m_b, acc_b)):
            m_ref[...] = jnp.full(m_ref.shape, -1e30, F32)
            acc_ref[...] = jnp.zeros(acc_ref.shape, F32)

        def kv_step(j, _):
            k, v_ext = kv_tile(j)
            step(qs_a, m_a, acc_a, k, v_ext, False)
            step(qs_b, m_b, acc_b, k, v_ext, False)
            return 0

        for j in range(2 * i):
            kv_step(j, 0)
        k, v_ext = kv_tile(2 * i)
        step(qs_a, m_a, acc_a, k, v_ext, True)
        step(qs_b, m_b, acc_b, k, v_ext, False)
        finish(qs_a, acc_a)
        k, v_ext = kv_tile(2 * i + 1)
        step(qs_b, m_b, acc_b, k, v_ext, True)
        finish(qs_b, acc_b)
        return 0

    for i in range(seq // (2 * ta)):
        q_pair(i, 0)


def _attention(qn, qr, kn, kr, v, bsz, seq, casts):
    t = bsz * seq
    ta = ATTN_TILE
    head_spec = pl.BlockSpec((None, seq, LANES), lambda b, h: (h, b, 0))
    cast_specs = [_cast_specs(w, layer, bsz * N_HEADS, lambda b, h: b * N_HEADS + h) for w, layer in casts]
    return pl.pallas_call(
        functools.partial(_attn_kernel, cast_periods=tuple(c[3] for c in cast_specs)),
        grid=(bsz, N_HEADS),
        in_specs=[head_spec,
                  pl.BlockSpec((None, seq, LANES), lambda b, h: (h // 2, b, 0)),
                  head_spec,
                  pl.BlockSpec((None, seq, LANES), lambda b, h: (h % 2, b, 0)),
                  head_spec] + [c[0] for c in cast_specs],
        out_specs=[head_spec] + [c[1] for c in cast_specs],
        out_shape=[jax.ShapeDtypeStruct((N_HEADS, t, LANES), BF16)] + [c[2] for c in cast_specs],
        scratch_shapes=[pltpu.VMEM((ta, LANES), F32), pltpu.VMEM((ta, LANES), F32),
                        pltpu.VMEM((ta, 2 * LANES), F32), pltpu.VMEM((ta, 2 * LANES), F32)],
        compiler_params=_params(("arbitrary", "arbitrary")),
        name="attention",
    )(qn, qr, kn, kr, v, *[w for w, _ in casts])


def _swiglu_residual(h1, g, wg_ref, wu_ref, wd_ref, act_ref):
    d_ff = wg_ref.shape[1]
    hn = _rms(h1, g).astype(BF16)
    for c in range(d_ff // FF_CHUNK):
        cs = slice(c * FF_CHUNK, (c + 1) * FF_CHUNK)
        gate = _dot(hn, wg_ref[:, cs])
        up = _dot(hn, wu_ref[:, cs])
        act_ref[:, cs] = (jax.nn.silu(gate) * up).astype(BF16)
    return h1 + _dot(act_ref[...], wd_ref[...])


def _wo_ffn_kernel(a_ref, h_ref, wo_ref, g_ref, wg_ref, wu_ref, wd_ref, out_ref, act_ref):
    a = jnp.concatenate([a_ref[h] for h in range(N_HEADS)], axis=1)
    h1 = h_ref[...] + _dot(a, wo_ref[...])
    out_ref[...] = _swiglu_residual(h1, g_ref[...], wg_ref, wu_ref, wd_ref, act_ref)


def _wo_ffn(attn, h, wo, g, wg, wu, wd):
    t = h.shape[0]
    tm = TOKEN_TILE
    return pl.pallas_call(
        _wo_ffn_kernel,
        grid=(t // tm,),
        in_specs=[
            pl.BlockSpec((N_HEADS, tm, LANES), lambda i: (0, i, 0)),
            pl.BlockSpec((tm, D_MODEL), lambda i: (i, 0)),
            _const_spec(wo.shape), _const_spec(g.shape), _const_spec(wg.shape),
            _const_spec(wu.shape), _const_spec(wd.shape),
        ],
        out_specs=pl.BlockSpec((tm, D_MODEL), lambda i: (i, 0)),
        out_shape=jax.ShapeDtypeStruct((t, D_MODEL), F32),
        scratch_shapes=[pltpu.VMEM((tm, wg.shape[1]), BF16)],
        compiler_params=_params(("parallel",)),
        name="wo_ffn",
    )(attn, h, wo, g, wg, wu, wd)


def _conv_ffn_kernel(h_ref, gc_ref, win_ref, cw_ref, wout_ref, gf_ref, wg_ref, wu_ref, wd_ref,
                     gfin_ref, out_ref, ubuf_ref, gated_ref, act_ref, *, tiles_per_seq):
    tm = h_ref.shape[0]
    pad = SUBLANES

    @pl.when(pl.program_id(0) % tiles_per_seq == 0)
    def _():
        ubuf_ref[0:pad, :] = jnp.zeros((pad, D_MODEL), F32)

    @pl.when(pl.program_id(0) % tiles_per_seq != 0)
    def _():
        ubuf_ref[0:pad, :] = ubuf_ref[tm:tm + pad, :]

    h = h_ref[...]
    hn = _rms(h, gc_ref[...]).astype(BF16)
    for c in range(D_MODEL // CONV_CHUNK):
        cs = slice(c * CONV_CHUNK, (c + 1) * CONV_CHUNK)
        b_gate = _dot(hn, win_ref[:, c * CONV_CHUNK:(c + 1) * CONV_CHUNK])
        c_gate = _dot(hn, win_ref[:, D_MODEL + c * CONV_CHUNK:D_MODEL + (c + 1) * CONV_CHUNK])
        xp = _dot(hn, win_ref[:, 2 * D_MODEL + c * CONV_CHUNK:2 * D_MODEL + (c + 1) * CONV_CHUNK])
        u = c_gate * xp
        ubuf_ref[pad:pad + tm, cs] = u
        u_conv = (cw_ref[0:1, cs] * ubuf_ref[pad - 2:pad - 2 + tm, cs]
                  + cw_ref[1:2, cs] * ubuf_ref[pad - 1:pad - 1 + tm, cs]
                  + cw_ref[2:3, cs] * u)
        gated_ref[:, cs] = (b_gate * u_conv).astype(BF16)

    h1 = h + _dot(gated_ref[...], wout_ref[...])
    h2 = _swiglu_residual(h1, gf_ref[...], wg_ref, wu_ref, wd_ref, act_ref)
    out_ref[...] = _rms(h2, gfin_ref[...])


def _conv_ffn(h, gc, win, cw, wout, gf, wg, wu, wd, gfin, seq):
    t = h.shape[0]
    tm = TOKEN_TILE
    return pl.pallas_call(
        functools.partial(_conv_ffn_kernel, tiles_per_seq=seq // tm),
        grid=(t // tm,),
        in_specs=[
            pl.BlockSpec((tm, D_MODEL), lambda i: (i, 0)),
            _const_spec(gc.shape), _const_spec(win.shape), _const_spec(cw.shape),
            _const_spec(wout.shape), _const_spec(gf.shape), _const_spec(wg.shape),
            _const_spec(wu.shape), _const_spec(wd.shape), _const_spec(gfin.shape),
        ],
        out_specs=pl.BlockSpec((tm, D_MODEL), lambda i: (i, 0)),
        out_shape=jax.ShapeDtypeStruct((t, D_MODEL), F32),
        scratch_shapes=[
            pltpu.VMEM((tm + SUBLANES, D_MODEL), F32),
            pltpu.VMEM((tm, D_MODEL), BF16),
            pltpu.VMEM((tm, wg.shape[1]), BF16),
        ],
        compiler_params=_params(("arbitrary",)),
        name="conv_ffn",
    )(h, gc, win, cw, wout, gf, wg, wu, wd, gfin)


def _rope_inv_freq_lanes():
    inv = 1.0 / (ROPE_THETA ** (np.arange(0, QK_ROPE_DIM, 2, dtype=np.float32) / np.float32(QK_ROPE_DIM)))
    inv = inv.astype(np.float32)
    return jnp.asarray(np.tile(inv, LANES // (QK_ROPE_DIM // 2))[None, :])


def kernel(x, positions, mla_norm, mla_w_in, mla_g_cq, mla_g_ckv, mla_w_uq, mla_w_ukv, mla_w_o,
           conv_norm, conv_w_in, conv_w, conv_w_out, ffn_norm, ffn_w_gate, ffn_w_up, ffn_w_down,
           final_norm):
    bsz, seq, d = x.shape
    t = bsz * seq
    assert d == D_MODEL and seq % TOKEN_TILE == 0 and seq % (2 * ATTN_TILE) == 0 and ATTN_TILE % CHUNK == 0
    x2 = x.reshape(t, d)
    pos = positions.reshape(t, 1).astype(F32)

    w_in = jnp.pad(mla_w_in[0], ((0, 0), (0, LANES - QK_ROPE_DIM))).astype(BF16)
    w_uq = mla_w_uq[0].reshape(Q_LORA_RANK, N_HEADS, QK_NOPE_DIM + QK_ROPE_DIM)
    wqn = w_uq[:, :, :QK_NOPE_DIM].reshape(Q_LORA_RANK, N_HEADS * QK_NOPE_DIM).astype(BF16)
    wqr = w_uq[:, :, QK_NOPE_DIM:].reshape(Q_LORA_RANK, N_HEADS * QK_ROPE_DIM).astype(BF16)
    w_ukv = mla_w_ukv[0].reshape(KV_LORA_RANK, N_HEADS, QK_NOPE_DIM + V_HEAD_DIM)
    wkn = w_ukv[:, :, :QK_NOPE_DIM].reshape(KV_LORA_RANK, N_HEADS * QK_NOPE_DIM).astype(BF16)
    wv = w_ukv[:, :, QK_NOPE_DIM:].reshape(KV_LORA_RANK, N_HEADS * V_HEAD_DIM).astype(BF16)

    qn, qr, kn, kr, v = _mla_pre(
        x2, pos, mla_norm[0][None, :], w_in, mla_g_cq[0][None, :], mla_g_ckv[0][None, :],
        wqn, wqr, wkn, wv, _rope_inv_freq_lanes())
    later_weights = [(mla_w_o, 0), (ffn_w_gate, 0), (ffn_w_up, 0), (ffn_w_down, 0),
                     (conv_w_in, 0), (conv_w_out, 0), (ffn_w_gate, 1), (ffn_w_up, 1), (ffn_w_down, 1)]
    (attn, wo_b, wg0_b, wu0_b, wd0_b, cwin_b, cwout_b, wg1_b, wu1_b, wd1_b) = _attention(
        qn, qr, kn, kr, v, bsz, seq, later_weights)
    h = _wo_ffn(attn, x2, wo_b, ffn_norm[0][None, :], wg0_b, wu0_b, wd0_b)
    out = _conv_ffn(h, conv_norm[0][None, :], cwin_b, conv_w[0], cwout_b, ffn_norm[1][None, :],
                    wg1_b, wu1_b, wd1_b, final_norm[None, :], seq)
    return out.reshape(bsz, seq, d)
```

```python
import functools
import math

import jax
import jax.numpy as jnp
import numpy as np
from jax import lax
from jax.experimental import pallas as pl
from jax.experimental.pallas import tpu as pltpu

D_MODEL = 1024
CHUNK = 64
N_HEADS = 8
QK_NOPE_DIM = 128
QK_ROPE_DIM = 64
V_HEAD_DIM = 128
Q_LORA_RANK = 512
KV_LORA_RANK = 256
ROPE_THETA = 10000.0
CONV_WIDTH = 3
RMS_EPS = 1e-6

LANES = 128
SUBLANES = 8
BF16_SUBLANES = 16
MXU_COLS = 256
VMEM_LIMIT_BYTES = 56 * 1024 * 1024

TOKEN_TILE = 512
WIDE_TOKEN_TILE = 1024
ATTN_TILE = 512
FF_CHUNK = MXU_COLS
CONV_CHUNK = MXU_COLS

BF16 = jnp.bfloat16
F32 = jnp.float32


def _dot(a, b):
    return jnp.dot(a, b, preferred_element_type=F32)


def _rms(x, g):
    return x * lax.rsqrt(jnp.mean(x * x, axis=-1, keepdims=True) + RMS_EPS) * g


def _const_spec(shape):
    return pl.BlockSpec(shape, lambda *_: (0,) * len(shape), pipeline_mode=pl.Buffered(1))


def _params(semantics):
    return pltpu.CompilerParams(dimension_semantics=semantics, vmem_limit_bytes=VMEM_LIMIT_BYTES)


def _cast_specs(w, layer, steps, step_of):
    _, rows, cols = w.shape
    out_shape = jax.ShapeDtypeStruct((rows, cols), BF16)
    if rows % (steps * BF16_SUBLANES) == 0:
        rb = rows // steps
        return (pl.BlockSpec((None, rb, cols), lambda *g: (layer, step_of(*g), 0)),
                pl.BlockSpec((rb, cols), lambda *g: (step_of(*g), 0)), out_shape, 1)
    per = steps // (cols // LANES)
    return (pl.BlockSpec((None, rows, LANES), lambda *g: (layer, 0, step_of(*g) // per)),
            pl.BlockSpec((rows, LANES), lambda *g: (0, step_of(*g) // per)), out_shape, per)


def _cast_side_work(step, cast_in_refs, cast_out_refs, periods):
    for src, dst, period in zip(cast_in_refs, cast_out_refs, periods):
        @pl.when(step % period == 0)
        def _(src=src, dst=dst):
            dst[...] = src[...].astype(BF16)


def _mla_pre_kernel(x_ref, pos_ref, g_ref, w_in_ref, gcq_ref, gckv_ref, wqn_ref, wqr_ref,
                    wkn_ref, wv_ref, invf_ref, qn_ref, qr_ref, kn_ref, kr_ref, v_ref, *, scale):
    tm = x_ref.shape[0]
    xn = _rms(x_ref[...], g_ref[...]).astype(BF16)
    proj = _dot(xn, w_in_ref[...])
    cq = _rms(proj[:, :Q_LORA_RANK], gcq_ref[...]).astype(BF16)
    ckv = _rms(proj[:, Q_LORA_RANK:Q_LORA_RANK + KV_LORA_RANK], gckv_ref[...]).astype(BF16)
    kr = proj[:, Q_LORA_RANK + KV_LORA_RANK:]

    ang = pos_ref[...] * invf_ref[...]
    cos = jnp.cos(ang)
    sin = jnp.sin(ang)
    lane = lax.broadcasted_iota(jnp.int32, (tm, LANES), 1)
    first_half = (lane % QK_ROPE_DIM) < (QK_ROPE_DIM // 2)
    sin_signed = jnp.where(first_half, -sin, sin)

    def rope(t):
        partner = jnp.where(first_half,
                            pltpu.roll(t, LANES - QK_ROPE_DIM // 2, 1),
                            pltpu.roll(t, QK_ROPE_DIM // 2, 1))
        return t * cos + partner * sin_signed

    kr_rot = rope(kr)
    kr_ref[0] = kr_rot.astype(BF16)
    kr_ref[1] = pltpu.roll(kr_rot, QK_ROPE_DIM, 1).astype(BF16)

    qn = _dot(cq, wqn_ref[...]) * scale
    for h in range(N_HEADS):
        qn_ref[h] = qn[:, h * LANES:(h + 1) * LANES].astype(BF16)

    qr = _dot(cq, wqr_ref[...])
    for p in range(N_HEADS // 2):
        qr_ref[p] = (rope(qr[:, p * LANES:(p + 1) * LANES]) * scale).astype(BF16)

    kn = _dot(ckv, wkn_ref[...])
    vv = _dot(ckv, wv_ref[...])
    for h in range(N_HEADS):
        kn_ref[h] = kn[:, h * LANES:(h + 1) * LANES].astype(BF16)
        v_ref[h] = vv[:, h * LANES:(h + 1) * LANES].astype(BF16)


def _mla_pre(x2, pos, g, w_in, gcq, gckv, wqn, wqr, wkn, wv, invf):
    t = x2.shape[0]
    tm = WIDE_TOKEN_TILE
    scale = math.log2(math.e) / math.sqrt(QK_NOPE_DIM + QK_ROPE_DIM)

    def stacked(n):
        return (pl.BlockSpec((n, tm, LANES), lambda i: (0, i, 0)), jax.ShapeDtypeStruct((n, t, LANES), BF16))

    specs, shapes = zip(stacked(N_HEADS), stacked(N_HEADS // 2), stacked(N_HEADS), stacked(2), stacked(N_HEADS))
    return pl.pallas_call(
        functools.partial(_mla_pre_kernel, scale=scale),
        grid=(t // tm,),
        in_specs=[
            pl.BlockSpec((tm, D_MODEL), lambda i: (i, 0)),
            pl.BlockSpec((tm, 1), lambda i: (i, 0)),
            _const_spec(g.shape), _const_spec(w_in.shape), _const_spec(gcq.shape),
            _const_spec(gckv.shape), _const_spec(wqn.shape), _const_spec(wqr.shape),
            _const_spec(wkn.shape), _const_spec(wv.shape), _const_spec(invf.shape),
        ],
        out_specs=list(specs),
        out_shape=list(shapes),
        compiler_params=_params(("parallel",)),
        name="mla_pre",
    )(x2, pos, g, w_in, gcq, gckv, wqn, wqr, wkn, wv, invf)


def _attn_kernel(*refs, cast_periods):
    n_cast = len(cast_periods)
    qn_ref, qr_ref, kn_ref, kr_ref, v_ref = refs[:5]
    o_ref = refs[5 + n_cast]
    m_a, m_b, acc_a, acc_b = refs[-4:]
    step_id = pl.program_id(0) * pl.num_programs(1) + pl.program_id(1)
    _cast_side_work(step_id, refs[5:5 + n_cast], refs[6 + n_cast:-4], cast_periods)

    seq = qn_ref.shape[0]
    ta = ATTN_TILE
    groups = ta // LANES
    row_chunk = lax.broadcasted_iota(jnp.int32, (ta, ta), 0) // CHUNK
    col_chunk = lax.broadcasted_iota(jnp.int32, (ta, ta), 1) // CHUNK
    diag_mask = col_chunk <= row_chunk
    ones = jnp.ones((ta, LANES), BF16)

    def tile_slice(i):
        return pl.ds(i * ta, ta)

    def kv_tile(j):
        ks = tile_slice(j)
        k = jnp.concatenate([kn_ref[ks, :], kr_ref[ks, :]], axis=1)
        v_ext = jnp.concatenate([v_ref[ks, :], ones], axis=1)
        return k, v_ext

    def step(qs, m_ref, acc_ref, k, v_ext, masked):
        q = jnp.concatenate([qn_ref[qs, :], qr_ref[qs, :]], axis=1)
        s = lax.dot_general(q, k, (((1,), (1,)), ((), ())), preferred_element_type=F32)
        if masked:
            s = jnp.where(diag_mask, s, -1e30)
        parts = [s[:, g * LANES:(g + 1) * LANES] for g in range(groups)]
        m_prev = m_ref[...]
        m_new = jnp.maximum(m_prev, jnp.max(functools.reduce(jnp.maximum, parts), axis=1, keepdims=True))
        alpha = jnp.exp2(m_prev - m_new)
        p = jnp.concatenate([jnp.exp2(part - m_new).astype(BF16) for part in parts], axis=1)
        acc_ref[...] = acc_ref[...] * jnp.concatenate([alpha, alpha], axis=1) + _dot(p, v_ext)
        m_ref[...] = m_new

    def finish(qs, acc_ref):
        acc = acc_ref[...]
        o_ref[qs, :] = (acc[:, :V_HEAD_DIM] / acc[:, V_HEAD_DIM:]).astype(o_ref.dtype)

    def q_pair(i, _):
        qs_a = tile_slice(2 * i)
        qs_b = tile_slice(2 * i + 1)
        for m_ref, acc_ref in ((m_a, acc_a), (m_b, acc_b)):
            m_ref[...] = jnp.full(m_ref.shape, -1e30, F32)
            acc_ref[...] = jnp.zeros(acc_ref.shape, F32)

        def kv_step(j, _):
            k, v_ext = kv_tile(j)
            step(qs_a, m_a, acc_a, k, v_ext, False)
            step(qs_b, m_b, acc_b, k, v_ext, False)
            return 0

        for j in range(2 * i):
            kv_step(j, 0)
        k, v_ext = kv_tile(2 * i)
        step(qs_a, m_a, acc_a, k, v_ext, True)
        step(qs_b, m_b, acc_b, k, v_ext, False)
        finish(qs_a, acc_a)
        k, v_ext = kv_tile(2 * i + 1)
        step(qs_b, m_b, acc_b, k, v_ext, True)
        finish(qs_b, acc_b)
        return 0

    for i in range(seq // (2 * ta)):
        q_pair(i, 0)


def _attention(qn, qr, kn, kr, v, bsz, seq, casts):
    t = bsz * seq
    ta = ATTN_TILE
    head_spec = pl.BlockSpec((None, seq, LANES), lambda b, h: (h, b, 0))
    cast_specs = [_cast_specs(w, layer, bsz * N_HEADS, lambda b, h: b * N_HEADS + h) for w, layer in casts]
    return pl.pallas_call(
        functools.partial(_attn_kernel, cast_periods=tuple(c[3] for c in cast_specs)),
        grid=(bsz, N_HEADS),
        in_specs=[head_spec,
                  pl.BlockSpec((None, seq, LANES), lambda b, h: (h // 2, b, 0)),
                  head_spec,
                  pl.BlockSpec((None, seq, LANES), lambda b, h: (h % 2, b, 0)),
                  head_spec] + [c[0] for c in cast_specs],
        out_specs=[head_spec] + [c[1] for c in cast_specs],
        out_shape=[jax.ShapeDtypeStruct((N_HEADS, t, LANES), BF16)] + [c[2] for c in cast_specs],
        scratch_shapes=[pltpu.VMEM((ta, LANES), F32), pltpu.VMEM((ta, LANES), F32),
                        pltpu.VMEM((ta, 2 * LANES), F32), pltpu.VMEM((ta, 2 * LANES), F32)],
        compiler_params=_params(("arbitrary", "arbitrary")),
        name="attention",
    )(qn, qr, kn, kr, v, *[w for w, _ in casts])


def _swiglu_residual(h1, g, wg_ref, wu_ref, wd_ref, act_ref):
    d_ff = wg_ref.shape[1]
    hn = _rms(h1, g).astype(BF16)
    for c in range(d_ff // FF_CHUNK):
        cs = slice(c * FF_CHUNK, (c + 1) * FF_CHUNK)
        gate = _dot(hn, wg_ref[:, cs])
        up = _dot(hn, wu_ref[:, cs])
        act_ref[:, cs] = (jax.nn.silu(gate) * up).astype(BF16)
    return h1 + _dot(act_ref[...], wd_ref[...])


def _wo_ffn_kernel(a_ref, h_ref, wo_ref, g_ref, wg_ref, wu_ref, wd_ref, out_ref, act_ref):
    a = jnp.concatenate([a_ref[h] for h in range(N_HEADS)], axis=1)
    h1 = h_ref[...] + _dot(a, wo_ref[...])
    out_ref[...] = _swiglu_residual(h1, g_ref[...], wg_ref, wu_ref, wd_ref, act_ref)


def _wo_ffn(attn, h, wo, g, wg, wu, wd):
    t = h.shape[0]
    tm = WIDE_TOKEN_TILE
    return pl.pallas_call(
        _wo_ffn_kernel,
        grid=(t // tm,),
        in_specs=[
            pl.BlockSpec((N_HEADS, tm, LANES), lambda i: (0, i, 0)),
            pl.BlockSpec((tm, D_MODEL), lambda i: (i, 0)),
            _const_spec(wo.shape), _const_spec(g.shape), _const_spec(wg.shape),
            _const_spec(wu.shape), _const_spec(wd.shape),
        ],
        out_specs=pl.BlockSpec((tm, D_MODEL), lambda i: (i, 0)),
        out_shape=jax.ShapeDtypeStruct((t, D_MODEL), F32),
        scratch_shapes=[pltpu.VMEM((tm, wg.shape[1]), BF16)],
        compiler_params=_params(("parallel",)),
        name="wo_ffn",
    )(attn, h, wo, g, wg, wu, wd)


def _conv_ffn_kernel(h_ref, gc_ref, win_ref, cw_ref, wout_ref, gf_ref, wg_ref, wu_ref, wd_ref,
                     gfin_ref, out_ref, ubuf_ref, gated_ref, act_ref, *, tiles_per_seq):
    tm = h_ref.shape[0]
    pad = SUBLANES

    @pl.when(pl.program_id(0) % tiles_per_seq == 0)
    def _():
        ubuf_ref[0:pad, :] = jnp.zeros((pad, D_MODEL), F32)

    @pl.when(pl.program_id(0) % tiles_per_seq != 0)
    def _():
        ubuf_ref[0:pad, :] = ubuf_ref[tm:tm + pad, :]

    h = h_ref[...]
    hn = _rms(h, gc_ref[...]).astype(BF16)
    for c in range(D_MODEL // CONV_CHUNK):
        cs = slice(c * CONV_CHUNK, (c + 1) * CONV_CHUNK)
        b_gate = _dot(hn, win_ref[:, c * CONV_CHUNK:(c + 1) * CONV_CHUNK])
        c_gate = _dot(hn, win_ref[:, D_MODEL + c * CONV_CHUNK:D_MODEL + (c + 1) * CONV_CHUNK])
        xp = _dot(hn, win_ref[:, 2 * D_MODEL + c * CONV_CHUNK:2 * D_MODEL + (c + 1) * CONV_CHUNK])
        u = c_gate * xp
        ubuf_ref[pad:pad + tm, cs] = u
        u_conv = (cw_ref[0:1, cs] * ubuf_ref[pad - 2:pad - 2 + tm, cs]
                  + cw_ref[1:2, cs] * ubuf_ref[pad - 1:pad - 1 + tm, cs]
                  + cw_ref[2:3, cs] * u)
        gated_ref[:, cs] = (b_gate * u_conv).astype(BF16)

    h1 = h + _dot(gated_ref[...], wout_ref[...])
    h2 = _swiglu_residual(h1, gf_ref[...], wg_ref, wu_ref, wd_ref, act_ref)
    out_ref[...] = _rms(h2, gfin_ref[...])


def _conv_ffn(h, gc, win, cw, wout, gf, wg, wu, wd, gfin, seq):
    t = h.shape[0]
    tm = TOKEN_TILE
    return pl.pallas_call(
        functools.partial(_conv_ffn_kernel, tiles_per_seq=seq // tm),
        grid=(t // tm,),
        in_specs=[
            pl.BlockSpec((tm, D_MODEL), lambda i: (i, 0)),
            _const_spec(gc.shape), _const_spec(win.shape), _const_spec(cw.shape),
            _const_spec(wout.shape), _const_spec(gf.shape), _const_spec(wg.shape),
            _const_spec(wu.shape), _const_spec(wd.shape), _const_spec(gfin.shape),
        ],
        out_specs=pl.BlockSpec((tm, D_MODEL), lambda i: (i, 0)),
        out_shape=jax.ShapeDtypeStruct((t, D_MODEL), F32),
        scratch_shapes=[
            pltpu.VMEM((tm + SUBLANES, D_MODEL), F32),
            pltpu.VMEM((tm, D_MODEL), BF16),
            pltpu.VMEM((tm, wg.shape[1]), BF16),
        ],
        compiler_params=_params(("arbitrary",)),
        name="conv_ffn",
    )(h, gc, win, cw, wout, gf, wg, wu, wd, gfin)


def _rope_inv_freq_lanes():
    inv = 1.0 / (ROPE_THETA ** (np.arange(0, QK_ROPE_DIM, 2, dtype=np.float32) / np.float32(QK_ROPE_DIM)))
    inv = inv.astype(np.float32)
    return jnp.asarray(np.tile(inv, LANES // (QK_ROPE_DIM // 2))[None, :])


def kernel(x, positions, mla_norm, mla_w_in, mla_g_cq, mla_g_ckv, mla_w_uq, mla_w_ukv, mla_w_o,
           conv_norm, conv_w_in, conv_w, conv_w_out, ffn_norm, ffn_w_gate, ffn_w_up, ffn_w_down,
           final_norm):
    bsz, seq, d = x.shape
    t = bsz * seq
    assert d == D_MODEL and seq % TOKEN_TILE == 0 and t % WIDE_TOKEN_TILE == 0 and seq % (2 * ATTN_TILE) == 0 and ATTN_TILE % CHUNK == 0
    x2 = x.reshape(t, d)
    pos = positions.reshape(t, 1).astype(F32)

    w_in = jnp.pad(mla_w_in[0], ((0, 0), (0, LANES - QK_ROPE_DIM))).astype(BF16)
    w_uq = mla_w_uq[0].reshape(Q_LORA_RANK, N_HEADS, QK_NOPE_DIM + QK_ROPE_DIM)
    wqn = w_uq[:, :, :QK_NOPE_DIM].reshape(Q_LORA_RANK, N_HEADS * QK_NOPE_DIM).astype(BF16)
    wqr = w_uq[:, :, QK_NOPE_DIM:].reshape(Q_LORA_RANK, N_HEADS * QK_ROPE_DIM).astype(BF16)
    w_ukv = mla_w_ukv[0].reshape(KV_LORA_RANK, N_HEADS, QK_NOPE_DIM + V_HEAD_DIM)
    wkn = w_ukv[:, :, :QK_NOPE_DIM].reshape(KV_LORA_RANK, N_HEADS * QK_NOPE_DIM).astype(BF16)
    wv = w_ukv[:, :, QK_NOPE_DIM:].reshape(KV_LORA_RANK, N_HEADS * V_HEAD_DIM).astype(BF16)

    qn, qr, kn, kr, v = _mla_pre(
        x2, pos, mla_norm[0][None, :], w_in, mla_g_cq[0][None, :], mla_g_ckv[0][None, :],
        wqn, wqr, wkn, wv, _rope_inv_freq_lanes())
    later_weights = [(mla_w_o, 0), (ffn_w_gate, 0), (ffn_w_up, 0), (ffn_w_down, 0),
                     (conv_w_in, 0), (conv_w_out, 0), (ffn_w_gate, 1), (ffn_w_up, 1), (ffn_w_down, 1)]
    (attn, wo_b, wg0_b, wu0_b, wd0_b, cwin_b, cwout_b, wg1_b, wu1_b, wd1_b) = _attention(
        qn, qr, kn, kr, v, bsz, seq, later_weights)
    h = _wo_ffn(attn, x2, wo_b, ffn_norm[0][None, :], wg0_b, wu0_b, wd0_b)
    out = _conv_ffn(h, conv_norm[0][None, :], cwin_b, conv_w[0], cwout_b, ffn_norm[1][None, :],
                    wg1_b, wu1_b, wd1_b, final_norm[None, :], seq)
    return out.reshape(bsz, seq, d)
```

```python
import functools
import math

import jax
import jax.numpy as jnp
import numpy as np
from jax import lax
from jax.experimental import pallas as pl
from jax.experimental.pallas import tpu as pltpu

D_MODEL = 1024
CHUNK = 64
N_HEADS = 8
QK_NOPE_DIM = 128
QK_ROPE_DIM = 64
V_HEAD_DIM = 128
Q_LORA_RANK = 512
KV_LORA_RANK = 256
ROPE_THETA = 10000.0
CONV_WIDTH = 3
RMS_EPS = 1e-6

LANES = 128
SUBLANES = 8
BF16_SUBLANES = 16
MXU_COLS = 256
VMEM_LIMIT_BYTES = 56 * 1024 * 1024

TOKEN_TILE = 1024
ATTN_TILE = 512
FF_CHUNK = MXU_COLS
CONV_CHUNK = MXU_COLS

BF16 = jnp.bfloat16
F32 = jnp.float32


def _dot(a, b):
    return jnp.dot(a, b, preferred_element_type=F32)


def _rms(x, g):
    return x * lax.rsqrt(jnp.mean(x * x, axis=-1, keepdims=True) + RMS_EPS) * g


def _const_spec(shape):
    return pl.BlockSpec(shape, lambda *_: (0,) * len(shape), pipeline_mode=pl.Buffered(1))


def _params(semantics):
    return pltpu.CompilerParams(dimension_semantics=semantics, vmem_limit_bytes=VMEM_LIMIT_BYTES)


def _cast_specs(w, layer, steps, step_of):
    _, rows, cols = w.shape
    out_shape = jax.ShapeDtypeStruct((rows, cols), BF16)
    if rows % (steps * BF16_SUBLANES) == 0:
        rb = rows // steps
        return (pl.BlockSpec((None, rb, cols), lambda *g: (layer, step_of(*g), 0)),
                pl.BlockSpec((rb, cols), lambda *g: (step_of(*g), 0)), out_shape, 1)
    per = steps // (cols // LANES)
    return (pl.BlockSpec((None, rows, LANES), lambda *g: (layer, 0, step_of(*g) // per)),
            pl.BlockSpec((rows, LANES), lambda *g: (0, step_of(*g) // per)), out_shape, per)


def _cast_side_work(step, cast_in_refs, cast_out_refs, periods):
    for src, dst, period in zip(cast_in_refs, cast_out_refs, periods):
        @pl.when(step % period == 0)
        def _(src=src, dst=dst):
            dst[...] = src[...].astype(BF16)


def _mla_pre_kernel(x_ref, pos_ref, g_ref, w_in_ref, gcq_ref, gckv_ref, wqn_ref, wqr_ref,
                    wkn_ref, wv_ref, invf_ref, qn_ref, qr_ref, kn_ref, kr_ref, v_ref, *, scale):
    tm = x_ref.shape[0]
    xn = _rms(x_ref[...], g_ref[...]).astype(BF16)
    proj = _dot(xn, w_in_ref[...])
    cq = _rms(proj[:, :Q_LORA_RANK], gcq_ref[...]).astype(BF16)
    ckv = _rms(proj[:, Q_LORA_RANK:Q_LORA_RANK + KV_LORA_RANK], gckv_ref[...]).astype(BF16)
    kr = proj[:, Q_LORA_RANK + KV_LORA_RANK:]

    ang = pos_ref[...] * invf_ref[...]
    cos = jnp.cos(ang)
    sin = jnp.sin(ang)
    lane = lax.broadcasted_iota(jnp.int32, (tm, LANES), 1)
    first_half = (lane % QK_ROPE_DIM) < (QK_ROPE_DIM // 2)
    sin_signed = jnp.where(first_half, -sin, sin)

    def rope(t):
        partner = jnp.where(first_half,
                            pltpu.roll(t, LANES - QK_ROPE_DIM // 2, 1),
                            pltpu.roll(t, QK_ROPE_DIM // 2, 1))
        return t * cos + partner * sin_signed

    kr_rot = rope(kr)
    kr_ref[0] = kr_rot.astype(BF16)
    kr_ref[1] = pltpu.roll(kr_rot, QK_ROPE_DIM, 1).astype(BF16)

    qn = _dot(cq, wqn_ref[...]) * scale
    for h in range(N_HEADS):
        qn_ref[h] = qn[:, h * LANES:(h + 1) * LANES].astype(BF16)

    qr = _dot(cq, wqr_ref[...])
    for p in range(N_HEADS // 2):
        qr_ref[p] = (rope(qr[:, p * LANES:(p + 1) * LANES]) * scale).astype(BF16)

    kn = _dot(ckv, wkn_ref[...])
    vv = _dot(ckv, wv_ref[...])
    for h in range(N_HEADS):
        kn_ref[h] = kn[:, h * LANES:(h + 1) * LANES].astype(BF16)
        v_ref[h] = vv[:, h * LANES:(h + 1) * LANES].astype(BF16)


def _mla_pre(x2, pos, g, w_in, gcq, gckv, wqn, wqr, wkn, wv, invf):
    t = x2.shape[0]
    tm = TOKEN_TILE
    scale = math.log2(math.e) / math.sqrt(QK_NOPE_DIM + QK_ROPE_DIM)

    def stacked(n):
        return (pl.BlockSpec((n, tm, LANES), lambda i: (0, i, 0)), jax.ShapeDtypeStruct((n, t, LANES), BF16))

    specs, shapes = zip(stacked(N_HEADS), stacked(N_HEADS // 2), stacked(N_HEADS), stacked(2), stacked(N_HEADS))
    return pl.pallas_call(
        functools.partial(_mla_pre_kernel, scale=scale),
        grid=(t // tm,),
        in_specs=[
            pl.BlockSpec((tm, D_MODEL), lambda i: (i, 0)),
            pl.BlockSpec((tm, 1), lambda i: (i, 0)),
            _const_spec(g.shape), _const_spec(w_in.shape), _const_spec(gcq.shape),
            _const_spec(gckv.shape), _const_spec(wqn.shape), _const_spec(wqr.shape),
            _const_spec(wkn.shape), _const_spec(wv.shape), _const_spec(invf.shape),
        ],
        out_specs=list(specs),
        out_shape=list(shapes),
        compiler_params=_params(("parallel",)),
        name="mla_pre",
    )(x2, pos, g, w_in, gcq, gckv, wqn, wqr, wkn, wv, invf)


def _attn_kernel(*refs, cast_periods):
    n_cast = len(cast_periods)
    qn_ref, qr_ref, kn_ref, kr_ref, v_ref = refs[:5]
    o_ref = refs[5 + n_cast]
    m_a, m_b, acc_a, acc_b = refs[-4:]
    step_id = pl.program_id(0) * pl.num_programs(1) + pl.program_id(1)
    _cast_side_work(step_id, refs[5:5 + n_cast], refs[6 + n_cast:-4], cast_periods)

    seq = qn_ref.shape[0]
    ta = ATTN_TILE
    groups = ta // LANES
    row_chunk = lax.broadcasted_iota(jnp.int32, (ta, ta), 0) // CHUNK
    col_chunk = lax.broadcasted_iota(jnp.int32, (ta, ta), 1) // CHUNK
    diag_mask = col_chunk <= row_chunk
    ones = jnp.ones((ta, LANES), BF16)

    def tile_slice(i):
        return pl.ds(i * ta, ta)

    def kv_tile(j):
        ks = tile_slice(j)
        k = jnp.concatenate([kn_ref[ks, :], kr_ref[ks, :]], axis=1)
        v_ext = jnp.concatenate([v_ref[ks, :], ones], axis=1)
        return k, v_ext

    def step(qs, m_ref, acc_ref, k, v_ext, masked):
        q = jnp.concatenate([qn_ref[qs, :], qr_ref[qs, :]], axis=1)
        s = lax.dot_general(q, k, (((1,), (1,)), ((), ())), preferred_element_type=F32)
        if masked:
            s = jnp.where(diag_mask, s, -1e30)
        parts = [s[:, g * LANES:(g + 1) * LANES] for g in range(groups)]
        m_prev = m_ref[...]
        m_new = jnp.maximum(m_prev, jnp.max(functools.reduce(jnp.maximum, parts), axis=1, keepdims=True))
        alpha = jnp.exp2(m_prev - m_new)
        p = jnp.concatenate([jnp.exp2(part - m_new).astype(BF16) for part in parts], axis=1)
        acc_ref[...] = acc_ref[...] * jnp.concatenate([alpha, alpha], axis=1) + _dot(p, v_ext)
        m_ref[...] = m_new

    def finish(qs, acc_ref):
        acc = acc_ref[...]
        o_ref[qs, :] = (acc[:, :V_HEAD_DIM] / acc[:, V_HEAD_DIM:]).astype(o_ref.dtype)

    def q_pair(i, _):
        qs_a = tile_slice(2 * i)
        qs_b = tile_slice(2 * i + 1)
        for m_ref, acc_ref in ((m_a, acc_a), (m_b, acc_b)):
            m_ref[...] = jnp.full(m_ref.shape, -1e30, F32)
            acc_ref[...] = jnp.zeros(acc_ref.shape, F32)

        def kv_step(j, _):
            k, v_ext = kv_tile(j)
            step(qs_a, m_a, acc_a, k, v_ext, False)
            step(qs_b, m_b, acc_b, k, v_ext, False)
            return 0

        for j in range(2 * i):
            kv_step(j, 0)
        k, v_ext = kv_tile(2 * i)
        step(qs_a, m_a, acc_a, k, v_ext, True)
        step(qs_b, m_b, acc_b, k, v_ext, False)
        finish(qs_a, acc_a)
        k, v_ext = kv_tile(2 * i + 1)
        step(qs_b, m_b, acc_b, k, v_ext, True)
        finish(qs_b, acc_b)
        return 0

    for i in range(seq // (2 * ta)):
        q_pair(i, 0)


def _attention(qn, qr, kn, kr, v, bsz, seq, casts):
    t = bsz * seq
    ta = ATTN_TILE
    head_spec = pl.BlockSpec((None, seq, LANES), lambda b, h: (h, b, 0))
    cast_specs = [_cast_specs(w, layer, bsz * N_HEADS, lambda b, h: b * N_HEADS + h) for w, layer in casts]
    return pl.pallas_call(
        functools.partial(_attn_kernel, cast_periods=tuple(c[3] for c in cast_specs)),
        grid=(bsz, N_HEADS),
        in_specs=[head_spec,
                  pl.BlockSpec((None, seq, LANES), lambda b, h: (h // 2, b, 0)),
                  head_spec,
                  pl.BlockSpec((None, seq, LANES), lambda b, h: (h % 2, b, 0)),
                  head_spec] + [c[0] for c in cast_specs],
        out_specs=[head_spec] + [c[1] for c in cast_specs],
        out_shape=[jax.ShapeDtypeStruct((N_HEADS, t, LANES), BF16)] + [c[2] for c in cast_specs],
        scratch_shapes=[pltpu.VMEM((ta, LANES), F32), pltpu.VMEM((ta, LANES), F32),
                        pltpu.VMEM((ta, 2 * LANES), F32), pltpu.VMEM((ta, 2 * LANES), F32)],
        compiler_params=_params(("arbitrary", "arbitrary")),
        name="attention",
    )(qn, qr, kn, kr, v, *[w for w, _ in casts])


def _swiglu(h1, g, wg_ref, wu_ref, wd_ref, act_ref):
    d_ff = wg_ref.shape[1]
    hn = _rms(h1, g).astype(BF16)
    for c in range(d_ff // FF_CHUNK):
        cs = slice(c * FF_CHUNK, (c + 1) * FF_CHUNK)
        gate = _dot(hn, wg_ref[:, cs])
        up = _dot(hn, wu_ref[:, cs])
        act_ref[:, cs] = (jax.nn.silu(gate) * up).astype(BF16)
    return _dot(act_ref[...], wd_ref[...])


def _wo_ffn_kernel(a_ref, h_ref, wo_ref, g_ref, wg_ref, wu_ref, wd_ref, out_ref, act_ref):
    a = jnp.concatenate([a_ref[h] for h in range(N_HEADS)], axis=1)
    h1 = h_ref[...] + _dot(a, wo_ref[...])
    out_ref[...] = h1 + _swiglu(h1, g_ref[...], wg_ref, wu_ref, wd_ref, act_ref)


def _wo_ffn(attn, h, wo, g, wg, wu, wd):
    t = h.shape[0]
    tm = TOKEN_TILE
    return pl.pallas_call(
        _wo_ffn_kernel,
        grid=(t // tm,),
        in_specs=[
            pl.BlockSpec((N_HEADS, tm, LANES), lambda i: (0, i, 0)),
            pl.BlockSpec((tm, D_MODEL), lambda i: (i, 0)),
            _const_spec(wo.shape), _const_spec(g.shape), _const_spec(wg.shape),
            _const_spec(wu.shape), _const_spec(wd.shape),
        ],
        out_specs=pl.BlockSpec((tm, D_MODEL), lambda i: (i, 0)),
        out_shape=jax.ShapeDtypeStruct((t, D_MODEL), F32),
        scratch_shapes=[pltpu.VMEM((tm, wg.shape[1]), BF16)],
        compiler_params=_params(("parallel",)),
        name="wo_ffn",
    )(attn, h, wo, g, wg, wu, wd)


def _conv_ffn_kernel(h_ref, gc_ref, win_ref, cw_ref, wout_ref, gf_ref, wg_ref, wu_ref, wd_ref,
                     gfin_ref, out_ref, ubuf_ref, hist_ref, act_ref, *, tiles_per_seq):
    tm = h_ref.shape[0]
    pad = SUBLANES

    @pl.when(pl.program_id(0) % tiles_per_seq == 0)
    def _():
        hist_ref[...] = jnp.zeros(hist_ref.shape, F32)

    hn = _rms(h_ref[...], gc_ref[...]).astype(BF16)
    for c in range(D_MODEL // CONV_CHUNK):
        cs = slice(c * CONV_CHUNK, (c + 1) * CONV_CHUNK)
        b_gate = _dot(hn, win_ref[:, c * CONV_CHUNK:(c + 1) * CONV_CHUNK])
        c_gate = _dot(hn, win_ref[:, D_MODEL + c * CONV_CHUNK:D_MODEL + (c + 1) * CONV_CHUNK])
        xp = _dot(hn, win_ref[:, 2 * D_MODEL + c * CONV_CHUNK:2 * D_MODEL + (c + 1) * CONV_CHUNK])
        u = c_gate * xp
        ubuf = ubuf_ref.at[c % 2]
        ubuf[0:pad, :] = hist_ref[:, cs]
        ubuf[pad:pad + tm, :] = u
        hist_ref[:, cs] = ubuf[tm:tm + pad, :]
        u_conv = (cw_ref[0:1, cs] * ubuf[pad - 2:pad - 2 + tm, :]
                  + cw_ref[1:2, cs] * ubuf[pad - 1:pad - 1 + tm, :]
                  + cw_ref[2:3, cs] * u)
        act_ref[:, cs] = (b_gate * u_conv).astype(BF16)

    out_ref[...] = h_ref[...] + _dot(act_ref[:, :D_MODEL], wout_ref[...])
    h2 = out_ref[...] + _swiglu(out_ref[...], gf_ref[...], wg_ref, wu_ref, wd_ref, act_ref)
    out_ref[...] = _rms(h2, gfin_ref[...])


def _conv_ffn(h, gc, win, cw, wout, gf, wg, wu, wd, gfin, seq):
    t = h.shape[0]
    tm = TOKEN_TILE
    return pl.pallas_call(
        functools.partial(_conv_ffn_kernel, tiles_per_seq=seq // tm),
        grid=(t // tm,),
        in_specs=[
            pl.BlockSpec((tm, D_MODEL), lambda i: (i, 0)),
            _const_spec(gc.shape), _const_spec(win.shape), _const_spec(cw.shape),
            _const_spec(wout.shape), _const_spec(gf.shape), _const_spec(wg.shape),
            _const_spec(wu.shape), _const_spec(wd.shape), _const_spec(gfin.shape),
        ],
        out_specs=pl.BlockSpec((tm, D_MODEL), lambda i: (i, 0)),
        out_shape=jax.ShapeDtypeStruct((t, D_MODEL), F32),
        scratch_shapes=[
            pltpu.VMEM((2, tm + SUBLANES, CONV_CHUNK), F32),
            pltpu.VMEM((SUBLANES, D_MODEL), F32),
            pltpu.VMEM((tm, wg.shape[1]), BF16),
        ],
        compiler_params=_params(("arbitrary",)),
        name="conv_ffn",
    )(h, gc, win, cw, wout, gf, wg, wu, wd, gfin)


def _rope_inv_freq_lanes():
    inv = 1.0 / (ROPE_THETA ** (np.arange(0, QK_ROPE_DIM, 2, dtype=np.float32) / np.float32(QK_ROPE_DIM)))
    inv = inv.astype(np.float32)
    return jnp.asarray(np.tile(inv, LANES // (QK_ROPE_DIM // 2))[None, :])


def kernel(x, positions, mla_norm, mla_w_in, mla_g_cq, mla_g_ckv, mla_w_uq, mla_w_ukv, mla_w_o,
           conv_norm, conv_w_in, conv_w, conv_w_out, ffn_norm, ffn_w_gate, ffn_w_up, ffn_w_down,
           final_norm):
    bsz, seq, d = x.shape
    t = bsz * seq
    assert d == D_MODEL and seq % TOKEN_TILE == 0 and seq % (2 * ATTN_TILE) == 0 and ATTN_TILE % CHUNK == 0
    x2 = x.reshape(t, d)
    pos = positions.reshape(t, 1).astype(F32)

    w_in = jnp.pad(mla_w_in[0], ((0, 0), (0, LANES - QK_ROPE_DIM))).astype(BF16)
    w_uq = mla_w_uq[0].reshape(Q_LORA_RANK, N_HEADS, QK_NOPE_DIM + QK_ROPE_DIM)
    wqn = w_uq[:, :, :QK_NOPE_DIM].reshape(Q_LORA_RANK, N_HEADS * QK_NOPE_DIM).astype(BF16)
    wqr = w_uq[:, :, QK_NOPE_DIM:].reshape(Q_LORA_RANK, N_HEADS * QK_ROPE_DIM).astype(BF16)
    w_ukv = mla_w_ukv[0].reshape(KV_LORA_RANK, N_HEADS, QK_NOPE_DIM + V_HEAD_DIM)
    wkn = w_ukv[:, :, :QK_NOPE_DIM].reshape(KV_LORA_RANK, N_HEADS * QK_NOPE_DIM).astype(BF16)
    wv = w_ukv[:, :, QK_NOPE_DIM:].reshape(KV_LORA_RANK, N_HEADS * V_HEAD_DIM).astype(BF16)

    qn, qr, kn, kr, v = _mla_pre(
        x2, pos, mla_norm[0][None, :], w_in, mla_g_cq[0][None, :], mla_g_ckv[0][None, :],
        wqn, wqr, wkn, wv, _rope_inv_freq_lanes())
    later_weights = [(mla_w_o, 0), (ffn_w_gate, 0), (ffn_w_up, 0), (ffn_w_down, 0),
                     (conv_w_in, 0), (conv_w_out, 0), (ffn_w_gate, 1), (ffn_w_up, 1), (ffn_w_down, 1)]
    (attn, wo_b, wg0_b, wu0_b, wd0_b, cwin_b, cwout_b, wg1_b, wu1_b, wd1_b) = _attention(
        qn, qr, kn, kr, v, bsz, seq, later_weights)
    h = _wo_ffn(attn, x2, wo_b, ffn_norm[0][None, :], wg0_b, wu0_b, wd0_b)
    out = _conv_ffn(h, conv_norm[0][None, :], cwin_b, conv_w[0], cwout_b, ffn_norm[1][None, :],
                    wg1_b, wu1_b, wd1_b, final_norm[None, :], seq)
    return out.reshape(bsz, seq, d)
```

```python
import functools
import math

import jax
import jax.numpy as jnp
import numpy as np
from jax import lax
from jax.experimental import pallas as pl
from jax.experimental.pallas import tpu as pltpu

D_MODEL = 1024
CHUNK = 64
N_HEADS = 8
QK_NOPE_DIM = 128
QK_ROPE_DIM = 64
V_HEAD_DIM = 128
Q_LORA_RANK = 512
KV_LORA_RANK = 256
ROPE_THETA = 10000.0
CONV_WIDTH = 3
RMS_EPS = 1e-6

LANES = 128
SUBLANES = 8
BF16_SUBLANES = 16
MXU_COLS = 256
VMEM_LIMIT_BYTES = 56 * 1024 * 1024

TOKEN_TILE = 1024
ATTN_TILE = 512
FF_CHUNK = MXU_COLS
CONV_CHUNK = MXU_COLS

BF16 = jnp.bfloat16
F32 = jnp.float32


def _dot(a, b):
    return jnp.dot(a, b, preferred_element_type=F32)


def _rms(x, g):
    return x * lax.rsqrt(jnp.mean(x * x, axis=-1, keepdims=True) + RMS_EPS) * g


def _const_spec(shape):
    return pl.BlockSpec(shape, lambda *_: (0,) * len(shape), pipeline_mode=pl.Buffered(1))


def _params(semantics):
    return pltpu.CompilerParams(dimension_semantics=semantics, vmem_limit_bytes=VMEM_LIMIT_BYTES)


def _cast_specs(w, layer, steps, step_of):
    _, rows, cols = w.shape
    out_shape = jax.ShapeDtypeStruct((rows, cols), BF16)
    if rows % (steps * BF16_SUBLANES) == 0:
        rb = rows // steps
        return (pl.BlockSpec((None, rb, cols), lambda *g: (layer, step_of(*g), 0)),
                pl.BlockSpec((rb, cols), lambda *g: (step_of(*g), 0)), out_shape)
    per = steps // (cols // LANES)
    return (pl.BlockSpec((None, rows, LANES), lambda *g: (layer, 0, step_of(*g) // per)),
            pl.BlockSpec((rows, LANES), lambda *g: (0, step_of(*g) // per)), out_shape)


def _cast_side_work(cast_in_refs, cast_out_refs):
    for src, dst in zip(cast_in_refs, cast_out_refs):
        dst[...] = src[...].astype(BF16)


def _mla_pre_kernel(x_ref, pos_ref, g_ref, w_in_ref, gcq_ref, gckv_ref, wqn_ref, wqr_ref,
                    wkn_ref, wv_ref, invf_ref, qn_ref, qr_ref, kn_ref, kr_ref, v_ref, *, scale):
    tm = x_ref.shape[0]
    xn = _rms(x_ref[...], g_ref[...]).astype(BF16)
    proj = _dot(xn, w_in_ref[...])
    cq = _rms(proj[:, :Q_LORA_RANK], gcq_ref[...]).astype(BF16)
    ckv = _rms(proj[:, Q_LORA_RANK:Q_LORA_RANK + KV_LORA_RANK], gckv_ref[...]).astype(BF16)
    kr = proj[:, Q_LORA_RANK + KV_LORA_RANK:]

    ang = pos_ref[...] * invf_ref[...]
    cos = jnp.cos(ang)
    sin = jnp.sin(ang)
    lane = lax.broadcasted_iota(jnp.int32, (tm, LANES), 1)
    first_half = (lane % QK_ROPE_DIM) < (QK_ROPE_DIM // 2)
    sin_signed = jnp.where(first_half, -sin, sin)

    def rope(t):
        partner = jnp.where(first_half,
                            pltpu.roll(t, LANES - QK_ROPE_DIM // 2, 1),
                            pltpu.roll(t, QK_ROPE_DIM // 2, 1))
        return t * cos + partner * sin_signed

    kr_rot = rope(kr)
    kr_ref[0] = kr_rot.astype(BF16)
    kr_ref[1] = pltpu.roll(kr_rot, QK_ROPE_DIM, 1).astype(BF16)

    qn = _dot(cq, wqn_ref[...]) * scale
    for h in range(N_HEADS):
        qn_ref[h] = qn[:, h * LANES:(h + 1) * LANES].astype(BF16)

    qr = _dot(cq, wqr_ref[...])
    for p in range(N_HEADS // 2):
        qr_ref[p] = (rope(qr[:, p * LANES:(p + 1) * LANES]) * scale).astype(BF16)

    kn = _dot(ckv, wkn_ref[...])
    vv = _dot(ckv, wv_ref[...])
    for h in range(N_HEADS):
        kn_ref[h] = kn[:, h * LANES:(h + 1) * LANES].astype(BF16)
        v_ref[h] = vv[:, h * LANES:(h + 1) * LANES].astype(BF16)


def _mla_pre(x2, pos, g, w_in, gcq, gckv, wqn, wqr, wkn, wv, invf):
    t = x2.shape[0]
    tm = TOKEN_TILE
    scale = math.log2(math.e) / math.sqrt(QK_NOPE_DIM + QK_ROPE_DIM)

    def stacked(n):
        return (pl.BlockSpec((n, tm, LANES), lambda i: (0, i, 0)), jax.ShapeDtypeStruct((n, t, LANES), BF16))

    specs, shapes = zip(stacked(N_HEADS), stacked(N_HEADS // 2), stacked(N_HEADS), stacked(2), stacked(N_HEADS))
    return pl.pallas_call(
        functools.partial(_mla_pre_kernel, scale=scale),
        grid=(t // tm,),
        in_specs=[
            pl.BlockSpec((tm, D_MODEL), lambda i: (i, 0)),
            pl.BlockSpec((tm, 1), lambda i: (i, 0)),
            _const_spec(g.shape), _const_spec(w_in.shape), _const_spec(gcq.shape),
            _const_spec(gckv.shape), _const_spec(wqn.shape), _const_spec(wqr.shape),
            _const_spec(wkn.shape), _const_spec(wv.shape), _const_spec(invf.shape),
        ],
        out_specs=list(specs),
        out_shape=list(shapes),
        compiler_params=_params(("parallel",)),
        name="mla_pre",
    )(x2, pos, g, w_in, gcq, gckv, wqn, wqr, wkn, wv, invf)


def _attn_kernel(*refs, n_cast):
    qn_ref, qr_ref, kn_ref, kr_ref, v_ref = refs[:5]
    o_ref = refs[5 + n_cast]
    m_ref, acc_ref = refs[-2:]
    _cast_side_work(refs[5:5 + n_cast], refs[6 + n_cast:-2])

    seq = qr_ref.shape[0]
    ta = ATTN_TILE
    groups = ta // LANES
    row_chunk = lax.broadcasted_iota(jnp.int32, (ta, ta), 0) // CHUNK
    col_chunk = lax.broadcasted_iota(jnp.int32, (ta, ta), 1) // CHUNK
    diag_mask = col_chunk <= row_chunk
    ones = jnp.ones((ta, LANES), BF16)

    def tile_slice(i):
        return pl.ds(i * ta, ta)

    def kv_tile(e, j):
        ks = tile_slice(j)
        k = jnp.concatenate([kn_ref[e, ks, :], kr_ref[e, ks, :]], axis=1)
        v_ext = jnp.concatenate([v_ref[e, ks, :], ones], axis=1)
        return k, v_ext

    def step(e, chain, qs, kv, masked):
        k, v_ext = kv
        c = 2 * e + chain
        q = jnp.concatenate([qn_ref[e, qs, :], qr_ref[qs, :]], axis=1)
        s = lax.dot_general(q, k, (((1,), (1,)), ((), ())), preferred_element_type=F32)
        if masked:
            s = jnp.where(diag_mask, s, -1e30)
        parts = [s[:, g * LANES:(g + 1) * LANES] for g in range(groups)]
        m_prev = m_ref[c]
        m_new = jnp.maximum(m_prev, jnp.max(functools.reduce(jnp.maximum, parts), axis=1, keepdims=True))
        alpha = jnp.exp2(m_prev - m_new)
        p = jnp.concatenate([jnp.exp2(part - m_new).astype(BF16) for part in parts], axis=1)
        acc_ref[c] = acc_ref[c] * jnp.concatenate([alpha, alpha], axis=1) + _dot(p, v_ext)
        m_ref[c] = m_new

    def finish(e, chain, qs):
        acc = acc_ref[2 * e + chain]
        o_ref[e, qs, :] = (acc[:, :V_HEAD_DIM] / acc[:, V_HEAD_DIM:]).astype(o_ref.dtype)

    for i in range(seq // (2 * ta)):
        qs_a = tile_slice(2 * i)
        qs_b = tile_slice(2 * i + 1)
        m_ref[...] = jnp.full(m_ref.shape, -1e30, F32)
        acc_ref[...] = jnp.zeros(acc_ref.shape, F32)
        for j in range(2 * i):
            for e in range(2):
                kv = kv_tile(e, j)
                step(e, 0, qs_a, kv, False)
                step(e, 1, qs_b, kv, False)
        for e in range(2):
            kv = kv_tile(e, 2 * i)
            step(e, 0, qs_a, kv, True)
            step(e, 1, qs_b, kv, False)
            finish(e, 0, qs_a)
        for e in range(2):
            step(e, 1, qs_b, kv_tile(e, 2 * i + 1), True)
            finish(e, 1, qs_b)


def _attention(qn, qr, kn, kr, v, bsz, seq, casts):
    t = bsz * seq
    ta = ATTN_TILE
    pairs = N_HEADS // 2
    pair_spec = pl.BlockSpec((2, seq, LANES), lambda b, p: (p, b, 0))
    cast_specs = [_cast_specs(w, layer, bsz * pairs, lambda b, p: b * pairs + p) for w, layer in casts]
    return pl.pallas_call(
        functools.partial(_attn_kernel, n_cast=len(casts)),
        grid=(bsz, pairs),
        in_specs=[pair_spec,
                  pl.BlockSpec((None, seq, LANES), lambda b, p: (p, b, 0)),
                  pair_spec,
                  pl.BlockSpec((2, seq, LANES), lambda b, p: (0, b, 0)),
                  pair_spec] + [c[0] for c in cast_specs],
        out_specs=[pair_spec] + [c[1] for c in cast_specs],
        out_shape=[jax.ShapeDtypeStruct((N_HEADS, t, LANES), BF16)] + [c[2] for c in cast_specs],
        scratch_shapes=[pltpu.VMEM((4, ta, LANES), F32), pltpu.VMEM((4, ta, 2 * LANES), F32)],
        compiler_params=_params(("arbitrary", "arbitrary")),
        name="attention",
    )(qn, qr, kn, kr, v, *[w for w, _ in casts])


def _swiglu(h1, g, wg_ref, wu_ref, wd_ref, act_ref):
    d_ff = wg_ref.shape[1]
    hn = _rms(h1, g).astype(BF16)
    for c in range(d_ff // FF_CHUNK):
        cs = slice(c * FF_CHUNK, (c + 1) * FF_CHUNK)
        gate = _dot(hn, wg_ref[:, cs])
        up = _dot(hn, wu_ref[:, cs])
        act_ref[:, cs] = (jax.nn.silu(gate) * up).astype(BF16)
    return _dot(act_ref[...], wd_ref[...])


def _wo_ffn_kernel(a_ref, h_ref, wo_ref, g_ref, wg_ref, wu_ref, wd_ref, out_ref, act_ref):
    a = jnp.concatenate([a_ref[h] for h in range(N_HEADS)], axis=1)
    h1 = h_ref[...] + _dot(a, wo_ref[...])
    out_ref[...] = h1 + _swiglu(h1, g_ref[...], wg_ref, wu_ref, wd_ref, act_ref)


def _wo_ffn(attn, h, wo, g, wg, wu, wd):
    t = h.shape[0]
    tm = TOKEN_TILE
    return pl.pallas_call(
        _wo_ffn_kernel,
        grid=(t // tm,),
        in_specs=[
            pl.BlockSpec((N_HEADS, tm, LANES), lambda i: (0, i, 0)),
            pl.BlockSpec((tm, D_MODEL), lambda i: (i, 0)),
            _const_spec(wo.shape), _const_spec(g.shape), _const_spec(wg.shape),
            _const_spec(wu.shape), _const_spec(wd.shape),
        ],
        out_specs=pl.BlockSpec((tm, D_MODEL), lambda i: (i, 0)),
        out_shape=jax.ShapeDtypeStruct((t, D_MODEL), F32),
        scratch_shapes=[pltpu.VMEM((tm, wg.shape[1]), BF16)],
        compiler_params=_params(("parallel",)),
        name="wo_ffn",
    )(attn, h, wo, g, wg, wu, wd)


def _conv_ffn_kernel(h_ref, gc_ref, win_ref, cw_ref, wout_ref, gf_ref, wg_ref, wu_ref, wd_ref,
                     gfin_ref, out_ref, ubuf_ref, hist_ref, act_ref, *, tiles_per_seq):
    tm = h_ref.shape[0]
    pad = SUBLANES

    @pl.when(pl.program_id(0) % tiles_per_seq == 0)
    def _():
        hist_ref[...] = jnp.zeros(hist_ref.shape, F32)

    hn = _rms(h_ref[...], gc_ref[...]).astype(BF16)
    for c in range(D_MODEL // CONV_CHUNK):
        cs = slice(c * CONV_CHUNK, (c + 1) * CONV_CHUNK)
        b_gate = _dot(hn, win_ref[:, c * CONV_CHUNK:(c + 1) * CONV_CHUNK])
        c_gate = _dot(hn, win_ref[:, D_MODEL + c * CONV_CHUNK:D_MODEL + (c + 1) * CONV_CHUNK])
        xp = _dot(hn, win_ref[:, 2 * D_MODEL + c * CONV_CHUNK:2 * D_MODEL + (c + 1) * CONV_CHUNK])
        u = c_gate * xp
        ubuf = ubuf_ref.at[c % 2]
        ubuf[0:pad, :] = hist_ref[:, cs]
        ubuf[pad:pad + tm, :] = u
        hist_ref[:, cs] = ubuf[tm:tm + pad, :]
        u_conv = (cw_ref[0:1, cs] * ubuf[pad - 2:pad - 2 + tm, :]
                  + cw_ref[1:2, cs] * ubuf[pad - 1:pad - 1 + tm, :]
                  + cw_ref[2:3, cs] * u)
        act_ref[:, cs] = (b_gate * u_conv).astype(BF16)

    out_ref[...] = h_ref[...] + _dot(act_ref[:, :D_MODEL], wout_ref[...])
    h2 = out_ref[...] + _swiglu(out_ref[...], gf_ref[...], wg_ref, wu_ref, wd_ref, act_ref)
    out_ref[...] = _rms(h2, gfin_ref[...])


def _conv_ffn(h, gc, win, cw, wout, gf, wg, wu, wd, gfin, seq):
    t = h.shape[0]
    tm = TOKEN_TILE
    return pl.pallas_call(
        functools.partial(_conv_ffn_kernel, tiles_per_seq=seq // tm),
        grid=(t // tm,),
        in_specs=[
            pl.BlockSpec((tm, D_MODEL), lambda i: (i, 0)),
            _const_spec(gc.shape), _const_spec(win.shape), _const_spec(cw.shape),
            _const_spec(wout.shape), _const_spec(gf.shape), _const_spec(wg.shape),
            _const_spec(wu.shape), _const_spec(wd.shape), _const_spec(gfin.shape),
        ],
        out_specs=pl.BlockSpec((tm, D_MODEL), lambda i: (i, 0)),
        out_shape=jax.ShapeDtypeStruct((t, D_MODEL), F32),
        scratch_shapes=[
            pltpu.VMEM((2, tm + SUBLANES, CONV_CHUNK), F32),
            pltpu.VMEM((SUBLANES, D_MODEL), F32),
            pltpu.VMEM((tm, wg.shape[1]), BF16),
        ],
        compiler_params=_params(("arbitrary",)),
        name="conv_ffn",
    )(h, gc, win, cw, wout, gf, wg, wu, wd, gfin)


def _rope_inv_freq_lanes():
    inv = 1.0 / (ROPE_THETA ** (np.arange(0, QK_ROPE_DIM, 2, dtype=np.float32) / np.float32(QK_ROPE_DIM)))
    inv = inv.astype(np.float32)
    return jnp.asarray(np.tile(inv, LANES // (QK_ROPE_DIM // 2))[None, :])


def kernel(x, positions, mla_norm, mla_w_in, mla_g_cq, mla_g_ckv, mla_w_uq, mla_w_ukv, mla_w_o,
           conv_norm, conv_w_in, conv_w, conv_w_out, ffn_norm, ffn_w_gate, ffn_w_up, ffn_w_down,
           final_norm):
    bsz, seq, d = x.shape
    t = bsz * seq
    assert d == D_MODEL and seq % TOKEN_TILE == 0 and seq % (2 * ATTN_TILE) == 0 and ATTN_TILE % CHUNK == 0
    x2 = x.reshape(t, d)
    pos = positions.reshape(t, 1).astype(F32)

    w_in = jnp.pad(mla_w_in[0], ((0, 0), (0, LANES - QK_ROPE_DIM))).astype(BF16)
    w_uq = mla_w_uq[0].reshape(Q_LORA_RANK, N_HEADS, QK_NOPE_DIM + QK_ROPE_DIM)
    wqn = w_uq[:, :, :QK_NOPE_DIM].reshape(Q_LORA_RANK, N_HEADS * QK_NOPE_DIM).astype(BF16)
    wqr = w_uq[:, :, QK_NOPE_DIM:].reshape(Q_LORA_RANK, N_HEADS * QK_ROPE_DIM).astype(BF16)
    w_ukv = mla_w_ukv[0].reshape(KV_LORA_RANK, N_HEADS, QK_NOPE_DIM + V_HEAD_DIM)
    wkn = w_ukv[:, :, :QK_NOPE_DIM].reshape(KV_LORA_RANK, N_HEADS * QK_NOPE_DIM).astype(BF16)
    wv = w_ukv[:, :, QK_NOPE_DIM:].reshape(KV_LORA_RANK, N_HEADS * V_HEAD_DIM).astype(BF16)

    qn, qr, kn, kr, v = _mla_pre(
        x2, pos, mla_norm[0][None, :], w_in, mla_g_cq[0][None, :], mla_g_ckv[0][None, :],
        wqn, wqr, wkn, wv, _rope_inv_freq_lanes())
    later_weights = [(mla_w_o, 0), (ffn_w_gate, 0), (ffn_w_up, 0), (ffn_w_down, 0),
                     (conv_w_in, 0), (conv_w_out, 0), (ffn_w_gate, 1), (ffn_w_up, 1), (ffn_w_down, 1)]
    (attn, wo_b, wg0_b, wu0_b, wd0_b, cwin_b, cwout_b, wg1_b, wu1_b, wd1_b) = _attention(
        qn, qr, kn, kr, v, bsz, seq, later_weights)
    h = _wo_ffn(attn, x2, wo_b, ffn_norm[0][None, :], wg0_b, wu0_b, wd0_b)
    out = _conv_ffn(h, conv_norm[0][None, :], cwin_b, conv_w[0], cwout_b, ffn_norm[1][None, :],
                    wg1_b, wu1_b, wd1_b, final_norm[None, :], seq)
    return out.reshape(bsz, seq, d)
```

```python
import functools
import math

import jax
import jax.numpy as jnp
import numpy as np
from jax import lax
from jax.experimental import pallas as pl
from jax.experimental.pallas import tpu as pltpu

D_MODEL = 1024
CHUNK = 64
N_HEADS = 8
QK_NOPE_DIM = 128
QK_ROPE_DIM = 64
V_HEAD_DIM = 128
Q_LORA_RANK = 512
KV_LORA_RANK = 256
ROPE_THETA = 10000.0
CONV_WIDTH = 3
RMS_EPS = 1e-6

LANES = 128
SUBLANES = 8
BF16_SUBLANES = 16
MXU_COLS = 256
VMEM_LIMIT_BYTES = 56 * 1024 * 1024

TOKEN_TILE = 1024
ATTN_TILE = 512
FF_CHUNK = MXU_COLS
CONV_CHUNK = MXU_COLS

BF16 = jnp.bfloat16
F32 = jnp.float32


def _dot(a, b):
    return jnp.dot(a, b, preferred_element_type=F32)


def _rms(x, g):
    return x * lax.rsqrt(jnp.mean(x * x, axis=-1, keepdims=True) + RMS_EPS) * g


def _const_spec(shape):
    return pl.BlockSpec(shape, lambda *_: (0,) * len(shape), pipeline_mode=pl.Buffered(1))


def _layer0_spec(w):
    return pl.BlockSpec((None,) + w.shape[1:], lambda *_: (0,) * w.ndim, pipeline_mode=pl.Buffered(1))


def _params(semantics):
    return pltpu.CompilerParams(dimension_semantics=semantics, vmem_limit_bytes=VMEM_LIMIT_BYTES)


def _cast_specs(w, layer, steps, step_of):
    _, rows, cols = w.shape
    out_shape = jax.ShapeDtypeStruct((rows, cols), BF16)
    if rows % (steps * BF16_SUBLANES) == 0:
        rb = rows // steps
        return (pl.BlockSpec((None, rb, cols), lambda *g: (layer, step_of(*g), 0)),
                pl.BlockSpec((rb, cols), lambda *g: (step_of(*g), 0)), out_shape)
    per = steps // (cols // LANES)
    return (pl.BlockSpec((None, rows, LANES), lambda *g: (layer, 0, step_of(*g) // per)),
            pl.BlockSpec((rows, LANES), lambda *g: (0, step_of(*g) // per)), out_shape)


def _cast_side_work(cast_in_refs, cast_out_refs):
    for src, dst in zip(cast_in_refs, cast_out_refs):
        dst[...] = src[...].astype(BF16)


def _mla_pre_kernel(x_ref, pos_ref, g_ref, w_in_ref, gcq_ref, gckv_ref, w_uq_ref, w_ukv_ref, invf_ref,
                    qn_ref, qr_ref, kn_ref, kr_ref, v_ref, w_in_b, w_uq_b, w_ukv_b, *, scale):
    tm = x_ref.shape[0]
    kr_col = Q_LORA_RANK + KV_LORA_RANK

    @pl.when(pl.program_id(0) == 0)
    def _():
        w_in_b[:, :kr_col] = w_in_ref[:, :kr_col].astype(BF16)
        w_in_b[:, kr_col:] = jnp.zeros((D_MODEL, LANES), BF16)
        w_in_b[:, kr_col:kr_col + QK_ROPE_DIM] = w_in_ref[:, kr_col:].astype(BF16)
        pair = 2 * (QK_NOPE_DIM + QK_ROPE_DIM)
        dst = lax.broadcasted_iota(jnp.int32, (pair, pair), 1)
        src = jnp.where(dst < QK_NOPE_DIM, dst,
                        jnp.where(dst < 2 * QK_NOPE_DIM, dst + QK_ROPE_DIM,
                                  jnp.where(dst < 2 * QK_NOPE_DIM + QK_ROPE_DIM, dst - QK_NOPE_DIM, dst)))
        select = (lax.broadcasted_iota(jnp.int32, (pair, pair), 0) == src).astype(BF16)
        n_nope = N_HEADS * QK_NOPE_DIM
        for p in range(N_HEADS // 2):
            wp = _dot(w_uq_ref[:, p * pair:(p + 1) * pair].astype(BF16), select).astype(BF16)
            w_uq_b[:, 2 * p * QK_NOPE_DIM:2 * (p + 1) * QK_NOPE_DIM] = wp[:, :2 * QK_NOPE_DIM]
            w_uq_b[:, n_nope + p * LANES:n_nope + (p + 1) * LANES] = wp[:, 2 * QK_NOPE_DIM:]
        w_ukv_b[...] = w_ukv_ref[...].astype(BF16)

    xn = _rms(x_ref[...], g_ref[...]).astype(BF16)
    proj = _dot(xn, w_in_b[...])
    cq = _rms(proj[:, :Q_LORA_RANK], gcq_ref[...]).astype(BF16)
    ckv = _rms(proj[:, Q_LORA_RANK:kr_col], gckv_ref[...]).astype(BF16)
    kr = proj[:, kr_col:]

    blk = tm // 4
    n_freq = QK_ROPE_DIM // 2
    lane_c = lax.broadcasted_iota(jnp.int32, (blk, LANES), 1)
    grp_c = lane_c // n_freq
    pos = pos_ref[...].astype(F32)
    pos_c = pos[3 * blk:, :]
    for a in (2, 1, 0):
        pos_c = jnp.where(grp_c == a, pos[a * blk:(a + 1) * blk, :], pos_c)
    ang_c = pos_c * invf_ref[...]
    cos_c = jnp.cos(ang_c)
    sin_c = jnp.sin(ang_c)
    cos_r = [cos_c] + [pltpu.roll(cos_c, n_freq * k, 1) for k in (1, 2, 3)]
    sin_r = [sin_c] + [pltpu.roll(sin_c, n_freq * k, 1) for k in (1, 2, 3)]
    nsin_r = [-t for t in sin_r]

    def expand(a, even_tabs, odd_tabs):
        out = odd_tabs[(3 - a) % 4]
        for g in (2, 1, 0):
            tabs = even_tabs if g % 2 == 0 else odd_tabs
            out = jnp.where(grp_c == g, tabs[(g - a) % 4], out)
        return out

    cos = jnp.concatenate([expand(a, cos_r, cos_r) for a in range(4)], axis=0)
    sin_signed = jnp.concatenate([expand(a, nsin_r, sin_r) for a in range(4)], axis=0)
    lane = lax.broadcasted_iota(jnp.int32, (tm, LANES), 1)
    first_half = (lane % QK_ROPE_DIM) < n_freq

    def rope(t):
        partner = jnp.where(first_half,
                            pltpu.roll(t, LANES - n_freq, 1),
                            pltpu.roll(t, n_freq, 1))
        return t * cos + partner * sin_signed

    kr_rot = rope(kr)
    kr_ref[0] = kr_rot.astype(BF16)
    kr_ref[1] = pltpu.roll(kr_rot, QK_ROPE_DIM, 1).astype(BF16)

    q = _dot(cq, w_uq_b[...])
    for h in range(N_HEADS):
        qn_ref[h] = (q[:, h * LANES:(h + 1) * LANES] * scale).astype(BF16)
    for p in range(N_HEADS // 2):
        pair_cols = slice((N_HEADS + p) * LANES, (N_HEADS + p + 1) * LANES)
        qr_ref[p] = (rope(q[:, pair_cols]) * scale).astype(BF16)

    kv = _dot(ckv, w_ukv_b[...])
    for h in range(N_HEADS):
        kn_ref[h] = kv[:, 2 * h * LANES:(2 * h + 1) * LANES].astype(BF16)
        v_ref[h] = kv[:, (2 * h + 1) * LANES:(2 * h + 2) * LANES].astype(BF16)


def _mla_pre(x2, pos, g, w_in, gcq, gckv, w_uq, w_ukv, invf):
    t = x2.shape[0]
    tm = TOKEN_TILE
    scale = math.log2(math.e) / math.sqrt(QK_NOPE_DIM + QK_ROPE_DIM)

    def stacked(n):
        return (pl.BlockSpec((n, tm, LANES), lambda i: (0, i, 0)), jax.ShapeDtypeStruct((n, t, LANES), BF16))

    specs, shapes = zip(stacked(N_HEADS), stacked(N_HEADS // 2), stacked(N_HEADS), stacked(2), stacked(N_HEADS))
    return pl.pallas_call(
        functools.partial(_mla_pre_kernel, scale=scale),
        grid=(t // tm,),
        in_specs=[
            pl.BlockSpec((tm, D_MODEL), lambda i: (i, 0)),
            pl.BlockSpec((tm, 1), lambda i: (i, 0)),
            _const_spec(g.shape), _layer0_spec(w_in), _const_spec(gcq.shape), _const_spec(gckv.shape),
            _layer0_spec(w_uq), _layer0_spec(w_ukv), _const_spec(invf.shape),
        ],
        out_specs=list(specs),
        out_shape=list(shapes),
        scratch_shapes=[
            pltpu.VMEM((D_MODEL, Q_LORA_RANK + KV_LORA_RANK + LANES), BF16),
            pltpu.VMEM(w_uq.shape[1:], BF16),
            pltpu.VMEM(w_ukv.shape[1:], BF16),
        ],
        compiler_params=_params(("arbitrary",)),
        name="mla_pre",
    )(x2, pos, g, w_in, gcq, gckv, w_uq, w_ukv, invf)


def _attn_kernel(*refs, n_cast):
    qn_ref, qr_ref, kn_ref, kr_ref, v_ref = refs[:5]
    o_ref = refs[5 + n_cast]
    m_a, m_b, acc_a, acc_b = refs[-4:]
    _cast_side_work(refs[5:5 + n_cast], refs[6 + n_cast:-4])

    seq = qn_ref.shape[0]
    ta = ATTN_TILE
    groups = ta // LANES
    row_chunk = lax.broadcasted_iota(jnp.int32, (ta, ta), 0) // CHUNK
    col_chunk = lax.broadcasted_iota(jnp.int32, (ta, ta), 1) // CHUNK
    diag_mask = col_chunk <= row_chunk
    ones = jnp.ones((ta, LANES), BF16)

    def tile_slice(i):
        return pl.ds(i * ta, ta)

    def kv_tile(j):
        ks = tile_slice(j)
        k = jnp.concatenate([kn_ref[ks, :], kr_ref[ks, :]], axis=1)
        v_ext = jnp.concatenate([v_ref[ks, :], ones], axis=1)
        return k, v_ext

    def step(qs, m_ref, acc_ref, k, v_ext, masked):
        q = jnp.concatenate([qn_ref[qs, :], qr_ref[qs, :]], axis=1)
        s = lax.dot_general(q, k, (((1,), (1,)), ((), ())), preferred_element_type=F32)
        if masked:
            s = jnp.where(diag_mask, s, -1e30)
        parts = [s[:, g * LANES:(g + 1) * LANES] for g in range(groups)]
        m_prev = m_ref[...]
        m_new = jnp.maximum(m_prev, jnp.max(functools.reduce(jnp.maximum, parts), axis=1, keepdims=True))
        alpha = jnp.exp2(m_prev - m_new)
        p = jnp.concatenate([jnp.exp2(part - m_new).astype(BF16) for part in parts], axis=1)
        acc_ref[...] = acc_ref[...] * jnp.concatenate([alpha, alpha], axis=1) + _dot(p, v_ext)
        m_ref[...] = m_new

    def finish(qs, acc_ref):
        acc = acc_ref[...]
        o_ref[qs, :] = (acc[:, :V_HEAD_DIM] / acc[:, V_HEAD_DIM:]).astype(o_ref.dtype)

    for i in range(seq // (2 * ta)):
        qs_a = tile_slice(2 * i)
        qs_b = tile_slice(2 * i + 1)
        for m_ref, acc_ref in ((m_a, acc_a), (m_b, acc_b)):
            m_ref[...] = jnp.full(m_ref.shape, -1e30, F32)
            acc_ref[...] = jnp.zeros(acc_ref.shape, F32)
        for j in range(2 * i):
            k, v_ext = kv_tile(j)
            step(qs_a, m_a, acc_a, k, v_ext, False)
            step(qs_b, m_b, acc_b, k, v_ext, False)
        k, v_ext = kv_tile(2 * i)
        step(qs_a, m_a, acc_a, k, v_ext, True)
        step(qs_b, m_b, acc_b, k, v_ext, False)
        finish(qs_a, acc_a)
        k, v_ext = kv_tile(2 * i + 1)
        step(qs_b, m_b, acc_b, k, v_ext, True)
        finish(qs_b, acc_b)


def _attention(qn, qr, kn, kr, v, bsz, seq, casts):
    t = bsz * seq
    ta = ATTN_TILE
    head_spec = pl.BlockSpec((None, seq, LANES), lambda b, h: (h, b, 0))
    cast_specs = [_cast_specs(w, layer, bsz * N_HEADS, lambda b, h: b * N_HEADS + h) for w, layer in casts]
    return pl.pallas_call(
        functools.partial(_attn_kernel, n_cast=len(casts)),
        grid=(bsz, N_HEADS),
        in_specs=[head_spec,
                  pl.BlockSpec((None, seq, LANES), lambda b, h: (h // 2, b, 0)),
                  head_spec,
                  pl.BlockSpec((None, seq, LANES), lambda b, h: (h % 2, b, 0)),
                  head_spec] + [c[0] for c in cast_specs],
        out_specs=[head_spec] + [c[1] for c in cast_specs],
        out_shape=[jax.ShapeDtypeStruct((N_HEADS, t, LANES), BF16)] + [c[2] for c in cast_specs],
        scratch_shapes=[pltpu.VMEM((ta, LANES), F32), pltpu.VMEM((ta, LANES), F32),
                        pltpu.VMEM((ta, 2 * LANES), F32), pltpu.VMEM((ta, 2 * LANES), F32)],
        compiler_params=_params(("arbitrary", "arbitrary")),
        name="attention",
    )(qn, qr, kn, kr, v, *[w for w, _ in casts])


def _swiglu(h1, g, wg_ref, wu_ref, wd_ref, act_ref):
    d_ff = wg_ref.shape[1]
    hn = _rms(h1, g).astype(BF16)
    for c in range(d_ff // FF_CHUNK):
        cs = slice(c * FF_CHUNK, (c + 1) * FF_CHUNK)
        gate = _dot(hn, wg_ref[:, cs])
        up = _dot(hn, wu_ref[:, cs])
        act_ref[:, cs] = (jax.nn.silu(gate) * up).astype(BF16)
    return _dot(act_ref[...], wd_ref[...])


def _wo_ffn_kernel(a_ref, h_ref, wo_ref, g_ref, wg_ref, wu_ref, wd_ref, out_ref, act_ref):
    a = jnp.concatenate([a_ref[h] for h in range(N_HEADS)], axis=1)
    h1 = h_ref[...] + _dot(a, wo_ref[...])
    out_ref[...] = h1 + _swiglu(h1, g_ref[0:1, :], wg_ref, wu_ref, wd_ref, act_ref)


def _wo_ffn(attn, h, wo, g, wg, wu, wd):
    t = h.shape[0]
    tm = TOKEN_TILE
    return pl.pallas_call(
        _wo_ffn_kernel,
        grid=(t // tm,),
        in_specs=[
            pl.BlockSpec((N_HEADS, tm, LANES), lambda i: (0, i, 0)),
            pl.BlockSpec((tm, D_MODEL), lambda i: (i, 0)),
            _const_spec(wo.shape), _const_spec(g.shape), _const_spec(wg.shape),
            _const_spec(wu.shape), _const_spec(wd.shape),
        ],
        out_specs=pl.BlockSpec((tm, D_MODEL), lambda i: (i, 0)),
        out_shape=jax.ShapeDtypeStruct((t, D_MODEL), F32),
        scratch_shapes=[pltpu.VMEM((tm, wg.shape[1]), BF16)],
        compiler_params=_params(("parallel",)),
        name="wo_ffn",
    )(attn, h, wo, g, wg, wu, wd)


def _conv_ffn_kernel(h_ref, gc_ref, win_ref, cw_ref, wout_ref, gf_ref, wg_ref, wu_ref, wd_ref,
                     gfin_ref, out_ref, ubuf_ref, hist_ref, act_ref, *, tiles_per_seq):
    tm = h_ref.shape[0]
    pad = SUBLANES

    @pl.when(pl.program_id(0) % tiles_per_seq == 0)
    def _():
        hist_ref[...] = jnp.zeros(hist_ref.shape, F32)

    hn = _rms(h_ref[...], gc_ref[...]).astype(BF16)
    for c in range(D_MODEL // CONV_CHUNK):
        cs = slice(c * CONV_CHUNK, (c + 1) * CONV_CHUNK)
        b_gate = _dot(hn, win_ref[:, c * CONV_CHUNK:(c + 1) * CONV_CHUNK])
        c_gate = _dot(hn, win_ref[:, D_MODEL + c * CONV_CHUNK:D_MODEL + (c + 1) * CONV_CHUNK])
        xp = _dot(hn, win_ref[:, 2 * D_MODEL + c * CONV_CHUNK:2 * D_MODEL + (c + 1) * CONV_CHUNK])
        u = c_gate * xp
        ubuf = ubuf_ref.at[c % 2]
        ubuf[0:pad, :] = hist_ref[:, cs]
        ubuf[pad:pad + tm, :] = u
        hist_ref[:, cs] = ubuf[tm:tm + pad, :]
        u_conv = (cw_ref[0:1, cs] * ubuf[pad - 2:pad - 2 + tm, :]
                  + cw_ref[1:2, cs] * ubuf[pad - 1:pad - 1 + tm, :]
                  + cw_ref[2:3, cs] * u)
        act_ref[:, cs] = (b_gate * u_conv).astype(BF16)

    out_ref[...] = h_ref[...] + _dot(act_ref[:, :D_MODEL], wout_ref[...])
    h2 = out_ref[...] + _swiglu(out_ref[...], gf_ref[1:2, :], wg_ref, wu_ref, wd_ref, act_ref)
    out_ref[...] = _rms(h2, gfin_ref[...])


def _conv_ffn(h, gc, win, cw, wout, gf, wg, wu, wd, gfin, seq):
    t = h.shape[0]
    tm = TOKEN_TILE
    return pl.pallas_call(
        functools.partial(_conv_ffn_kernel, tiles_per_seq=seq // tm),
        grid=(t // tm,),
        in_specs=[
            pl.BlockSpec((tm, D_MODEL), lambda i: (i, 0)),
            _const_spec(gc.shape), _const_spec(win.shape), _layer0_spec(cw),
            _const_spec(wout.shape), _const_spec(gf.shape), _const_spec(wg.shape),
            _const_spec(wu.shape), _const_spec(wd.shape), _const_spec(gfin.shape),
        ],
        out_specs=pl.BlockSpec((tm, D_MODEL), lambda i: (i, 0)),
        out_shape=jax.ShapeDtypeStruct((t, D_MODEL), F32),
        scratch_shapes=[
            pltpu.VMEM((2, tm + SUBLANES, CONV_CHUNK), F32),
            pltpu.VMEM((SUBLANES, D_MODEL), F32),
            pltpu.VMEM((tm, wg.shape[1]), BF16),
        ],
        compiler_params=_params(("arbitrary",)),
        name="conv_ffn",
    )(h, gc, win, cw, wout, gf, wg, wu, wd, gfin)


def _rope_inv_freq_lanes():
    inv = 1.0 / (ROPE_THETA ** (np.arange(0, QK_ROPE_DIM, 2, dtype=np.float32) / np.float32(QK_ROPE_DIM)))
    inv = inv.astype(np.float32)
    return jnp.asarray(np.tile(inv, LANES // (QK_ROPE_DIM // 2))[None, :])


def kernel(x, positions, mla_norm, mla_w_in, mla_g_cq, mla_g_ckv, mla_w_uq, mla_w_ukv, mla_w_o,
           conv_norm, conv_w_in, conv_w, conv_w_out, ffn_norm, ffn_w_gate, ffn_w_up, ffn_w_down,
           final_norm):
    bsz, seq, d = x.shape
    t = bsz * seq
    assert d == D_MODEL and seq % TOKEN_TILE == 0 and seq % (2 * ATTN_TILE) == 0 and ATTN_TILE % CHUNK == 0
    x2 = x.reshape(t, d)
    qn, qr, kn, kr, v = _mla_pre(
        x2, positions.reshape(t, 1), mla_norm, mla_w_in, mla_g_cq, mla_g_ckv, mla_w_uq, mla_w_ukv,
        _rope_inv_freq_lanes())
    later_weights = [(mla_w_o, 0), (ffn_w_gate, 0), (ffn_w_up, 0), (ffn_w_down, 0),
                     (conv_w_in, 0), (conv_w_out, 0), (ffn_w_gate, 1), (ffn_w_up, 1), (ffn_w_down, 1)]
    (attn, wo_b, wg0_b, wu0_b, wd0_b, cwin_b, cwout_b, wg1_b, wu1_b, wd1_b) = _attention(
        qn, qr, kn, kr, v, bsz, seq, later_weights)
    h = _wo_ffn(attn, x2, wo_b, ffn_norm, wg0_b, wu0_b, wd0_b)
    out = _conv_ffn(h, conv_norm, cwin_b, conv_w, cwout_b, ffn_norm,
                    wg1_b, wu1_b, wd1_b, final_norm[None, :], seq)
    return out.reshape(bsz, seq, d)
```

```python
import functools
import math

import jax
import jax.numpy as jnp
import numpy as np
from jax import lax
from jax.experimental import pallas as pl
from jax.experimental.pallas import tpu as pltpu

D_MODEL = 1024
CHUNK = 64
N_HEADS = 8
QK_NOPE_DIM = 128
QK_ROPE_DIM = 64
V_HEAD_DIM = 128
Q_LORA_RANK = 512
KV_LORA_RANK = 256
ROPE_THETA = 10000.0
CONV_WIDTH = 3
RMS_EPS = 1e-6

LANES = 128
SUBLANES = 8
BF16_SUBLANES = 16
MXU_COLS = 256
VMEM_LIMIT_BYTES = 56 * 1024 * 1024

TOKEN_TILE = 1024
ATTN_TILE = 256
FF_CHUNK = MXU_COLS
CONV_CHUNK = MXU_COLS

BF16 = jnp.bfloat16
F32 = jnp.float32


def _dot(a, b):
    return jnp.dot(a, b, preferred_element_type=F32)


def _rms(x, g):
    return x * lax.rsqrt(jnp.mean(x * x, axis=-1, keepdims=True) + RMS_EPS) * g


def _const_spec(shape):
    return pl.BlockSpec(shape, lambda *_: (0,) * len(shape), pipeline_mode=pl.Buffered(1))


def _layer0_spec(w):
    return pl.BlockSpec((None,) + w.shape[1:], lambda *_: (0,) * w.ndim, pipeline_mode=pl.Buffered(1))


def _params(semantics):
    return pltpu.CompilerParams(dimension_semantics=semantics, vmem_limit_bytes=VMEM_LIMIT_BYTES)


def _cast_specs(w, layer, steps, step_of):
    _, rows, cols = w.shape
    out_shape = jax.ShapeDtypeStruct((rows, cols), BF16)
    if rows % (steps * BF16_SUBLANES) == 0:
        rb = rows // steps
        return (pl.BlockSpec((None, rb, cols), lambda *g: (layer, step_of(*g), 0)),
                pl.BlockSpec((rb, cols), lambda *g: (step_of(*g), 0)), out_shape)
    per = steps // (cols // LANES)
    return (pl.BlockSpec((None, rows, LANES), lambda *g: (layer, 0, step_of(*g) // per)),
            pl.BlockSpec((rows, LANES), lambda *g: (0, step_of(*g) // per)), out_shape)


def _cast_side_work(cast_in_refs, cast_out_refs):
    for src, dst in zip(cast_in_refs, cast_out_refs):
        dst[...] = src[...].astype(BF16)


def _mla_pre_kernel(x_ref, pos_ref, g_ref, w_in_ref, gcq_ref, gckv_ref, w_uq_ref, w_ukv_ref, invf_ref,
                    qn_ref, qr_ref, kn_ref, kr_ref, v_ref, w_in_b, w_uq_b, w_ukv_b, *, scale):
    tm = x_ref.shape[0]
    kr_col = Q_LORA_RANK + KV_LORA_RANK

    @pl.when(pl.program_id(0) == 0)
    def _():
        w_in_b[:, :kr_col] = w_in_ref[:, :kr_col].astype(BF16)
        w_in_b[:, kr_col:] = jnp.zeros((D_MODEL, LANES), BF16)
        w_in_b[:, kr_col:kr_col + QK_ROPE_DIM] = w_in_ref[:, kr_col:].astype(BF16)
        pair = 2 * (QK_NOPE_DIM + QK_ROPE_DIM)
        dst = lax.broadcasted_iota(jnp.int32, (pair, pair), 1)
        src = jnp.where(dst < QK_NOPE_DIM, dst,
                        jnp.where(dst < 2 * QK_NOPE_DIM, dst + QK_ROPE_DIM,
                                  jnp.where(dst < 2 * QK_NOPE_DIM + QK_ROPE_DIM, dst - QK_NOPE_DIM, dst)))
        select = (lax.broadcasted_iota(jnp.int32, (pair, pair), 0) == src).astype(BF16)
        n_nope = N_HEADS * QK_NOPE_DIM
        for p in range(N_HEADS // 2):
            wp = _dot(w_uq_ref[:, p * pair:(p + 1) * pair].astype(BF16), select).astype(BF16)
            w_uq_b[:, 2 * p * QK_NOPE_DIM:2 * (p + 1) * QK_NOPE_DIM] = wp[:, :2 * QK_NOPE_DIM]
            w_uq_b[:, n_nope + p * LANES:n_nope + (p + 1) * LANES] = wp[:, 2 * QK_NOPE_DIM:]
        w_ukv_b[...] = w_ukv_ref[...].astype(BF16)

    xn = _rms(x_ref[...], g_ref[...]).astype(BF16)
    proj = _dot(xn, w_in_b[...])
    cq = _rms(proj[:, :Q_LORA_RANK], gcq_ref[...]).astype(BF16)
    ckv = _rms(proj[:, Q_LORA_RANK:kr_col], gckv_ref[...]).astype(BF16)
    kr = proj[:, kr_col:]

    blk = tm // 4
    n_freq = QK_ROPE_DIM // 2
    lane_c = lax.broadcasted_iota(jnp.int32, (blk, LANES), 1)
    grp_c = lane_c // n_freq
    pos = pos_ref[...].astype(F32)
    pos_c = pos[3 * blk:, :]
    for a in (2, 1, 0):
        pos_c = jnp.where(grp_c == a, pos[a * blk:(a + 1) * blk, :], pos_c)
    ang_c = pos_c * invf_ref[...]
    cos_c = jnp.cos(ang_c)
    sin_c = jnp.sin(ang_c)
    cos_r = [cos_c] + [pltpu.roll(cos_c, n_freq * k, 1) for k in (1, 2, 3)]
    sin_r = [sin_c] + [pltpu.roll(sin_c, n_freq * k, 1) for k in (1, 2, 3)]
    nsin_r = [-t for t in sin_r]

    def expand(a, even_tabs, odd_tabs):
        out = odd_tabs[(3 - a) % 4]
        for g in (2, 1, 0):
            tabs = even_tabs if g % 2 == 0 else odd_tabs
            out = jnp.where(grp_c == g, tabs[(g - a) % 4], out)
        return out

    cos = jnp.concatenate([expand(a, cos_r, cos_r) for a in range(4)], axis=0)
    sin_signed = jnp.concatenate([expand(a, nsin_r, sin_r) for a in range(4)], axis=0)
    lane = lax.broadcasted_iota(jnp.int32, (tm, LANES), 1)
    first_half = (lane % QK_ROPE_DIM) < n_freq

    def rope(t):
        partner = jnp.where(first_half,
                            pltpu.roll(t, LANES - n_freq, 1),
                            pltpu.roll(t, n_freq, 1))
        return t * cos + partner * sin_signed

    kr_rot = rope(kr)
    kr_ref[0] = kr_rot.astype(BF16)
    kr_ref[1] = pltpu.roll(kr_rot, QK_ROPE_DIM, 1).astype(BF16)

    q = _dot(cq, w_uq_b[...])
    for h in range(N_HEADS):
        qn_ref[h] = (q[:, h * LANES:(h + 1) * LANES] * scale).astype(BF16)
    for p in range(N_HEADS // 2):
        pair_cols = slice((N_HEADS + p) * LANES, (N_HEADS + p + 1) * LANES)
        qr_ref[p] = (rope(q[:, pair_cols]) * scale).astype(BF16)

    kv = _dot(ckv, w_ukv_b[...])
    for h in range(N_HEADS):
        kn_ref[h] = kv[:, 2 * h * LANES:(2 * h + 1) * LANES].astype(BF16)
        v_ref[h] = kv[:, (2 * h + 1) * LANES:(2 * h + 2) * LANES].astype(BF16)


def _mla_pre(x2, pos, g, w_in, gcq, gckv, w_uq, w_ukv, invf):
    t = x2.shape[0]
    tm = TOKEN_TILE
    scale = math.log2(math.e) / math.sqrt(QK_NOPE_DIM + QK_ROPE_DIM)

    def stacked(n):
        return (pl.BlockSpec((n, tm, LANES), lambda i: (0, i, 0)), jax.ShapeDtypeStruct((n, t, LANES), BF16))

    specs, shapes = zip(stacked(N_HEADS), stacked(N_HEADS // 2), stacked(N_HEADS), stacked(2), stacked(N_HEADS))
    return pl.pallas_call(
        functools.partial(_mla_pre_kernel, scale=scale),
        grid=(t // tm,),
        in_specs=[
            pl.BlockSpec((tm, D_MODEL), lambda i: (i, 0)),
            pl.BlockSpec((tm, 1), lambda i: (i, 0)),
            _const_spec(g.shape), _layer0_spec(w_in), _const_spec(gcq.shape), _const_spec(gckv.shape),
            _layer0_spec(w_uq), _layer0_spec(w_ukv), _const_spec(invf.shape),
        ],
        out_specs=list(specs),
        out_shape=list(shapes),
        scratch_shapes=[
            pltpu.VMEM((D_MODEL, Q_LORA_RANK + KV_LORA_RANK + LANES), BF16),
            pltpu.VMEM(w_uq.shape[1:], BF16),
            pltpu.VMEM(w_ukv.shape[1:], BF16),
        ],
        compiler_params=_params(("arbitrary",)),
        name="mla_pre",
    )(x2, pos, g, w_in, gcq, gckv, w_uq, w_ukv, invf)


def _attn_kernel(*refs, n_cast):
    qn_ref, qr_ref, kn_ref, kr_ref, v_ref = refs[:5]
    o_ref = refs[5 + n_cast]
    m_a, m_b, acc_a, acc_b = refs[-4:]
    _cast_side_work(refs[5:5 + n_cast], refs[6 + n_cast:-4])

    seq = qn_ref.shape[0]
    ta = ATTN_TILE
    groups = ta // LANES
    row_chunk = lax.broadcasted_iota(jnp.int32, (ta, ta), 0) // CHUNK
    col_chunk = lax.broadcasted_iota(jnp.int32, (ta, ta), 1) // CHUNK
    diag_mask = col_chunk <= row_chunk
    ones = jnp.ones((ta, LANES), BF16)

    def tile_slice(i):
        return pl.ds(i * ta, ta)

    def kv_tile(j):
        ks = tile_slice(j)
        k = jnp.concatenate([kn_ref[ks, :], kr_ref[ks, :]], axis=1)
        v_ext = jnp.concatenate([v_ref[ks, :], ones], axis=1)
        return k, v_ext

    def step(qs, m_ref, acc_ref, k, v_ext, masked):
        q = jnp.concatenate([qn_ref[qs, :], qr_ref[qs, :]], axis=1)
        s = lax.dot_general(q, k, (((1,), (1,)), ((), ())), preferred_element_type=F32)
        if masked:
            s = jnp.where(diag_mask, s, -1e30)
        parts = [s[:, g * LANES:(g + 1) * LANES] for g in range(groups)]
        m_prev = m_ref[...]
        m_new = jnp.maximum(m_prev, jnp.max(functools.reduce(jnp.maximum, parts), axis=1, keepdims=True))
        alpha = jnp.exp2(m_prev - m_new)
        p = jnp.concatenate([jnp.exp2(part - m_new).astype(BF16) for part in parts], axis=1)
        acc_ref[...] = acc_ref[...] * jnp.concatenate([alpha, alpha], axis=1) + _dot(p, v_ext)
        m_ref[...] = m_new

    def finish(qs, acc_ref):
        acc = acc_ref[...]
        o_ref[qs, :] = (acc[:, :V_HEAD_DIM] / acc[:, V_HEAD_DIM:]).astype(o_ref.dtype)

    for i in range(seq // (2 * ta)):
        qs_a = tile_slice(2 * i)
        qs_b = tile_slice(2 * i + 1)
        for m_ref, acc_ref in ((m_a, acc_a), (m_b, acc_b)):
            m_ref[...] = jnp.full(m_ref.shape, -1e30, F32)
            acc_ref[...] = jnp.zeros(acc_ref.shape, F32)
        for j in range(2 * i):
            k, v_ext = kv_tile(j)
            step(qs_a, m_a, acc_a, k, v_ext, False)
            step(qs_b, m_b, acc_b, k, v_ext, False)
        k, v_ext = kv_tile(2 * i)
        step(qs_a, m_a, acc_a, k, v_ext, True)
        step(qs_b, m_b, acc_b, k, v_ext, False)
        finish(qs_a, acc_a)
        k, v_ext = kv_tile(2 * i + 1)
        step(qs_b, m_b, acc_b, k, v_ext, True)
        finish(qs_b, acc_b)


def _attention(qn, qr, kn, kr, v, bsz, seq, casts):
    t = bsz * seq
    ta = ATTN_TILE
    head_spec = pl.BlockSpec((None, seq, LANES), lambda b, h: (h, b, 0))
    cast_specs = [_cast_specs(w, layer, bsz * N_HEADS, lambda b, h: b * N_HEADS + h) for w, layer in casts]
    return pl.pallas_call(
        functools.partial(_attn_kernel, n_cast=len(casts)),
        grid=(bsz, N_HEADS),
        in_specs=[head_spec,
                  pl.BlockSpec((None, seq, LANES), lambda b, h: (h // 2, b, 0)),
                  head_spec,
                  pl.BlockSpec((None, seq, LANES), lambda b, h: (h % 2, b, 0)),
                  head_spec] + [c[0] for c in cast_specs],
        out_specs=[head_spec] + [c[1] for c in cast_specs],
        out_shape=[jax.ShapeDtypeStruct((N_HEADS, t, LANES), BF16)] + [c[2] for c in cast_specs],
        scratch_shapes=[pltpu.VMEM((ta, LANES), F32), pltpu.VMEM((ta, LANES), F32),
                        pltpu.VMEM((ta, 2 * LANES), F32), pltpu.VMEM((ta, 2 * LANES), F32)],
        compiler_params=_params(("arbitrary", "arbitrary")),
        name="attention",
    )(qn, qr, kn, kr, v, *[w for w, _ in casts])


def _swiglu(h1, g, wg_ref, wu_ref, wd_ref, act_ref):
    d_ff = wg_ref.shape[1]
    hn = _rms(h1, g).astype(BF16)
    for c in range(d_ff // FF_CHUNK):
        cs = slice(c * FF_CHUNK, (c + 1) * FF_CHUNK)
        gate = _dot(hn, wg_ref[:, cs])
        up = _dot(hn, wu_ref[:, cs])
        act_ref[:, cs] = (jax.nn.silu(gate) * up).astype(BF16)
    return _dot(act_ref[...], wd_ref[...])


def _wo_ffn_kernel(a_ref, h_ref, wo_ref, g_ref, wg_ref, wu_ref, wd_ref, out_ref, act_ref):
    a = jnp.concatenate([a_ref[h] for h in range(N_HEADS)], axis=1)
    h1 = h_ref[...] + _dot(a, wo_ref[...])
    out_ref[...] = h1 + _swiglu(h1, g_ref[0:1, :], wg_ref, wu_ref, wd_ref, act_ref)


def _wo_ffn(attn, h, wo, g, wg, wu, wd):
    t = h.shape[0]
    tm = TOKEN_TILE
    return pl.pallas_call(
        _wo_ffn_kernel,
        grid=(t // tm,),
        in_specs=[
            pl.BlockSpec((N_HEADS, tm, LANES), lambda i: (0, i, 0)),
            pl.BlockSpec((tm, D_MODEL), lambda i: (i, 0)),
            _const_spec(wo.shape), _const_spec(g.shape), _const_spec(wg.shape),
            _const_spec(wu.shape), _const_spec(wd.shape),
        ],
        out_specs=pl.BlockSpec((tm, D_MODEL), lambda i: (i, 0)),
        out_shape=jax.ShapeDtypeStruct((t, D_MODEL), F32),
        scratch_shapes=[pltpu.VMEM((tm, wg.shape[1]), BF16)],
        compiler_params=_params(("parallel",)),
        name="wo_ffn",
    )(attn, h, wo, g, wg, wu, wd)


def _conv_ffn_kernel(h_ref, gc_ref, win_ref, cw_ref, wout_ref, gf_ref, wg_ref, wu_ref, wd_ref,
                     gfin_ref, out_ref, ubuf_ref, hist_ref, act_ref, *, tiles_per_seq):
    tm = h_ref.shape[0]
    pad = SUBLANES

    @pl.when(pl.program_id(0) % tiles_per_seq == 0)
    def _():
        hist_ref[...] = jnp.zeros(hist_ref.shape, F32)

    hn = _rms(h_ref[...], gc_ref[...]).astype(BF16)
    for c in range(D_MODEL // CONV_CHUNK):
        cs = slice(c * CONV_CHUNK, (c + 1) * CONV_CHUNK)
        b_gate = _dot(hn, win_ref[:, c * CONV_CHUNK:(c + 1) * CONV_CHUNK])
        c_gate = _dot(hn, win_ref[:, D_MODEL + c * CONV_CHUNK:D_MODEL + (c + 1) * CONV_CHUNK])
        xp = _dot(hn, win_ref[:, 2 * D_MODEL + c * CONV_CHUNK:2 * D_MODEL + (c + 1) * CONV_CHUNK])
        u = c_gate * xp
        ubuf = ubuf_ref.at[c % 2]
        ubuf[0:pad, :] = hist_ref[:, cs]
        ubuf[pad:pad + tm, :] = u
        hist_ref[:, cs] = ubuf[tm:tm + pad, :]
        u_conv = (cw_ref[0:1, cs] * ubuf[pad - 2:pad - 2 + tm, :]
                  + cw_ref[1:2, cs] * ubuf[pad - 1:pad - 1 + tm, :]
                  + cw_ref[2:3, cs] * u)
        act_ref[:, cs] = (b_gate * u_conv).astype(BF16)

    out_ref[...] = h_ref[...] + _dot(act_ref[:, :D_MODEL], wout_ref[...])
    h2 = out_ref[...] + _swiglu(out_ref[...], gf_ref[1:2, :], wg_ref, wu_ref, wd_ref, act_ref)
    out_ref[...] = _rms(h2, gfin_ref[...])


def _conv_ffn(h, gc, win, cw, wout, gf, wg, wu, wd, gfin, seq):
    t = h.shape[0]
    tm = TOKEN_TILE
    return pl.pallas_call(
        functools.partial(_conv_ffn_kernel, tiles_per_seq=seq // tm),
        grid=(t // tm,),
        in_specs=[
            pl.BlockSpec((tm, D_MODEL), lambda i: (i, 0)),
            _const_spec(gc.shape), _const_spec(win.shape), _layer0_spec(cw),
            _const_spec(wout.shape), _const_spec(gf.shape), _const_spec(wg.shape),
            _const_spec(wu.shape), _const_spec(wd.shape), _const_spec(gfin.shape),
        ],
        out_specs=pl.BlockSpec((tm, D_MODEL), lambda i: (i, 0)),
        out_shape=jax.ShapeDtypeStruct((t, D_MODEL), F32),
        scratch_shapes=[
            pltpu.VMEM((2, tm + SUBLANES, CONV_CHUNK), F32),
            pltpu.VMEM((SUBLANES, D_MODEL), F32),
            pltpu.VMEM((tm, wg.shape[1]), BF16),
        ],
        compiler_params=_params(("arbitrary",)),
        name="conv_ffn",
    )(h, gc, win, cw, wout, gf, wg, wu, wd, gfin)


def _rope_inv_freq_lanes():
    inv = 1.0 / (ROPE_THETA ** (np.arange(0, QK_ROPE_DIM, 2, dtype=np.float32) / np.float32(QK_ROPE_DIM)))
    inv = inv.astype(np.float32)
    return jnp.asarray(np.tile(inv, LANES // (QK_ROPE_DIM // 2))[None, :])


def kernel(x, positions, mla_norm, mla_w_in, mla_g_cq, mla_g_ckv, mla_w_uq, mla_w_ukv, mla_w_o,
           conv_norm, conv_w_in, conv_w, conv_w_out, ffn_norm, ffn_w_gate, ffn_w_up, ffn_w_down,
           final_norm):
    bsz, seq, d = x.shape
    t = bsz * seq
    assert d == D_MODEL and seq % TOKEN_TILE == 0 and seq % (2 * ATTN_TILE) == 0 and ATTN_TILE % CHUNK == 0
    x2 = x.reshape(t, d)
    qn, qr, kn, kr, v = _mla_pre(
        x2, positions.reshape(t, 1), mla_norm, mla_w_in, mla_g_cq, mla_g_ckv, mla_w_uq, mla_w_ukv,
        _rope_inv_freq_lanes())
    later_weights = [(mla_w_o, 0), (ffn_w_gate, 0), (ffn_w_up, 0), (ffn_w_down, 0),
                     (conv_w_in, 0), (conv_w_out, 0), (ffn_w_gate, 1), (ffn_w_up, 1), (ffn_w_down, 1)]
    (attn, wo_b, wg0_b, wu0_b, wd0_b, cwin_b, cwout_b, wg1_b, wu1_b, wd1_b) = _attention(
        qn, qr, kn, kr, v, bsz, seq, later_weights)
    h = _wo_ffn(attn, x2, wo_b, ffn_norm, wg0_b, wu0_b, wd0_b)
    out = _conv_ffn(h, conv_norm, cwin_b, conv_w, cwout_b, ffn_norm,
                    wg1_b, wu1_b, wd1_b, final_norm[None, :], seq)
    return out.reshape(bsz, seq, d)
```

```python
import functools
import math

import jax
import jax.numpy as jnp
import numpy as np
from jax import lax
from jax.experimental import pallas as pl
from jax.experimental.pallas import tpu as pltpu

D_MODEL = 1024
CHUNK = 64
N_HEADS = 8
QK_NOPE_DIM = 128
QK_ROPE_DIM = 64
V_HEAD_DIM = 128
Q_LORA_RANK = 512
KV_LORA_RANK = 256
ROPE_THETA = 10000.0
CONV_WIDTH = 3
RMS_EPS = 1e-6

LANES = 128
SUBLANES = 8
BF16_SUBLANES = 16
MXU_COLS = 256
VMEM_LIMIT_BYTES = 56 * 1024 * 1024

TOKEN_TILE = 1024
ATTN_TILE = 256
ATTN_CHAINS = 2
FF_CHUNK = MXU_COLS
CONV_CHUNK = MXU_COLS

BF16 = jnp.bfloat16
F32 = jnp.float32


def _dot(a, b):
    return jnp.dot(a, b, preferred_element_type=F32)


def _rms(x, g):
    return x * lax.rsqrt(jnp.mean(x * x, axis=-1, keepdims=True) + RMS_EPS) * g


def _const_spec(shape):
    return pl.BlockSpec(shape, lambda *_: (0,) * len(shape), pipeline_mode=pl.Buffered(1))


def _layer0_spec(w):
    return pl.BlockSpec((None,) + w.shape[1:], lambda *_: (0,) * w.ndim, pipeline_mode=pl.Buffered(1))


def _params(semantics):
    return pltpu.CompilerParams(dimension_semantics=semantics, vmem_limit_bytes=VMEM_LIMIT_BYTES)


def _cast_specs(w, layer, steps, step_of):
    _, rows, cols = w.shape
    out_shape = jax.ShapeDtypeStruct((rows, cols), BF16)
    if rows % (steps * BF16_SUBLANES) == 0:
        rb = rows // steps
        return (pl.BlockSpec((None, rb, cols), lambda *g: (layer, step_of(*g), 0)),
                pl.BlockSpec((rb, cols), lambda *g: (step_of(*g), 0)), out_shape)
    per = steps // (cols // LANES)
    return (pl.BlockSpec((None, rows, LANES), lambda *g: (layer, 0, step_of(*g) // per)),
            pl.BlockSpec((rows, LANES), lambda *g: (0, step_of(*g) // per)), out_shape)


def _cast_side_work(cast_in_refs, cast_out_refs):
    for src, dst in zip(cast_in_refs, cast_out_refs):
        dst[...] = src[...].astype(BF16)


def _mla_pre_kernel(x_ref, pos_ref, g_ref, w_in_ref, gcq_ref, gckv_ref, w_uq_ref, w_ukv_ref, invf_ref,
                    qn_ref, qr_ref, kn_ref, kr_ref, v_ref, w_in_b, w_uq_b, w_ukv_b, *, scale):
    tm = x_ref.shape[0]
    kr_col = Q_LORA_RANK + KV_LORA_RANK

    @pl.when(pl.program_id(0) == 0)
    def _():
        w_in_b[:, :kr_col] = w_in_ref[:, :kr_col].astype(BF16)
        w_in_b[:, kr_col:] = jnp.zeros((D_MODEL, LANES), BF16)
        w_in_b[:, kr_col:kr_col + QK_ROPE_DIM] = w_in_ref[:, kr_col:].astype(BF16)
        pair = 2 * (QK_NOPE_DIM + QK_ROPE_DIM)
        dst = lax.broadcasted_iota(jnp.int32, (pair, pair), 1)
        src = jnp.where(dst < QK_NOPE_DIM, dst,
                        jnp.where(dst < 2 * QK_NOPE_DIM, dst + QK_ROPE_DIM,
                                  jnp.where(dst < 2 * QK_NOPE_DIM + QK_ROPE_DIM, dst - QK_NOPE_DIM, dst)))
        select = (lax.broadcasted_iota(jnp.int32, (pair, pair), 0) == src).astype(BF16)
        n_nope = N_HEADS * QK_NOPE_DIM
        for p in range(N_HEADS // 2):
            wp = _dot(w_uq_ref[:, p * pair:(p + 1) * pair].astype(BF16), select).astype(BF16)
            w_uq_b[:, 2 * p * QK_NOPE_DIM:2 * (p + 1) * QK_NOPE_DIM] = wp[:, :2 * QK_NOPE_DIM]
            w_uq_b[:, n_nope + p * LANES:n_nope + (p + 1) * LANES] = wp[:, 2 * QK_NOPE_DIM:]
        w_ukv_b[...] = w_ukv_ref[...].astype(BF16)

    xn = _rms(x_ref[...], g_ref[...]).astype(BF16)
    proj = _dot(xn, w_in_b[...])
    cq = _rms(proj[:, :Q_LORA_RANK], gcq_ref[...]).astype(BF16)
    ckv = _rms(proj[:, Q_LORA_RANK:kr_col], gckv_ref[...]).astype(BF16)
    kr = proj[:, kr_col:]

    blk = tm // 4
    n_freq = QK_ROPE_DIM // 2
    lane_c = lax.broadcasted_iota(jnp.int32, (blk, LANES), 1)
    grp_c = lane_c // n_freq
    pos = pos_ref[...].astype(F32)
    pos_c = pos[3 * blk:, :]
    for a in (2, 1, 0):
        pos_c = jnp.where(grp_c == a, pos[a * blk:(a + 1) * blk, :], pos_c)
    ang_c = pos_c * invf_ref[...]
    cos_c = jnp.cos(ang_c)
    sin_c = jnp.sin(ang_c)
    cos_r = [cos_c] + [pltpu.roll(cos_c, n_freq * k, 1) for k in (1, 2, 3)]
    sin_r = [sin_c] + [pltpu.roll(sin_c, n_freq * k, 1) for k in (1, 2, 3)]
    nsin_r = [-t for t in sin_r]

    def expand(a, even_tabs, odd_tabs):
        out = odd_tabs[(3 - a) % 4]
        for g in (2, 1, 0):
            tabs = even_tabs if g % 2 == 0 else odd_tabs
            out = jnp.where(grp_c == g, tabs[(g - a) % 4], out)
        return out

    cos = jnp.concatenate([expand(a, cos_r, cos_r) for a in range(4)], axis=0)
    sin_signed = jnp.concatenate([expand(a, nsin_r, sin_r) for a in range(4)], axis=0)
    lane = lax.broadcasted_iota(jnp.int32, (tm, LANES), 1)
    first_half = (lane % QK_ROPE_DIM) < n_freq

    def rope(t):
        partner = jnp.where(first_half,
                            pltpu.roll(t, LANES - n_freq, 1),
                            pltpu.roll(t, n_freq, 1))
        return t * cos + partner * sin_signed

    kr_rot = rope(kr)
    kr_ref[0] = kr_rot.astype(BF16)
    kr_ref[1] = pltpu.roll(kr_rot, QK_ROPE_DIM, 1).astype(BF16)

    q = _dot(cq, w_uq_b[...])
    for h in range(N_HEADS):
        qn_ref[h] = (q[:, h * LANES:(h + 1) * LANES] * scale).astype(BF16)
    for p in range(N_HEADS // 2):
        pair_cols = slice((N_HEADS + p) * LANES, (N_HEADS + p + 1) * LANES)
        qr_ref[p] = (rope(q[:, pair_cols]) * scale).astype(BF16)

    kv = _dot(ckv, w_ukv_b[...])
    for h in range(N_HEADS):
        kn_ref[h] = kv[:, 2 * h * LANES:(2 * h + 1) * LANES].astype(BF16)
        v_ref[h] = kv[:, (2 * h + 1) * LANES:(2 * h + 2) * LANES].astype(BF16)


def _mla_pre(x2, pos, g, w_in, gcq, gckv, w_uq, w_ukv, invf):
    t = x2.shape[0]
    tm = TOKEN_TILE
    scale = math.log2(math.e) / math.sqrt(QK_NOPE_DIM + QK_ROPE_DIM)

    def stacked(n):
        return (pl.BlockSpec((n, tm, LANES), lambda i: (0, i, 0)), jax.ShapeDtypeStruct((n, t, LANES), BF16))

    specs, shapes = zip(stacked(N_HEADS), stacked(N_HEADS // 2), stacked(N_HEADS), stacked(2), stacked(N_HEADS))
    return pl.pallas_call(
        functools.partial(_mla_pre_kernel, scale=scale),
        grid=(t // tm,),
        in_specs=[
            pl.BlockSpec((tm, D_MODEL), lambda i: (i, 0)),
            pl.BlockSpec((tm, LANES), lambda i: (i, 0)),
            _const_spec(g.shape), _const_spec(w_in.shape), _const_spec(gcq.shape), _const_spec(gckv.shape),
            _layer0_spec(w_uq), _layer0_spec(w_ukv), _const_spec(invf.shape),
        ],
        out_specs=list(specs),
        out_shape=list(shapes),
        scratch_shapes=[
            pltpu.VMEM((D_MODEL, Q_LORA_RANK + KV_LORA_RANK + LANES), BF16),
            pltpu.VMEM(w_uq.shape[1:], BF16),
            pltpu.VMEM(w_ukv.shape[1:], BF16),
        ],
        compiler_params=_params(("arbitrary",)),
        name="mla_pre",
    )(x2, pos, g, w_in, gcq, gckv, w_uq, w_ukv, invf)


def _attn_kernel(*refs, n_cast):
    qn_ref, qr_ref, kn_ref, kr_ref, v_ref = refs[:5]
    o_ref = refs[5 + n_cast]
    m_ref, acc_ref = refs[-2:]
    _cast_side_work(refs[5:5 + n_cast], refs[6 + n_cast:-2])

    seq = qn_ref.shape[0]
    ta = ATTN_TILE
    groups = ta // LANES
    row_chunk = lax.broadcasted_iota(jnp.int32, (ta, ta), 0) // CHUNK
    col_chunk = lax.broadcasted_iota(jnp.int32, (ta, ta), 1) // CHUNK
    diag_mask = col_chunk <= row_chunk
    ones = jnp.ones((ta, LANES), BF16)

    def tile_slice(i):
        return pl.ds(i * ta, ta)

    def kv_tile(j):
        ks = tile_slice(j)
        k = jnp.concatenate([kn_ref[ks, :], kr_ref[ks, :]], axis=1)
        v_ext = jnp.concatenate([v_ref[ks, :], ones], axis=1)
        return k, v_ext

    def step(c, qs, k, v_ext, masked):
        q = jnp.concatenate([qn_ref[qs, :], qr_ref[qs, :]], axis=1)
        s = lax.dot_general(q, k, (((1,), (1,)), ((), ())), preferred_element_type=F32)
        if masked:
            s = jnp.where(diag_mask, s, -1e30)
        parts = [s[:, g * LANES:(g + 1) * LANES] for g in range(groups)]
        m_prev = m_ref[c]
        m_new = jnp.maximum(m_prev, jnp.max(functools.reduce(jnp.maximum, parts), axis=1, keepdims=True))
        alpha = jnp.exp2(m_prev - m_new)
        p = jnp.concatenate([jnp.exp2(part - m_new).astype(BF16) for part in parts], axis=1)
        acc_ref[c] = acc_ref[c] * jnp.concatenate([alpha, alpha], axis=1) + _dot(p, v_ext)
        m_ref[c] = m_new

    def finish(c, qs):
        acc = acc_ref[c]
        o_ref[qs, :] = (acc[:, :V_HEAD_DIM] / acc[:, V_HEAD_DIM:]).astype(o_ref.dtype)

    for group in range(seq // (ATTN_CHAINS * ta)):
        first = group * ATTN_CHAINS
        m_ref[...] = jnp.full(m_ref.shape, -1e30, F32)
        acc_ref[...] = jnp.zeros(acc_ref.shape, F32)
        for j in range(first + ATTN_CHAINS):
            k, v_ext = kv_tile(j)
            for c in range(max(0, j - first), ATTN_CHAINS):
                step(c, tile_slice(first + c), k, v_ext, masked=(j == first + c))
            if j >= first:
                finish(j - first, tile_slice(j))


def _attention(qn, qr, kn, kr, v, bsz, seq, casts):
    t = bsz * seq
    ta = ATTN_TILE
    head_spec = pl.BlockSpec((None, seq, LANES), lambda b, h: (h, b, 0))
    cast_specs = [_cast_specs(w, layer, bsz * N_HEADS, lambda b, h: b * N_HEADS + h) for w, layer in casts]
    return pl.pallas_call(
        functools.partial(_attn_kernel, n_cast=len(casts)),
        grid=(bsz, N_HEADS),
        in_specs=[head_spec,
                  pl.BlockSpec((None, seq, LANES), lambda b, h: (h // 2, b, 0)),
                  head_spec,
                  pl.BlockSpec((None, seq, LANES), lambda b, h: (h % 2, b, 0)),
                  head_spec] + [c[0] for c in cast_specs],
        out_specs=[head_spec] + [c[1] for c in cast_specs],
        out_shape=[jax.ShapeDtypeStruct((N_HEADS, t, LANES), BF16)] + [c[2] for c in cast_specs],
        scratch_shapes=[pltpu.VMEM((ATTN_CHAINS, ta, LANES), F32),
                        pltpu.VMEM((ATTN_CHAINS, ta, 2 * LANES), F32)],
        compiler_params=_params(("arbitrary", "arbitrary")),
        name="attention",
    )(qn, qr, kn, kr, v, *[w for w, _ in casts])


def _swiglu(h1, g, wg_ref, wu_ref, wd_ref, act_ref):
    d_ff = wg_ref.shape[1]
    hn = _rms(h1, g).astype(BF16)
    for c in range(d_ff // FF_CHUNK):
        cs = slice(c * FF_CHUNK, (c + 1) * FF_CHUNK)
        gate = _dot(hn, wg_ref[:, cs])
        up = _dot(hn, wu_ref[:, cs])
        act_ref[:, cs] = (jax.nn.silu(gate) * up).astype(BF16)
    return _dot(act_ref[...], wd_ref[...])


def _wo_ffn_kernel(a_ref, h_ref, wo_ref, g_ref, wg_ref, wu_ref, wd_ref, out_ref, act_ref):
    a = jnp.concatenate([a_ref[h] for h in range(N_HEADS)], axis=1)
    h1 = h_ref[...] + _dot(a, wo_ref[...])
    out_ref[...] = h1 + _swiglu(h1, g_ref[0:1, :], wg_ref, wu_ref, wd_ref, act_ref)


def _wo_ffn(attn, h, wo, g, wg, wu, wd):
    t = h.shape[0]
    tm = TOKEN_TILE
    return pl.pallas_call(
        _wo_ffn_kernel,
        grid=(t // tm,),
        in_specs=[
            pl.BlockSpec((N_HEADS, tm, LANES), lambda i: (0, i, 0)),
            pl.BlockSpec((tm, D_MODEL), lambda i: (i, 0)),
            _const_spec(wo.shape), _const_spec(g.shape), _const_spec(wg.shape),
            _const_spec(wu.shape), _const_spec(wd.shape),
        ],
        out_specs=pl.BlockSpec((tm, D_MODEL), lambda i: (i, 0)),
        out_shape=jax.ShapeDtypeStruct((t, D_MODEL), F32),
        scratch_shapes=[pltpu.VMEM((tm, wg.shape[1]), BF16)],
        compiler_params=_params(("parallel",)),
        name="wo_ffn",
    )(attn, h, wo, g, wg, wu, wd)


def _conv_ffn_kernel(h_ref, gc_ref, win_ref, cw_ref, wout_ref, gf_ref, wg_ref, wu_ref, wd_ref,
                     gfin_ref, out_ref, ubuf_ref, hist_ref, act_ref, *, tiles_per_seq):
    tm = h_ref.shape[0]
    pad = SUBLANES

    @pl.when(pl.program_id(0) % tiles_per_seq == 0)
    def _():
        hist_ref[...] = jnp.zeros(hist_ref.shape, F32)

    hn = _rms(h_ref[...], gc_ref[...]).astype(BF16)
    for c in range(D_MODEL // CONV_CHUNK):
        cs = slice(c * CONV_CHUNK, (c + 1) * CONV_CHUNK)
        b_gate = _dot(hn, win_ref[:, c * CONV_CHUNK:(c + 1) * CONV_CHUNK])
        c_gate = _dot(hn, win_ref[:, D_MODEL + c * CONV_CHUNK:D_MODEL + (c + 1) * CONV_CHUNK])
        xp = _dot(hn, win_ref[:, 2 * D_MODEL + c * CONV_CHUNK:2 * D_MODEL + (c + 1) * CONV_CHUNK])
        u = c_gate * xp
        ubuf = ubuf_ref.at[c % 2]
        ubuf[0:pad, :] = hist_ref[:, cs]
        ubuf[pad:pad + tm, :] = u
        hist_ref[:, cs] = ubuf[tm:tm + pad, :]
        u_conv = (cw_ref[0:1, cs] * ubuf[pad - 2:pad - 2 + tm, :]
                  + cw_ref[1:2, cs] * ubuf[pad - 1:pad - 1 + tm, :]
                  + cw_ref[2:3, cs] * u)
        act_ref[:, cs] = (b_gate * u_conv).astype(BF16)

    out_ref[...] = h_ref[...] + _dot(act_ref[:, :D_MODEL], wout_ref[...])
    h2 = out_ref[...] + _swiglu(out_ref[...], gf_ref[1:2, :], wg_ref, wu_ref, wd_ref, act_ref)
    out_ref[...] = _rms(h2, gfin_ref[...])


def _conv_ffn(h, gc, win, cw, wout, gf, wg, wu, wd, gfin, seq):
    t = h.shape[0]
    tm = TOKEN_TILE
    return pl.pallas_call(
        functools.partial(_conv_ffn_kernel, tiles_per_seq=seq // tm),
        grid=(t // tm,),
        in_specs=[
            pl.BlockSpec((tm, D_MODEL), lambda i: (i, 0)),
            _const_spec(gc.shape), _const_spec(win.shape), _layer0_spec(cw),
            _const_spec(wout.shape), _const_spec(gf.shape), _const_spec(wg.shape),
            _const_spec(wu.shape), _const_spec(wd.shape), _const_spec(gfin.shape),
        ],
        out_specs=pl.BlockSpec((tm, D_MODEL), lambda i: (i, 0)),
        out_shape=jax.ShapeDtypeStruct((t, D_MODEL), F32),
        scratch_shapes=[
            pltpu.VMEM((2, tm + SUBLANES, CONV_CHUNK), F32),
            pltpu.VMEM((SUBLANES, D_MODEL), F32),
            pltpu.VMEM((tm, wg.shape[1]), BF16),
        ],
        compiler_params=_params(("arbitrary",)),
        name="conv_ffn",
    )(h, gc, win, cw, wout, gf, wg, wu, wd, gfin)


def _rope_inv_freq_lanes():
    inv = 1.0 / (ROPE_THETA ** (np.arange(0, QK_ROPE_DIM, 2, dtype=np.float32) / np.float32(QK_ROPE_DIM)))
    inv = inv.astype(np.float32)
    return jnp.asarray(np.tile(inv, LANES // (QK_ROPE_DIM // 2))[None, :])


def kernel(x, positions, mla_norm, mla_w_in, mla_g_cq, mla_g_ckv, mla_w_uq, mla_w_ukv, mla_w_o,
           conv_norm, conv_w_in, conv_w, conv_w_out, ffn_norm, ffn_w_gate, ffn_w_up, ffn_w_down,
           final_norm):
    bsz, seq, d = x.shape
    t = bsz * seq
    assert d == D_MODEL and seq % TOKEN_TILE == 0 and seq % (ATTN_CHAINS * ATTN_TILE) == 0 and ATTN_TILE % CHUNK == 0
    x2 = x.reshape(t, d)
    pos = jnp.broadcast_to(positions.reshape(t, 1), (t, LANES))
    qn, qr, kn, kr, v = _mla_pre(
        x2, pos, mla_norm, mla_w_in.reshape(mla_w_in.shape[1:]), mla_g_cq, mla_g_ckv, mla_w_uq, mla_w_ukv,
        _rope_inv_freq_lanes())
    later_weights = [(mla_w_o, 0), (ffn_w_gate, 0), (ffn_w_up, 0), (ffn_w_down, 0),
                     (conv_w_in, 0), (conv_w_out, 0), (ffn_w_gate, 1), (ffn_w_up, 1), (ffn_w_down, 1)]
    (attn, wo_b, wg0_b, wu0_b, wd0_b, cwin_b, cwout_b, wg1_b, wu1_b, wd1_b) = _attention(
        qn, qr, kn, kr, v, bsz, seq, later_weights)
    h = _wo_ffn(attn, x2, wo_b, ffn_norm, wg0_b, wu0_b, wd0_b)
    out = _conv_ffn(h, conv_norm, cwin_b, conv_w, cwout_b, ffn_norm,
                    wg1_b, wu1_b, wd1_b, final_norm[None, :], seq)
    return out.reshape(bsz, seq, d)
```

```python
import functools
import math

import jax
import jax.numpy as jnp
import numpy as np
from jax import lax
from jax.experimental import pallas as pl
from jax.experimental.pallas import tpu as pltpu

D_MODEL = 1024
CHUNK = 64
N_HEADS = 8
QK_NOPE_DIM = 128
QK_ROPE_DIM = 64
V_HEAD_DIM = 128
Q_LORA_RANK = 512
KV_LORA_RANK = 256
ROPE_THETA = 10000.0
CONV_WIDTH = 3
RMS_EPS = 1e-6

LANES = 128
SUBLANES = 8
BF16_SUBLANES = 16
MXU_COLS = 256
VMEM_LIMIT_BYTES = 56 * 1024 * 1024

TOKEN_TILE = 1024
ATTN_TILE = 256
ATTN_CHAINS = 2
FF_CHUNK = MXU_COLS
CONV_CHUNK = MXU_COLS

BF16 = jnp.bfloat16
F32 = jnp.float32


def _dot(a, b):
    return jnp.dot(a, b, preferred_element_type=F32)


def _rms(x, g):
    return x * lax.rsqrt(jnp.mean(x * x, axis=-1, keepdims=True) + RMS_EPS) * g


def _const_spec(shape):
    return pl.BlockSpec(shape, lambda *_: (0,) * len(shape), pipeline_mode=pl.Buffered(1))


def _layer0_spec(w):
    return pl.BlockSpec((None,) + w.shape[1:], lambda *_: (0,) * w.ndim, pipeline_mode=pl.Buffered(1))


def _params(semantics):
    return pltpu.CompilerParams(dimension_semantics=semantics, vmem_limit_bytes=VMEM_LIMIT_BYTES)


def _cast_specs(w, layer, steps, step_of):
    _, rows, cols = w.shape
    out_shape = jax.ShapeDtypeStruct((rows, cols), BF16)
    if rows % (steps * BF16_SUBLANES) == 0:
        rb = rows // steps
        return (pl.BlockSpec((None, rb, cols), lambda *g: (layer, step_of(*g), 0)),
                pl.BlockSpec((rb, cols), lambda *g: (step_of(*g), 0)), out_shape)
    per = steps // (cols // LANES)
    return (pl.BlockSpec((None, rows, LANES), lambda *g: (layer, 0, step_of(*g) // per)),
            pl.BlockSpec((rows, LANES), lambda *g: (0, step_of(*g) // per)), out_shape)


def _cast_side_work(cast_in_refs, cast_out_refs):
    for src, dst in zip(cast_in_refs, cast_out_refs):
        dst[...] = src[...].astype(BF16)


def _mla_pre_kernel(x_ref, pos_ref, g_ref, w_in_ref, gcq_ref, gckv_ref, w_uq_ref, w_ukv_ref, invf_ref,
                    qn_ref, qr_ref, kn_ref, kr_ref, v_ref, w_in_b, w_uq_b, w_ukv_b, *, scale):
    tm = x_ref.shape[0]
    kr_col = Q_LORA_RANK + KV_LORA_RANK

    @pl.when(pl.program_id(0) == 0)
    def _():
        w_in_b[...] = w_in_ref[...].astype(BF16)
        pair = 2 * (QK_NOPE_DIM + QK_ROPE_DIM)
        dst = lax.broadcasted_iota(jnp.int32, (pair, pair), 1)
        src = jnp.where(dst < QK_NOPE_DIM, dst,
                        jnp.where(dst < 2 * QK_NOPE_DIM, dst + QK_ROPE_DIM,
                                  jnp.where(dst < 2 * QK_NOPE_DIM + QK_ROPE_DIM, dst - QK_NOPE_DIM, dst)))
        select = (lax.broadcasted_iota(jnp.int32, (pair, pair), 0) == src).astype(BF16)
        n_nope = N_HEADS * QK_NOPE_DIM
        for p in range(N_HEADS // 2):
            wp = _dot(w_uq_ref[:, p * pair:(p + 1) * pair].astype(BF16), select).astype(BF16)
            w_uq_b[:, 2 * p * QK_NOPE_DIM:2 * (p + 1) * QK_NOPE_DIM] = wp[:, :2 * QK_NOPE_DIM]
            w_uq_b[:, n_nope + p * LANES:n_nope + (p + 1) * LANES] = wp[:, 2 * QK_NOPE_DIM:]
        w_ukv_b[...] = w_ukv_ref[...].astype(BF16)

    xn = _rms(x_ref[...], g_ref[...]).astype(BF16)
    proj = _dot(xn, w_in_b[...])
    cq = _rms(proj[:, :Q_LORA_RANK], gcq_ref[...]).astype(BF16)
    ckv = _rms(proj[:, Q_LORA_RANK:kr_col], gckv_ref[...]).astype(BF16)
    kr = proj[:, kr_col:]

    blk = tm // 4
    n_freq = QK_ROPE_DIM // 2
    lane_c = lax.broadcasted_iota(jnp.int32, (blk, LANES), 1)
    grp_c = lane_c // n_freq
    pos = pos_ref[...].astype(F32)
    pos_c = pos[3 * blk:, :]
    for a in (2, 1, 0):
        pos_c = jnp.where(grp_c == a, pos[a * blk:(a + 1) * blk, :], pos_c)
    ang_c = pos_c * invf_ref[...]
    cos_c = jnp.cos(ang_c)
    sin_c = jnp.sin(ang_c)
    cos_r = [cos_c] + [pltpu.roll(cos_c, n_freq * k, 1) for k in (1, 2, 3)]
    sin_r = [sin_c] + [pltpu.roll(sin_c, n_freq * k, 1) for k in (1, 2, 3)]
    nsin_r = [-t for t in sin_r]

    def expand(a, even_tabs, odd_tabs):
        out = odd_tabs[(3 - a) % 4]
        for g in (2, 1, 0):
            tabs = even_tabs if g % 2 == 0 else odd_tabs
            out = jnp.where(grp_c == g, tabs[(g - a) % 4], out)
        return out

    cos = jnp.concatenate([expand(a, cos_r, cos_r) for a in range(4)], axis=0)
    sin_signed = jnp.concatenate([expand(a, nsin_r, sin_r) for a in range(4)], axis=0)
    lane = lax.broadcasted_iota(jnp.int32, (tm, LANES), 1)
    first_half = (lane % QK_ROPE_DIM) < n_freq

    def rope(t):
        partner = jnp.where(first_half,
                            pltpu.roll(t, LANES - n_freq, 1),
                            pltpu.roll(t, n_freq, 1))
        return t * cos + partner * sin_signed

    kr_rot = rope(kr)
    kr_ref[0] = kr_rot.astype(BF16)
    kr_ref[1] = pltpu.roll(kr_rot, QK_ROPE_DIM, 1).astype(BF16)

    q = _dot(cq, w_uq_b[...])
    for h in range(N_HEADS):
        qn_ref[h] = (q[:, h * LANES:(h + 1) * LANES] * scale).astype(BF16)
    for p in range(N_HEADS // 2):
        pair_cols = slice((N_HEADS + p) * LANES, (N_HEADS + p + 1) * LANES)
        qr_ref[p] = (rope(q[:, pair_cols]) * scale).astype(BF16)

    kv = _dot(ckv, w_ukv_b[...])
    for h in range(N_HEADS):
        kn_ref[h] = kv[:, 2 * h * LANES:(2 * h + 1) * LANES].astype(BF16)
        v_ref[h] = kv[:, (2 * h + 1) * LANES:(2 * h + 2) * LANES].astype(BF16)


def _mla_pre(x2, pos, g, w_in, gcq, gckv, w_uq, w_ukv, invf):
    t = x2.shape[0]
    tm = TOKEN_TILE
    scale = math.log2(math.e) / math.sqrt(QK_NOPE_DIM + QK_ROPE_DIM)

    def stacked(n):
        return (pl.BlockSpec((n, tm, LANES), lambda i: (0, i, 0)), jax.ShapeDtypeStruct((n, t, LANES), BF16))

    specs, shapes = zip(stacked(N_HEADS), stacked(N_HEADS // 2), stacked(N_HEADS), stacked(2), stacked(N_HEADS))
    return pl.pallas_call(
        functools.partial(_mla_pre_kernel, scale=scale),
        grid=(t // tm,),
        in_specs=[
            pl.BlockSpec((tm, D_MODEL), lambda i: (i, 0)),
            pl.BlockSpec((tm, 1), lambda i: (i, 0)),
            _const_spec(g.shape), _layer0_spec(w_in), _const_spec(gcq.shape), _const_spec(gckv.shape),
            _layer0_spec(w_uq), _layer0_spec(w_ukv), _const_spec(invf.shape),
        ],
        out_specs=list(specs),
        out_shape=list(shapes),
        scratch_shapes=[
            pltpu.VMEM(w_in.shape[1:], BF16),
            pltpu.VMEM(w_uq.shape[1:], BF16),
            pltpu.VMEM(w_ukv.shape[1:], BF16),
        ],
        compiler_params=_params(("arbitrary",)),
        name="mla_pre",
    )(x2, pos, g, w_in, gcq, gckv, w_uq, w_ukv, invf)


def _attn_kernel(*refs, n_cast):
    qn_ref, qr_ref, kn_ref, kr_ref, v_ref = refs[:5]
    o_ref = refs[5 + n_cast]
    m_ref, acc_ref = refs[-2:]
    _cast_side_work(refs[5:5 + n_cast], refs[6 + n_cast:-2])

    seq = qn_ref.shape[0]
    ta = ATTN_TILE
    groups = ta // LANES
    row_chunk = lax.broadcasted_iota(jnp.int32, (ta, ta), 0) // CHUNK
    col_chunk = lax.broadcasted_iota(jnp.int32, (ta, ta), 1) // CHUNK
    diag_mask = col_chunk <= row_chunk
    ones = jnp.ones((ta, LANES), BF16)

    def tile_slice(i):
        return pl.ds(i * ta, ta)

    def kv_tile(j):
        ks = tile_slice(j)
        k = jnp.concatenate([kn_ref[ks, :], kr_ref[ks, :]], axis=1)
        v_ext = jnp.concatenate([v_ref[ks, :], ones], axis=1)
        return k, v_ext

    def step(c, qs, k, v_ext, masked):
        q = jnp.concatenate([qn_ref[qs, :], qr_ref[qs, :]], axis=1)
        s = lax.dot_general(q, k, (((1,), (1,)), ((), ())), preferred_element_type=F32)
        if masked:
            s = jnp.where(diag_mask, s, -1e30)
        parts = [s[:, g * LANES:(g + 1) * LANES] for g in range(groups)]
        m_prev = m_ref[c]
        m_new = jnp.maximum(m_prev, jnp.max(functools.reduce(jnp.maximum, parts), axis=1, keepdims=True))
        alpha = jnp.exp2(m_prev - m_new)
        p = jnp.concatenate([jnp.exp2(part - m_new).astype(BF16) for part in parts], axis=1)
        acc_ref[c] = acc_ref[c] * jnp.concatenate([alpha, alpha], axis=1) + _dot(p, v_ext)
        m_ref[c] = m_new

    def finish(c, qs):
        acc = acc_ref[c]
        o_ref[qs, :] = (acc[:, :V_HEAD_DIM] / acc[:, V_HEAD_DIM:]).astype(o_ref.dtype)

    for group in range(seq // (ATTN_CHAINS * ta)):
        first = group * ATTN_CHAINS
        m_ref[...] = jnp.full(m_ref.shape, -1e30, F32)
        acc_ref[...] = jnp.zeros(acc_ref.shape, F32)
        for j in range(first + ATTN_CHAINS):
            k, v_ext = kv_tile(j)
            for c in range(max(0, j - first), ATTN_CHAINS):
                step(c, tile_slice(first + c), k, v_ext, masked=(j == first + c))
            if j >= first:
                finish(j - first, tile_slice(j))


def _attention(qn, qr, kn, kr, v, bsz, seq, casts):
    t = bsz * seq
    ta = ATTN_TILE
    head_spec = pl.BlockSpec((None, seq, LANES), lambda b, h: (h, b, 0))
    cast_specs = [_cast_specs(w, layer, bsz * N_HEADS, lambda b, h: b * N_HEADS + h) for w, layer in casts]
    return pl.pallas_call(
        functools.partial(_attn_kernel, n_cast=len(casts)),
        grid=(bsz, N_HEADS),
        in_specs=[head_spec,
                  pl.BlockSpec((None, seq, LANES), lambda b, h: (h // 2, b, 0)),
                  head_spec,
                  pl.BlockSpec((None, seq, LANES), lambda b, h: (h % 2, b, 0)),
                  head_spec] + [c[0] for c in cast_specs],
        out_specs=[head_spec] + [c[1] for c in cast_specs],
        out_shape=[jax.ShapeDtypeStruct((N_HEADS, t, LANES), BF16)] + [c[2] for c in cast_specs],
        scratch_shapes=[pltpu.VMEM((ATTN_CHAINS, ta, LANES), F32),
                        pltpu.VMEM((ATTN_CHAINS, ta, 2 * LANES), F32)],
        compiler_params=_params(("arbitrary", "arbitrary")),
        name="attention",
    )(qn, qr, kn, kr, v, *[w for w, _ in casts])


def _swiglu(h1, g, wg_ref, wu_ref, wd_ref, act_ref):
    d_ff = wg_ref.shape[1]
    hn = _rms(h1, g).astype(BF16)
    for c in range(d_ff // FF_CHUNK):
        cs = slice(c * FF_CHUNK, (c + 1) * FF_CHUNK)
        gate = _dot(hn, wg_ref[:, cs])
        up = _dot(hn, wu_ref[:, cs])
        act_ref[:, cs] = (jax.nn.silu(gate) * up).astype(BF16)
    return _dot(act_ref[...], wd_ref[...])


def _wo_ffn_kernel(a_ref, h_ref, wo_ref, g_ref, wg_ref, wu_ref, wd_ref, out_ref, act_ref):
    a = jnp.concatenate([a_ref[h] for h in range(N_HEADS)], axis=1)
    h1 = h_ref[...] + _dot(a, wo_ref[...])
    out_ref[...] = h1 + _swiglu(h1, g_ref[0:1, :], wg_ref, wu_ref, wd_ref, act_ref)


def _wo_ffn(attn, h, wo, g, wg, wu, wd):
    t = h.shape[0]
    tm = TOKEN_TILE
    return pl.pallas_call(
        _wo_ffn_kernel,
        grid=(t // tm,),
        in_specs=[
            pl.BlockSpec((N_HEADS, tm, LANES), lambda i: (0, i, 0)),
            pl.BlockSpec((tm, D_MODEL), lambda i: (i, 0)),
            _const_spec(wo.shape), _const_spec(g.shape), _const_spec(wg.shape),
            _const_spec(wu.shape), _const_spec(wd.shape),
        ],
        out_specs=pl.BlockSpec((tm, D_MODEL), lambda i: (i, 0)),
        out_shape=jax.ShapeDtypeStruct((t, D_MODEL), F32),
        scratch_shapes=[pltpu.VMEM((tm, wg.shape[1]), BF16)],
        compiler_params=_params(("parallel",)),
        name="wo_ffn",
    )(attn, h, wo, g, wg, wu, wd)


def _conv_ffn_kernel(h_ref, gc_ref, win_ref, cw_ref, wout_ref, gf_ref, wg_ref, wu_ref, wd_ref,
                     gfin_ref, out_ref, ubuf_ref, hist_ref, act_ref, *, tiles_per_seq):
    tm = h_ref.shape[0]
    pad = SUBLANES

    @pl.when(pl.program_id(0) % tiles_per_seq == 0)
    def _():
        hist_ref[...] = jnp.zeros(hist_ref.shape, F32)

    hn = _rms(h_ref[...], gc_ref[...]).astype(BF16)
    for c in range(D_MODEL // CONV_CHUNK):
        cs = slice(c * CONV_CHUNK, (c + 1) * CONV_CHUNK)
        b_gate = _dot(hn, win_ref[:, c * CONV_CHUNK:(c + 1) * CONV_CHUNK])
        c_gate = _dot(hn, win_ref[:, D_MODEL + c * CONV_CHUNK:D_MODEL + (c + 1) * CONV_CHUNK])
        xp = _dot(hn, win_ref[:, 2 * D_MODEL + c * CONV_CHUNK:2 * D_MODEL + (c + 1) * CONV_CHUNK])
        u = c_gate * xp
        ubuf = ubuf_ref.at[c % 2]
        ubuf[0:pad, :] = hist_ref[:, cs]
        ubuf[pad:pad + tm, :] = u
        hist_ref[:, cs] = ubuf[tm:tm + pad, :]
        u_conv = (cw_ref[0:1, cs] * ubuf[pad - 2:pad - 2 + tm, :]
                  + cw_ref[1:2, cs] * ubuf[pad - 1:pad - 1 + tm, :]
                  + cw_ref[2:3, cs] * u)
        act_ref[:, cs] = (b_gate * u_conv).astype(BF16)

    out_ref[...] = h_ref[...] + _dot(act_ref[:, :D_MODEL], wout_ref[...])
    h2 = out_ref[...] + _swiglu(out_ref[...], gf_ref[1:2, :], wg_ref, wu_ref, wd_ref, act_ref)
    out_ref[...] = _rms(h2, gfin_ref[...])


def _conv_ffn(h, gc, win, cw, wout, gf, wg, wu, wd, gfin, seq):
    t = h.shape[0]
    tm = TOKEN_TILE
    return pl.pallas_call(
        functools.partial(_conv_ffn_kernel, tiles_per_seq=seq // tm),
        grid=(t // tm,),
        in_specs=[
            pl.BlockSpec((tm, D_MODEL), lambda i: (i, 0)),
            _const_spec(gc.shape), _const_spec(win.shape), _layer0_spec(cw),
            _const_spec(wout.shape), _const_spec(gf.shape), _const_spec(wg.shape),
            _const_spec(wu.shape), _const_spec(wd.shape), _const_spec(gfin.shape),
        ],
        out_specs=pl.BlockSpec((tm, D_MODEL), lambda i: (i, 0)),
        out_shape=jax.ShapeDtypeStruct((t, D_MODEL), F32),
        scratch_shapes=[
            pltpu.VMEM((2, tm + SUBLANES, CONV_CHUNK), F32),
            pltpu.VMEM((SUBLANES, D_MODEL), F32),
            pltpu.VMEM((tm, wg.shape[1]), BF16),
        ],
        compiler_params=_params(("arbitrary",)),
        name="conv_ffn",
    )(h, gc, win, cw, wout, gf, wg, wu, wd, gfin)


def _rope_inv_freq_lanes():
    inv = 1.0 / (ROPE_THETA ** (np.arange(0, QK_ROPE_DIM, 2, dtype=np.float32) / np.float32(QK_ROPE_DIM)))
    inv = inv.astype(np.float32)
    return jnp.asarray(np.tile(inv, LANES // (QK_ROPE_DIM // 2))[None, :])


def kernel(x, positions, mla_norm, mla_w_in, mla_g_cq, mla_g_ckv, mla_w_uq, mla_w_ukv, mla_w_o,
           conv_norm, conv_w_in, conv_w, conv_w_out, ffn_norm, ffn_w_gate, ffn_w_up, ffn_w_down,
           final_norm):
    bsz, seq, d = x.shape
    t = bsz * seq
    assert d == D_MODEL and seq % TOKEN_TILE == 0 and seq % (ATTN_CHAINS * ATTN_TILE) == 0 and ATTN_TILE % CHUNK == 0
    x2 = x.reshape(t, d)
    w_in = jnp.pad(mla_w_in, ((0, 0), (0, 0), (0, LANES - QK_ROPE_DIM)))
    qn, qr, kn, kr, v = _mla_pre(
        x2, positions.reshape(t, 1), mla_norm, w_in, mla_g_cq, mla_g_ckv, mla_w_uq, mla_w_ukv,
        _rope_inv_freq_lanes())
    later_weights = [(mla_w_o, 0), (ffn_w_gate, 0), (ffn_w_up, 0), (ffn_w_down, 0),
                     (conv_w_in, 0), (conv_w_out, 0), (ffn_w_gate, 1), (ffn_w_up, 1), (ffn_w_down, 1)]
    (attn, wo_b, wg0_b, wu0_b, wd0_b, cwin_b, cwout_b, wg1_b, wu1_b, wd1_b) = _attention(
        qn, qr, kn, kr, v, bsz, seq, later_weights)
    h = _wo_ffn(attn, x2, wo_b, ffn_norm, wg0_b, wu0_b, wd0_b)
    out = _conv_ffn(h, conv_norm, cwin_b, conv_w, cwout_b, ffn_norm,
                    wg1_b, wu1_b, wd1_b, final_norm[None, :], seq)
    return out.reshape(bsz, seq, d)
```

```python
import functools
import math

import jax
import jax.numpy as jnp
import numpy as np
from jax import lax
from jax.experimental import pallas as pl
from jax.experimental.pallas import tpu as pltpu

D_MODEL = 1024
CHUNK = 64
N_HEADS = 8
QK_NOPE_DIM = 128
QK_ROPE_DIM = 64
V_HEAD_DIM = 128
Q_LORA_RANK = 512
KV_LORA_RANK = 256
ROPE_THETA = 10000.0
CONV_WIDTH = 3
RMS_EPS = 1e-6

LANES = 128
SUBLANES = 8
BF16_SUBLANES = 16
MXU_COLS = 256
VMEM_LIMIT_BYTES = 56 * 1024 * 1024

TOKEN_TILE = 1024
ATTN_TILE = 256
ATTN_CHAINS = 2
FF_CHUNK = MXU_COLS
CONV_CHUNK = MXU_COLS

BF16 = jnp.bfloat16
F32 = jnp.float32


def _dot(a, b):
    return jnp.dot(a, b, preferred_element_type=F32)


def _rms(x, g):
    return x * lax.rsqrt(jnp.mean(x * x, axis=-1, keepdims=True) + RMS_EPS) * g


def _const_spec(shape):
    return pl.BlockSpec(shape, lambda *_: (0,) * len(shape), pipeline_mode=pl.Buffered(1))


def _layer0_spec(w):
    return pl.BlockSpec((None,) + w.shape[1:], lambda *_: (0,) * w.ndim, pipeline_mode=pl.Buffered(1))


def _params(semantics):
    return pltpu.CompilerParams(dimension_semantics=semantics, vmem_limit_bytes=VMEM_LIMIT_BYTES)


def _cast_specs(w, layer, steps, step_of):
    _, rows, cols = w.shape
    out_shape = jax.ShapeDtypeStruct((rows, cols), BF16)
    if rows % (steps * BF16_SUBLANES) == 0:
        rb = rows // steps
        return (pl.BlockSpec((None, rb, cols), lambda *g: (layer, step_of(*g), 0)),
                pl.BlockSpec((rb, cols), lambda *g: (step_of(*g), 0)), out_shape)
    per = steps // (cols // LANES)
    return (pl.BlockSpec((None, rows, LANES), lambda *g: (layer, 0, step_of(*g) // per)),
            pl.BlockSpec((rows, LANES), lambda *g: (0, step_of(*g) // per)), out_shape)


def _cast_side_work(cast_in_refs, cast_out_refs):
    for src, dst in zip(cast_in_refs, cast_out_refs):
        dst[...] = src[...].astype(BF16)


def _mla_pre_kernel(x_ref, pos_ref, g_ref, w_in_ref, gcq_ref, gckv_ref, w_uq_ref, w_ukv_ref, invf_ref,
                    qn_ref, qr_ref, kn_ref, kr_ref, v_ref, w_in_b, w_uq_b, w_ukv_b, *, scale):
    tm = x_ref.shape[0]
    kr_col = Q_LORA_RANK + KV_LORA_RANK

    @pl.when(pl.program_id(0) == 0)
    def _():
        w_in_b[:, :kr_col] = w_in_ref[:, :kr_col].astype(BF16)
        w_in_b[:, kr_col:] = jnp.zeros((D_MODEL, LANES), BF16)
        w_in_b[:, kr_col:kr_col + QK_ROPE_DIM] = w_in_ref[:, kr_col:].astype(BF16)
        pair = 2 * (QK_NOPE_DIM + QK_ROPE_DIM)
        dst = lax.broadcasted_iota(jnp.int32, (pair, pair), 1)
        src = jnp.where(dst < QK_NOPE_DIM, dst,
                        jnp.where(dst < 2 * QK_NOPE_DIM, dst + QK_ROPE_DIM,
                                  jnp.where(dst < 2 * QK_NOPE_DIM + QK_ROPE_DIM, dst - QK_NOPE_DIM, dst)))
        select = (lax.broadcasted_iota(jnp.int32, (pair, pair), 0) == src).astype(BF16)
        n_nope = N_HEADS * QK_NOPE_DIM
        for p in range(N_HEADS // 2):
            wp = _dot(w_uq_ref[:, p * pair:(p + 1) * pair].astype(BF16), select).astype(BF16)
            w_uq_b[:, 2 * p * QK_NOPE_DIM:2 * (p + 1) * QK_NOPE_DIM] = wp[:, :2 * QK_NOPE_DIM]
            w_uq_b[:, n_nope + p * LANES:n_nope + (p + 1) * LANES] = wp[:, 2 * QK_NOPE_DIM:]
        w_ukv_b[...] = w_ukv_ref[...].astype(BF16)

    xn = _rms(x_ref[...], g_ref[...]).astype(BF16)
    proj = _dot(xn, w_in_b[...])
    cq = _rms(proj[:, :Q_LORA_RANK], gcq_ref[...]).astype(BF16)
    ckv = _rms(proj[:, Q_LORA_RANK:kr_col], gckv_ref[...]).astype(BF16)
    kr = proj[:, kr_col:]

    n_freq = QK_ROPE_DIM // 2
    n_grp = LANES // n_freq
    blk = tm // n_grp
    grp_c = lax.broadcasted_iota(jnp.int32, (blk, LANES), 1) // n_freq
    pos = pos_ref[...].astype(F32)
    pos_c = pos[(n_grp - 1) * blk:, :]
    for a in reversed(range(n_grp - 1)):
        pos_c = jnp.where(grp_c == a, pos[a * blk:(a + 1) * blk, :], pos_c)
    ang_c = pos_c * invf_ref[...]
    cos_c = jnp.cos(ang_c)
    sin_c = jnp.sin(ang_c)
    cos_r = [cos_c] + [pltpu.roll(cos_c, n_freq * k, 1) for k in range(1, n_grp)]
    sin_r = [sin_c] + [pltpu.roll(sin_c, n_freq * k, 1) for k in range(1, n_grp)]
    nsin_r = [-t for t in sin_r]

    def expand(a, even_tabs, odd_tabs):
        out = None
        for g in reversed(range(n_grp)):
            tab = (even_tabs if g % 2 == 0 else odd_tabs)[(g - a) % n_grp]
            out = tab if out is None else jnp.where(grp_c == g, tab, out)
        return out

    cos = jnp.concatenate([expand(a, cos_r, cos_r) for a in range(n_grp)], axis=0)
    sin_signed = jnp.concatenate([expand(a, nsin_r, sin_r) for a in range(n_grp)], axis=0)
    lane = lax.broadcasted_iota(jnp.int32, (tm, LANES), 1)
    first_half = (lane % QK_ROPE_DIM) < n_freq

    def rope(t):
        partner = jnp.where(first_half,
                            pltpu.roll(t, LANES - n_freq, 1),
                            pltpu.roll(t, n_freq, 1))
        return t * cos + partner * sin_signed

    kr_rot = rope(kr)
    kr_ref[0] = kr_rot.astype(BF16)
    kr_ref[1] = pltpu.roll(kr_rot, QK_ROPE_DIM, 1).astype(BF16)

    q = _dot(cq, w_uq_b[...])
    for h in range(N_HEADS):
        qn_ref[h] = (q[:, h * LANES:(h + 1) * LANES] * scale).astype(BF16)
    for p in range(N_HEADS // 2):
        pair_cols = slice((N_HEADS + p) * LANES, (N_HEADS + p + 1) * LANES)
        qr_ref[p] = (rope(q[:, pair_cols]) * scale).astype(BF16)

    kv = _dot(ckv, w_ukv_b[...])
    for h in range(N_HEADS):
        kn_ref[h] = kv[:, 2 * h * LANES:(2 * h + 1) * LANES].astype(BF16)
        v_ref[h] = kv[:, (2 * h + 1) * LANES:(2 * h + 2) * LANES].astype(BF16)


def _mla_pre(x2, pos, g, w_in, gcq, gckv, w_uq, w_ukv, invf):
    t = x2.shape[0]
    tm = TOKEN_TILE
    scale = math.log2(math.e) / math.sqrt(QK_NOPE_DIM + QK_ROPE_DIM)

    def stacked(n):
        return (pl.BlockSpec((n, tm, LANES), lambda i: (0, i, 0)), jax.ShapeDtypeStruct((n, t, LANES), BF16))

    specs, shapes = zip(stacked(N_HEADS), stacked(N_HEADS // 2), stacked(N_HEADS), stacked(2), stacked(N_HEADS))
    return pl.pallas_call(
        functools.partial(_mla_pre_kernel, scale=scale),
        grid=(t // tm,),
        in_specs=[
            pl.BlockSpec((tm, D_MODEL), lambda i: (i, 0)),
            pl.BlockSpec((tm, 1), lambda i: (i, 0)),
            _const_spec(g.shape), _layer0_spec(w_in), _const_spec(gcq.shape), _const_spec(gckv.shape),
            _layer0_spec(w_uq), _layer0_spec(w_ukv), _const_spec(invf.shape),
        ],
        out_specs=list(specs),
        out_shape=list(shapes),
        scratch_shapes=[
            pltpu.VMEM((D_MODEL, Q_LORA_RANK + KV_LORA_RANK + LANES), BF16),
            pltpu.VMEM(w_uq.shape[1:], BF16),
            pltpu.VMEM(w_ukv.shape[1:], BF16),
        ],
        compiler_params=_params(("arbitrary",)),
        name="mla_pre",
    )(x2, pos, g, w_in, gcq, gckv, w_uq, w_ukv, invf)


def _attn_kernel(*refs, n_cast):
    qn_ref, qr_ref, kn_ref, kr_ref, v_ref = refs[:5]
    o_ref = refs[5 + n_cast]
    m_ref, acc_ref = refs[-2:]
    _cast_side_work(refs[5:5 + n_cast], refs[6 + n_cast:-2])

    seq = qn_ref.shape[0]
    ta = ATTN_TILE
    groups = ta // LANES
    row_chunk = lax.broadcasted_iota(jnp.int32, (ta, ta), 0) // CHUNK
    col_chunk = lax.broadcasted_iota(jnp.int32, (ta, ta), 1) // CHUNK
    diag_mask = col_chunk <= row_chunk
    ones = jnp.ones((ta, LANES), BF16)

    def tile_slice(i):
        return pl.ds(i * ta, ta)

    def kv_tile(j):
        ks = tile_slice(j)
        k = jnp.concatenate([kn_ref[ks, :], kr_ref[ks, :]], axis=1)
        v_ext = jnp.concatenate([v_ref[ks, :], ones], axis=1)
        return k, v_ext

    def step(c, qs, k, v_ext, masked):
        q = jnp.concatenate([qn_ref[qs, :], qr_ref[qs, :]], axis=1)
        s = lax.dot_general(q, k, (((1,), (1,)), ((), ())), preferred_element_type=F32)
        if masked:
            s = jnp.where(diag_mask, s, -1e30)
        parts = [s[:, g * LANES:(g + 1) * LANES] for g in range(groups)]
        m_prev = m_ref[c]
        m_new = jnp.maximum(m_prev, jnp.max(functools.reduce(jnp.maximum, parts), axis=1, keepdims=True))
        alpha = jnp.exp2(m_prev - m_new)
        p = jnp.concatenate([jnp.exp2(part - m_new).astype(BF16) for part in parts], axis=1)
        acc_ref[c] = acc_ref[c] * jnp.concatenate([alpha, alpha], axis=1) + _dot(p, v_ext)
        m_ref[c] = m_new

    def finish(c, qs):
        acc = acc_ref[c]
        o_ref[qs, :] = (acc[:, :V_HEAD_DIM] / acc[:, V_HEAD_DIM:]).astype(o_ref.dtype)

    for group in range(seq // (ATTN_CHAINS * ta)):
        first = group * ATTN_CHAINS
        m_ref[...] = jnp.full(m_ref.shape, -1e30, F32)
        acc_ref[...] = jnp.zeros(acc_ref.shape, F32)
        for j in range(first + ATTN_CHAINS):
            k, v_ext = kv_tile(j)
            for c in range(max(0, j - first), ATTN_CHAINS):
                step(c, tile_slice(first + c), k, v_ext, masked=(j == first + c))
            if j >= first:
                finish(j - first, tile_slice(j))


def _attention(qn, qr, kn, kr, v, bsz, seq, casts):
    t = bsz * seq
    ta = ATTN_TILE
    head_spec = pl.BlockSpec((None, seq, LANES), lambda b, h: (h, b, 0))
    cast_specs = [_cast_specs(w, layer, bsz * N_HEADS, lambda b, h: b * N_HEADS + h) for w, layer in casts]
    return pl.pallas_call(
        functools.partial(_attn_kernel, n_cast=len(casts)),
        grid=(bsz, N_HEADS),
        in_specs=[head_spec,
                  pl.BlockSpec((None, seq, LANES), lambda b, h: (h // 2, b, 0)),
                  head_spec,
                  pl.BlockSpec((None, seq, LANES), lambda b, h: (h % 2, b, 0)),
                  head_spec] + [c[0] for c in cast_specs],
        out_specs=[head_spec] + [c[1] for c in cast_specs],
        out_shape=[jax.ShapeDtypeStruct((N_HEADS, t, LANES), BF16)] + [c[2] for c in cast_specs],
        scratch_shapes=[pltpu.VMEM((ATTN_CHAINS, ta, LANES), F32),
                        pltpu.VMEM((ATTN_CHAINS, ta, 2 * LANES), F32)],
        compiler_params=_params(("arbitrary", "arbitrary")),
        name="attention",
    )(qn, qr, kn, kr, v, *[w for w, _ in casts])


def _swiglu(h1, g, wg_ref, wu_ref, wd_ref, act_ref):
    d_ff = wg_ref.shape[1]
    hn = _rms(h1, g).astype(BF16)
    for c in range(d_ff // FF_CHUNK):
        cs = slice(c * FF_CHUNK, (c + 1) * FF_CHUNK)
        gate = _dot(hn, wg_ref[:, cs])
        up = _dot(hn, wu_ref[:, cs])
        act_ref[:, cs] = (jax.nn.silu(gate) * up).astype(BF16)
    return _dot(act_ref[...], wd_ref[...])


def _wo_ffn_kernel(a_ref, h_ref, wo_ref, g_ref, wg_ref, wu_ref, wd_ref, out_ref, act_ref):
    a = jnp.concatenate([a_ref[h] for h in range(N_HEADS)], axis=1)
    h1 = h_ref[...] + _dot(a, wo_ref[...])
    out_ref[...] = h1 + _swiglu(h1, g_ref[0:1, :], wg_ref, wu_ref, wd_ref, act_ref)


def _wo_ffn(attn, h, wo, g, wg, wu, wd):
    t = h.shape[0]
    tm = TOKEN_TILE
    return pl.pallas_call(
        _wo_ffn_kernel,
        grid=(t // tm,),
        in_specs=[
            pl.BlockSpec((N_HEADS, tm, LANES), lambda i: (0, i, 0)),
            pl.BlockSpec((tm, D_MODEL), lambda i: (i, 0)),
            _const_spec(wo.shape), _const_spec(g.shape), _const_spec(wg.shape),
            _const_spec(wu.shape), _const_spec(wd.shape),
        ],
        out_specs=pl.BlockSpec((tm, D_MODEL), lambda i: (i, 0)),
        out_shape=jax.ShapeDtypeStruct((t, D_MODEL), F32),
        scratch_shapes=[pltpu.VMEM((tm, wg.shape[1]), BF16)],
        compiler_params=_params(("parallel",)),
        name="wo_ffn",
    )(attn, h, wo, g, wg, wu, wd)


def _conv_ffn_kernel(h_ref, gc_ref, win_ref, cw_ref, wout_ref, gf_ref, wg_ref, wu_ref, wd_ref,
                     gfin_ref, out_ref, ubuf_ref, hist_ref, act_ref, *, tiles_per_seq):
    tm = h_ref.shape[0]
    pad = SUBLANES

    @pl.when(pl.program_id(0) % tiles_per_seq == 0)
    def _():
        hist_ref[...] = jnp.zeros(hist_ref.shape, F32)

    hn = _rms(h_ref[...], gc_ref[...]).astype(BF16)
    for c in range(D_MODEL // CONV_CHUNK):
        cs = slice(c * CONV_CHUNK, (c + 1) * CONV_CHUNK)
        b_gate = _dot(hn, win_ref[:, c * CONV_CHUNK:(c + 1) * CONV_CHUNK])
        c_gate = _dot(hn, win_ref[:, D_MODEL + c * CONV_CHUNK:D_MODEL + (c + 1) * CONV_CHUNK])
        xp = _dot(hn, win_ref[:, 2 * D_MODEL + c * CONV_CHUNK:2 * D_MODEL + (c + 1) * CONV_CHUNK])
        u = c_gate * xp
        ubuf = ubuf_ref.at[c % 2]
        ubuf[0:pad, :] = hist_ref[:, cs]
        ubuf[pad:pad + tm, :] = u
        hist_ref[:, cs] = ubuf[tm:tm + pad, :]
        u_conv = (cw_ref[0:1, cs] * ubuf[pad - 2:pad - 2 + tm, :]
                  + cw_ref[1:2, cs] * ubuf[pad - 1:pad - 1 + tm, :]
                  + cw_ref[2:3, cs] * u)
        act_ref[:, cs] = (b_gate * u_conv).astype(BF16)

    out_ref[...] = h_ref[...] + _dot(act_ref[:, :D_MODEL], wout_ref[...])
    h2 = out_ref[...] + _swiglu(out_ref[...], gf_ref[1:2, :], wg_ref, wu_ref, wd_ref, act_ref)
    out_ref[...] = _rms(h2, gfin_ref[...])


def _conv_ffn(h, gc, win, cw, wout, gf, wg, wu, wd, gfin, seq):
    t = h.shape[0]
    tm = TOKEN_TILE
    return pl.pallas_call(
        functools.partial(_conv_ffn_kernel, tiles_per_seq=seq // tm),
        grid=(t // tm,),
        in_specs=[
            pl.BlockSpec((tm, D_MODEL), lambda i: (i, 0)),
            _const_spec(gc.shape), _const_spec(win.shape), _layer0_spec(cw),
            _const_spec(wout.shape), _const_spec(gf.shape), _const_spec(wg.shape),
            _const_spec(wu.shape), _const_spec(wd.shape), _const_spec(gfin.shape),
        ],
        out_specs=pl.BlockSpec((tm, D_MODEL), lambda i: (i, 0)),
        out_shape=jax.ShapeDtypeStruct((t, D_MODEL), F32),
        scratch_shapes=[
            pltpu.VMEM((2, tm + SUBLANES, CONV_CHUNK), F32),
            pltpu.VMEM((SUBLANES, D_MODEL), F32),
            pltpu.VMEM((tm, wg.shape[1]), BF16),
        ],
        compiler_params=_params(("arbitrary",)),
        name="conv_ffn",
    )(h, gc, win, cw, wout, gf, wg, wu, wd, gfin)


def _rope_inv_freq_lanes():
    inv = 1.0 / (ROPE_THETA ** (np.arange(0, QK_ROPE_DIM, 2, dtype=np.float32) / np.float32(QK_ROPE_DIM)))
    inv = inv.astype(np.float32)
    return jnp.asarray(np.tile(inv, LANES // (QK_ROPE_DIM // 2))[None, :])


def kernel(x, positions, mla_norm, mla_w_in, mla_g_cq, mla_g_ckv, mla_w_uq, mla_w_ukv, mla_w_o,
           conv_norm, conv_w_in, conv_w, conv_w_out, ffn_norm, ffn_w_gate, ffn_w_up, ffn_w_down,
           final_norm):
    bsz, seq, d = x.shape
    t = bsz * seq
    assert d == D_MODEL and seq % TOKEN_TILE == 0 and seq % (ATTN_CHAINS * ATTN_TILE) == 0 and ATTN_TILE % CHUNK == 0
    x2 = x.reshape(t, d)
    qn, qr, kn, kr, v = _mla_pre(
        x2, positions.reshape(t, 1), mla_norm, mla_w_in, mla_g_cq, mla_g_ckv, mla_w_uq, mla_w_ukv,
        _rope_inv_freq_lanes())
    later_weights = [(mla_w_o, 0), (ffn_w_gate, 0), (ffn_w_up, 0), (ffn_w_down, 0),
                     (conv_w_in, 0), (conv_w_out, 0), (ffn_w_gate, 1), (ffn_w_up, 1), (ffn_w_down, 1)]
    (attn, wo_b, wg0_b, wu0_b, wd0_b, cwin_b, cwout_b, wg1_b, wu1_b, wd1_b) = _attention(
        qn, qr, kn, kr, v, bsz, seq, later_weights)
    h = _wo_ffn(attn, x2, wo_b, ffn_norm, wg0_b, wu0_b, wd0_b)
    out = _conv_ffn(h, conv_norm, cwin_b, conv_w, cwout_b, ffn_norm,
                    wg1_b, wu1_b, wd1_b, final_norm[None, :], seq)
    return out.reshape(bsz, seq, d)
```

```python
import functools
import math

import jax
import jax.numpy as jnp
import numpy as np
from jax import lax
from jax.experimental import pallas as pl
from jax.experimental.pallas import tpu as pltpu

D_MODEL = 1024
CHUNK = 64
N_HEADS = 8
QK_NOPE_DIM = 128
QK_ROPE_DIM = 64
V_HEAD_DIM = 128
Q_LORA_RANK = 512
KV_LORA_RANK = 256
ROPE_THETA = 10000.0
CONV_WIDTH = 3
RMS_EPS = 1e-6

LANES = 128
SUBLANES = 8
BF16_SUBLANES = 16
MXU_COLS = 256
VMEM_LIMIT_BYTES = 56 * 1024 * 1024

TOKEN_TILE = 1024
ATTN_TILE = 256
ATTN_CHAINS = 2
FF_CHUNK = MXU_COLS
CONV_CHUNK = MXU_COLS

BF16 = jnp.bfloat16
F32 = jnp.float32


def _dot(a, b):
    return jnp.dot(a, b, preferred_element_type=F32)


def _rms(x, g):
    return x * lax.rsqrt(jnp.mean(x * x, axis=-1, keepdims=True) + RMS_EPS) * g


def _const_spec(shape):
    return pl.BlockSpec(shape, lambda *_: (0,) * len(shape), pipeline_mode=pl.Buffered(1))


def _layer0_spec(w):
    return pl.BlockSpec((None,) + w.shape[1:], lambda *_: (0,) * w.ndim, pipeline_mode=pl.Buffered(1))


def _params(semantics):
    return pltpu.CompilerParams(dimension_semantics=semantics, vmem_limit_bytes=VMEM_LIMIT_BYTES)


def _cast_specs(w, layer, steps, step_of):
    _, rows, cols = w.shape
    out_shape = jax.ShapeDtypeStruct((rows, cols), BF16)
    if rows % (steps * BF16_SUBLANES) == 0:
        rb = rows // steps
        return (pl.BlockSpec((None, rb, cols), lambda *g: (layer, step_of(*g), 0)),
                pl.BlockSpec((rb, cols), lambda *g: (step_of(*g), 0)), out_shape, 1)
    per = steps // (cols // LANES)
    return (pl.BlockSpec((None, rows, LANES), lambda *g: (layer, 0, step_of(*g) // per)),
            pl.BlockSpec((rows, LANES), lambda *g: (0, step_of(*g) // per)), out_shape, per)


def _cast_side_work(step, cast_in_refs, cast_out_refs, periods):
    for src, dst, period in zip(cast_in_refs, cast_out_refs, periods):
        if period == 1:
            dst[...] = src[...].astype(BF16)
        else:
            @pl.when(step % period == 0)
            def _(src=src, dst=dst):
                dst[...] = src[...].astype(BF16)


def _mla_pre_kernel(x_ref, pos_ref, g_ref, w_in_ref, gcq_ref, gckv_ref, w_uq_ref, w_ukv_ref, invf_ref,
                    qn_ref, qr_ref, kn_ref, kr_ref, v_ref, w_in_b, w_uq_b, w_ukv_b, *, scale):
    tm = x_ref.shape[0]
    kr_col = Q_LORA_RANK + KV_LORA_RANK

    @pl.when(pl.program_id(0) == 0)
    def _():
        w_in_b[:, :kr_col] = w_in_ref[:, :kr_col].astype(BF16)
        w_in_b[:, kr_col:] = jnp.zeros((D_MODEL, LANES), BF16)
        w_in_b[:, kr_col:kr_col + QK_ROPE_DIM] = w_in_ref[:, kr_col:].astype(BF16)
        pair = 2 * (QK_NOPE_DIM + QK_ROPE_DIM)
        dst = lax.broadcasted_iota(jnp.int32, (pair, pair), 1)
        src = jnp.where(dst < QK_NOPE_DIM, dst,
                        jnp.where(dst < 2 * QK_NOPE_DIM, dst + QK_ROPE_DIM,
                                  jnp.where(dst < 2 * QK_NOPE_DIM + QK_ROPE_DIM, dst - QK_NOPE_DIM, dst)))
        select = (lax.broadcasted_iota(jnp.int32, (pair, pair), 0) == src).astype(BF16)
        n_nope = N_HEADS * QK_NOPE_DIM
        for p in range(N_HEADS // 2):
            wp = _dot(w_uq_ref[:, p * pair:(p + 1) * pair].astype(BF16), select).astype(BF16)
            w_uq_b[:, 2 * p * QK_NOPE_DIM:2 * (p + 1) * QK_NOPE_DIM] = wp[:, :2 * QK_NOPE_DIM]
            w_uq_b[:, n_nope + p * LANES:n_nope + (p + 1) * LANES] = wp[:, 2 * QK_NOPE_DIM:]
        w_ukv_b[...] = w_ukv_ref[...].astype(BF16)

    xn = _rms(x_ref[...], g_ref[...]).astype(BF16)
    proj = _dot(xn, w_in_b[...])
    cq = _rms(proj[:, :Q_LORA_RANK], gcq_ref[...]).astype(BF16)
    ckv = _rms(proj[:, Q_LORA_RANK:kr_col], gckv_ref[...]).astype(BF16)
    kr = proj[:, kr_col:]

    n_freq = QK_ROPE_DIM // 2
    n_grp = LANES // n_freq
    blk = tm // n_grp
    grp_c = lax.broadcasted_iota(jnp.int32, (blk, LANES), 1) // n_freq
    pos = pos_ref[...].astype(F32)
    pos_c = pos[(n_grp - 1) * blk:, :]
    for a in reversed(range(n_grp - 1)):
        pos_c = jnp.where(grp_c == a, pos[a * blk:(a + 1) * blk, :], pos_c)
    ang_c = pos_c * invf_ref[...]
    cos_c = jnp.cos(ang_c)
    sin_c = jnp.sin(ang_c)
    cos_r = [cos_c] + [pltpu.roll(cos_c, n_freq * k, 1) for k in range(1, n_grp)]
    sin_r = [sin_c] + [pltpu.roll(sin_c, n_freq * k, 1) for k in range(1, n_grp)]
    nsin_r = [-t for t in sin_r]

    def expand(a, even_tabs, odd_tabs):
        out = None
        for g in reversed(range(n_grp)):
            tab = (even_tabs if g % 2 == 0 else odd_tabs)[(g - a) % n_grp]
            out = tab if out is None else jnp.where(grp_c == g, tab, out)
        return out

    cos = jnp.concatenate([expand(a, cos_r, cos_r) for a in range(n_grp)], axis=0)
    sin_signed = jnp.concatenate([expand(a, nsin_r, sin_r) for a in range(n_grp)], axis=0)
    lane = lax.broadcasted_iota(jnp.int32, (tm, LANES), 1)
    first_half = (lane % QK_ROPE_DIM) < n_freq

    def rope(t):
        partner = jnp.where(first_half,
                            pltpu.roll(t, LANES - n_freq, 1),
                            pltpu.roll(t, n_freq, 1))
        return t * cos + partner * sin_signed

    kr_rot = rope(kr)
    kr_ref[0] = kr_rot.astype(BF16)
    kr_ref[1] = pltpu.roll(kr_rot, QK_ROPE_DIM, 1).astype(BF16)

    q = _dot(cq, w_uq_b[...])
    for h in range(N_HEADS):
        qn_ref[h] = (q[:, h * LANES:(h + 1) * LANES] * scale).astype(BF16)
    for p in range(N_HEADS // 2):
        pair_cols = slice((N_HEADS + p) * LANES, (N_HEADS + p + 1) * LANES)
        qr_ref[p] = (rope(q[:, pair_cols]) * scale).astype(BF16)

    kv = _dot(ckv, w_ukv_b[...])
    for h in range(N_HEADS):
        kn_ref[h] = kv[:, 2 * h * LANES:(2 * h + 1) * LANES].astype(BF16)
        v_ref[h] = kv[:, (2 * h + 1) * LANES:(2 * h + 2) * LANES].astype(BF16)


def _mla_pre(x2, pos, g, w_in, gcq, gckv, w_uq, w_ukv, invf):
    t = x2.shape[0]
    tm = TOKEN_TILE
    scale = math.log2(math.e) / math.sqrt(QK_NOPE_DIM + QK_ROPE_DIM)

    def stacked(n):
        return (pl.BlockSpec((n, tm, LANES), lambda i: (0, i, 0)), jax.ShapeDtypeStruct((n, t, LANES), BF16))

    specs, shapes = zip(stacked(N_HEADS), stacked(N_HEADS // 2), stacked(N_HEADS), stacked(2), stacked(N_HEADS))
    return pl.pallas_call(
        functools.partial(_mla_pre_kernel, scale=scale),
        grid=(t // tm,),
        in_specs=[
            pl.BlockSpec((tm, D_MODEL), lambda i: (i, 0)),
            pl.BlockSpec((tm, 1), lambda i: (i, 0)),
            _const_spec(g.shape), _layer0_spec(w_in), _const_spec(gcq.shape), _const_spec(gckv.shape),
            _layer0_spec(w_uq), _layer0_spec(w_ukv), _const_spec(invf.shape),
        ],
        out_specs=list(specs),
        out_shape=list(shapes),
        scratch_shapes=[
            pltpu.VMEM((D_MODEL, Q_LORA_RANK + KV_LORA_RANK + LANES), BF16),
            pltpu.VMEM(w_uq.shape[1:], BF16),
            pltpu.VMEM(w_ukv.shape[1:], BF16),
        ],
        compiler_params=_params(("arbitrary",)),
        name="mla_pre",
    )(x2, pos, g, w_in, gcq, gckv, w_uq, w_ukv, invf)


def _attn_kernel(*refs, cast_periods):
    n_cast = len(cast_periods)
    qn_ref, qr_ref, kn_ref, kr_ref, v_ref = refs[:5]
    o_ref = refs[5 + n_cast]
    m_ref, acc_ref = refs[-2:]
    step_id = pl.program_id(0) * pl.num_programs(1) + pl.program_id(1)
    _cast_side_work(step_id, refs[5:5 + n_cast], refs[6 + n_cast:-2], cast_periods)

    seq = qn_ref.shape[0]
    ta = ATTN_TILE
    groups = ta // LANES
    row_chunk = lax.broadcasted_iota(jnp.int32, (ta, ta), 0) // CHUNK
    col_chunk = lax.broadcasted_iota(jnp.int32, (ta, ta), 1) // CHUNK
    diag_mask = col_chunk <= row_chunk
    ones = jnp.ones((ta, LANES), BF16)

    def tile_slice(i):
        return pl.ds(i * ta, ta)

    def kv_tile(j):
        ks = tile_slice(j)
        k = jnp.concatenate([kn_ref[ks, :], kr_ref[ks, :]], axis=1)
        v_ext = jnp.concatenate([v_ref[ks, :], ones], axis=1)
        return k, v_ext

    def step(c, qs, k, v_ext, masked):
        q = jnp.concatenate([qn_ref[qs, :], qr_ref[qs, :]], axis=1)
        s = lax.dot_general(q, k, (((1,), (1,)), ((), ())), preferred_element_type=F32)
        if masked:
            s = jnp.where(diag_mask, s, -1e30)
        parts = [s[:, g * LANES:(g + 1) * LANES] for g in range(groups)]
        m_prev = m_ref[c]
        m_new = jnp.maximum(m_prev, jnp.max(functools.reduce(jnp.maximum, parts), axis=1, keepdims=True))
        alpha = jnp.exp2(m_prev - m_new)
        p = jnp.concatenate([jnp.exp2(part - m_new).astype(BF16) for part in parts], axis=1)
        acc_ref[c] = acc_ref[c] * jnp.concatenate([alpha, alpha], axis=1) + _dot(p, v_ext)
        m_ref[c] = m_new

    def finish(c, qs):
        acc = acc_ref[c]
        o_ref[qs, :] = (acc[:, :V_HEAD_DIM] / acc[:, V_HEAD_DIM:]).astype(o_ref.dtype)

    for group in range(seq // (ATTN_CHAINS * ta)):
        first = group * ATTN_CHAINS
        m_ref[...] = jnp.full(m_ref.shape, -1e30, F32)
        acc_ref[...] = jnp.zeros(acc_ref.shape, F32)
        for j in range(first + ATTN_CHAINS):
            k, v_ext = kv_tile(j)
            for c in range(max(0, j - first), ATTN_CHAINS):
                step(c, tile_slice(first + c), k, v_ext, masked=(j == first + c))
            if j >= first:
                finish(j - first, tile_slice(j))


def _attention(qn, qr, kn, kr, v, bsz, seq, casts):
    t = bsz * seq
    ta = ATTN_TILE
    head_spec = pl.BlockSpec((None, seq, LANES), lambda b, h: (h, b, 0))
    cast_specs = [_cast_specs(w, layer, bsz * N_HEADS, lambda b, h: b * N_HEADS + h) for w, layer in casts]
    return pl.pallas_call(
        functools.partial(_attn_kernel, cast_periods=tuple(c[3] for c in cast_specs)),
        grid=(bsz, N_HEADS),
        in_specs=[head_spec,
                  pl.BlockSpec((None, seq, LANES), lambda b, h: (h // 2, b, 0)),
                  head_spec,
                  pl.BlockSpec((None, seq, LANES), lambda b, h: (h % 2, b, 0)),
                  head_spec] + [c[0] for c in cast_specs],
        out_specs=[head_spec] + [c[1] for c in cast_specs],
        out_shape=[jax.ShapeDtypeStruct((N_HEADS, t, LANES), BF16)] + [c[2] for c in cast_specs],
        scratch_shapes=[pltpu.VMEM((ATTN_CHAINS, ta, LANES), F32),
                        pltpu.VMEM((ATTN_CHAINS, ta, 2 * LANES), F32)],
        compiler_params=_params(("arbitrary", "arbitrary")),
        name="attention",
    )(qn, qr, kn, kr, v, *[w for w, _ in casts])


def _swiglu(h1, g, wg_ref, wu_ref, wd_ref, act_ref):
    d_ff = wg_ref.shape[1]
    hn = _rms(h1, g).astype(BF16)
    for c in range(d_ff // FF_CHUNK):
        cs = slice(c * FF_CHUNK, (c + 1) * FF_CHUNK)
        gate = _dot(hn, wg_ref[:, cs])
        up = _dot(hn, wu_ref[:, cs])
        act_ref[:, cs] = (jax.nn.silu(gate) * up).astype(BF16)
    return _dot(act_ref[...], wd_ref[...])


def _wo_ffn_kernel(a_ref, h_ref, wo_ref, g_ref, wg_ref, wu_ref, wd_ref, out_ref, act_ref):
    a = jnp.concatenate([a_ref[h] for h in range(N_HEADS)], axis=1)
    h1 = h_ref[...] + _dot(a, wo_ref[...])
    out_ref[...] = h1 + _swiglu(h1, g_ref[0:1, :], wg_ref, wu_ref, wd_ref, act_ref)


def _wo_ffn(attn, h, wo, g, wg, wu, wd):
    t = h.shape[0]
    tm = TOKEN_TILE
    return pl.pallas_call(
        _wo_ffn_kernel,
        grid=(t // tm,),
        in_specs=[
            pl.BlockSpec((N_HEADS, tm, LANES), lambda i: (0, i, 0)),
            pl.BlockSpec((tm, D_MODEL), lambda i: (i, 0)),
            _const_spec(wo.shape), _const_spec(g.shape), _const_spec(wg.shape),
            _const_spec(wu.shape), _const_spec(wd.shape),
        ],
        out_specs=pl.BlockSpec((tm, D_MODEL), lambda i: (i, 0)),
        out_shape=jax.ShapeDtypeStruct((t, D_MODEL), F32),
        scratch_shapes=[pltpu.VMEM((tm, wg.shape[1]), BF16)],
        compiler_params=_params(("parallel",)),
        name="wo_ffn",
    )(attn, h, wo, g, wg, wu, wd)


def _conv_ffn_kernel(h_ref, gc_ref, win_ref, cw_ref, wout_ref, gf_ref, wg_ref, wu_ref, wd_ref,
                     gfin_ref, out_ref, ubuf_ref, hist_ref, act_ref, *, tiles_per_seq):
    tm = h_ref.shape[0]
    pad = SUBLANES

    @pl.when(pl.program_id(0) % tiles_per_seq == 0)
    def _():
        hist_ref[...] = jnp.zeros(hist_ref.shape, F32)

    hn = _rms(h_ref[...], gc_ref[...]).astype(BF16)
    for c in range(D_MODEL // CONV_CHUNK):
        cs = slice(c * CONV_CHUNK, (c + 1) * CONV_CHUNK)
        b_gate = _dot(hn, win_ref[:, c * CONV_CHUNK:(c + 1) * CONV_CHUNK])
        c_gate = _dot(hn, win_ref[:, D_MODEL + c * CONV_CHUNK:D_MODEL + (c + 1) * CONV_CHUNK])
        xp = _dot(hn, win_ref[:, 2 * D_MODEL + c * CONV_CHUNK:2 * D_MODEL + (c + 1) * CONV_CHUNK])
        u = c_gate * xp
        ubuf = ubuf_ref.at[c % 2]
        ubuf[0:pad, :] = hist_ref[:, cs]
        ubuf[pad:pad + tm, :] = u
        hist_ref[:, cs] = ubuf[tm:tm + pad, :]
        u_conv = (cw_ref[0:1, cs] * ubuf[pad - 2:pad - 2 + tm, :]
                  + cw_ref[1:2, cs] * ubuf[pad - 1:pad - 1 + tm, :]
                  + cw_ref[2:3, cs] * u)
        act_ref[:, cs] = (b_gate * u_conv).astype(BF16)

    out_ref[...] = h_ref[...] + _dot(act_ref[:, :D_MODEL], wout_ref[...])
    h2 = out_ref[...] + _swiglu(out_ref[...], gf_ref[1:2, :], wg_ref, wu_ref, wd_ref, act_ref)
    out_ref[...] = _rms(h2, gfin_ref[...])


def _conv_ffn(h, gc, win, cw, wout, gf, wg, wu, wd, gfin, seq):
    t = h.shape[0]
    tm = TOKEN_TILE
    return pl.pallas_call(
        functools.partial(_conv_ffn_kernel, tiles_per_seq=seq // tm),
        grid=(t // tm,),
        in_specs=[
            pl.BlockSpec((tm, D_MODEL), lambda i: (i, 0)),
            _const_spec(gc.shape), _const_spec(win.shape), _layer0_spec(cw),
            _const_spec(wout.shape), _const_spec(gf.shape), _const_spec(wg.shape),
            _const_spec(wu.shape), _const_spec(wd.shape), _const_spec(gfin.shape),
        ],
        out_specs=pl.BlockSpec((tm, D_MODEL), lambda i: (i, 0)),
        out_shape=jax.ShapeDtypeStruct((t, D_MODEL), F32),
        scratch_shapes=[
            pltpu.VMEM((2, tm + SUBLANES, CONV_CHUNK), F32),
            pltpu.VMEM((SUBLANES, D_MODEL), F32),
            pltpu.VMEM((tm, wg.shape[1]), BF16),
        ],
        compiler_params=_params(("arbitrary",)),
        name="conv_ffn",
    )(h, gc, win, cw, wout, gf, wg, wu, wd, gfin)


def _rope_inv_freq_lanes():
    inv = 1.0 / (ROPE_THETA ** (np.arange(0, QK_ROPE_DIM, 2, dtype=np.float32) / np.float32(QK_ROPE_DIM)))
    inv = inv.astype(np.float32)
    return jnp.asarray(np.tile(inv, LANES // (QK_ROPE_DIM // 2))[None, :])


def kernel(x, positions, mla_norm, mla_w_in, mla_g_cq, mla_g_ckv, mla_w_uq, mla_w_ukv, mla_w_o,
           conv_norm, conv_w_in, conv_w, conv_w_out, ffn_norm, ffn_w_gate, ffn_w_up, ffn_w_down,
           final_norm):
    bsz, seq, d = x.shape
    t = bsz * seq
    assert d == D_MODEL and seq % TOKEN_TILE == 0 and seq % (ATTN_CHAINS * ATTN_TILE) == 0 and ATTN_TILE % CHUNK == 0
    x2 = x.reshape(t, d)
    qn, qr, kn, kr, v = _mla_pre(
        x2, positions.reshape(t, 1), mla_norm, mla_w_in, mla_g_cq, mla_g_ckv, mla_w_uq, mla_w_ukv,
        _rope_inv_freq_lanes())
    later_weights = [(mla_w_o, 0), (ffn_w_gate, 0), (ffn_w_up, 0), (ffn_w_down, 0),
                     (conv_w_in, 0), (conv_w_out, 0), (ffn_w_gate, 1), (ffn_w_up, 1), (ffn_w_down, 1)]
    (attn, wo_b, wg0_b, wu0_b, wd0_b, cwin_b, cwout_b, wg1_b, wu1_b, wd1_b) = _attention(
        qn, qr, kn, kr, v, bsz, seq, later_weights)
    h = _wo_ffn(attn, x2, wo_b, ffn_norm, wg0_b, wu0_b, wd0_b)
    out = _conv_ffn(h, conv_norm, cwin_b, conv_w, cwout_b, ffn_norm,
                    wg1_b, wu1_b, wd1_b, final_norm[None, :], seq)
    return out.reshape(bsz, seq, d)
```

```python
import functools
import math

import jax
import jax.numpy as jnp
import numpy as np
from jax import lax
from jax.experimental import pallas as pl
from jax.experimental.pallas import tpu as pltpu

D_MODEL = 1024
CHUNK = 64
N_HEADS = 8
QK_NOPE_DIM = 128
QK_ROPE_DIM = 64
V_HEAD_DIM = 128
Q_LORA_RANK = 512
KV_LORA_RANK = 256
ROPE_THETA = 10000.0
CONV_WIDTH = 3
RMS_EPS = 1e-6

LANES = 128
SUBLANES = 8
BF16_SUBLANES = 16
MXU_COLS = 256
VMEM_LIMIT_BYTES = 56 * 1024 * 1024

TOKEN_TILE = 1024
ATTN_TILE = 256
ATTN_CHAINS = 2
FF_CHUNK = MXU_COLS
CONV_CHUNK = MXU_COLS

BF16 = jnp.bfloat16
F32 = jnp.float32


def _dot(a, b):
    return jnp.dot(a, b, preferred_element_type=F32)


def _rms(x, g):
    return x * lax.rsqrt(jnp.mean(x * x, axis=-1, keepdims=True) + RMS_EPS) * g


def _const_spec(shape):
    return pl.BlockSpec(shape, lambda *_: (0,) * len(shape), pipeline_mode=pl.Buffered(1))


def _layer0_spec(w):
    return pl.BlockSpec((None,) + w.shape[1:], lambda *_: (0,) * w.ndim, pipeline_mode=pl.Buffered(1))


def _params(semantics):
    return pltpu.CompilerParams(dimension_semantics=semantics, vmem_limit_bytes=VMEM_LIMIT_BYTES)


def _cast_specs(w, layer, steps, step_of):
    _, rows, cols = w.shape
    out_shape = jax.ShapeDtypeStruct((rows, cols), BF16)
    if rows % (steps * BF16_SUBLANES) == 0:
        rb = rows // steps
        return (pl.BlockSpec((None, rb, cols), lambda *g: (layer, step_of(*g), 0)),
                pl.BlockSpec((rb, cols), lambda *g: (step_of(*g), 0)), out_shape, 1)
    per = steps // (cols // LANES)
    return (pl.BlockSpec((None, rows, LANES), lambda *g: (layer, 0, step_of(*g) // per)),
            pl.BlockSpec((rows, LANES), lambda *g: (0, step_of(*g) // per)), out_shape, per)


def _cast_side_work(step, cast_in_refs, cast_out_refs, periods):
    for src, dst, period in zip(cast_in_refs, cast_out_refs, periods):
        if period == 1:
            dst[...] = src[...].astype(BF16)
        else:
            @pl.when(step % period == 0)
            def _(src=src, dst=dst):
                dst[...] = src[...].astype(BF16)


def _mla_pre_kernel(x_ref, pos_ref, g_ref, w_in_ref, gcq_ref, gckv_ref, w_uq_ref, w_ukv_ref, invf_ref,
                    qn_ref, qr_ref, kn_ref, kr_ref, v_ref, w_in_b, w_uq_b, w_ukv_b, *, scale):
    tm = x_ref.shape[0]
    kr_col = Q_LORA_RANK + KV_LORA_RANK

    @pl.when(pl.program_id(0) == 0)
    def _():
        w_in_b[:, :kr_col] = w_in_ref[:, :kr_col].astype(BF16)
        w_in_b[:, kr_col:] = jnp.zeros((D_MODEL, LANES), BF16)
        w_in_b[:, kr_col:kr_col + QK_ROPE_DIM] = w_in_ref[:, kr_col:].astype(BF16)
        pair = 2 * (QK_NOPE_DIM + QK_ROPE_DIM)
        dst = lax.broadcasted_iota(jnp.int32, (pair, pair), 1)
        src = jnp.where(dst < QK_NOPE_DIM, dst,
                        jnp.where(dst < 2 * QK_NOPE_DIM, dst + QK_ROPE_DIM,
                                  jnp.where(dst < 2 * QK_NOPE_DIM + QK_ROPE_DIM, dst - QK_NOPE_DIM, dst)))
        select = (lax.broadcasted_iota(jnp.int32, (pair, pair), 0) == src).astype(BF16)
        n_nope = N_HEADS * QK_NOPE_DIM
        for p in range(N_HEADS // 2):
            wp = _dot(w_uq_ref[:, p * pair:(p + 1) * pair].astype(BF16), select).astype(BF16)
            w_uq_b[:, 2 * p * QK_NOPE_DIM:2 * (p + 1) * QK_NOPE_DIM] = wp[:, :2 * QK_NOPE_DIM]
            w_uq_b[:, n_nope + p * LANES:n_nope + (p + 1) * LANES] = wp[:, 2 * QK_NOPE_DIM:]
        w_ukv_b[...] = w_ukv_ref[...].astype(BF16)

    xn = _rms(x_ref[...], g_ref[...]).astype(BF16)
    proj = _dot(xn, w_in_b[...])
    cq = _rms(proj[:, :Q_LORA_RANK], gcq_ref[...]).astype(BF16)
    ckv = _rms(proj[:, Q_LORA_RANK:kr_col], gckv_ref[...]).astype(BF16)
    kr = proj[:, kr_col:]

    n_freq = QK_ROPE_DIM // 2
    n_grp = LANES // n_freq
    blk = tm // n_grp
    grp_c = lax.broadcasted_iota(jnp.int32, (blk, LANES), 1) // n_freq
    pos = pos_ref[...].astype(F32)
    pos_c = pos[(n_grp - 1) * blk:, :]
    for a in reversed(range(n_grp - 1)):
        pos_c = jnp.where(grp_c == a, pos[a * blk:(a + 1) * blk, :], pos_c)
    ang_c = pos_c * invf_ref[...]
    cos_c = jnp.cos(ang_c)
    sin_c = jnp.sin(ang_c)
    cos_r = [cos_c] + [pltpu.roll(cos_c, n_freq * k, 1) for k in range(1, n_grp)]
    sin_r = [sin_c] + [pltpu.roll(sin_c, n_freq * k, 1) for k in range(1, n_grp)]
    nsin_r = [-t for t in sin_r]

    def expand(a, even_tabs, odd_tabs):
        out = None
        for g in reversed(range(n_grp)):
            tab = (even_tabs if g % 2 == 0 else odd_tabs)[(g - a) % n_grp]
            out = tab if out is None else jnp.where(grp_c == g, tab, out)
        return out

    cos = jnp.concatenate([expand(a, cos_r, cos_r) for a in range(n_grp)], axis=0)
    sin_signed = jnp.concatenate([expand(a, nsin_r, sin_r) for a in range(n_grp)], axis=0)
    lane = lax.broadcasted_iota(jnp.int32, (tm, LANES), 1)
    first_half = (lane % QK_ROPE_DIM) < n_freq

    def rope(t):
        partner = jnp.where(first_half,
                            pltpu.roll(t, LANES - n_freq, 1),
                            pltpu.roll(t, n_freq, 1))
        return t * cos + partner * sin_signed

    kr_rot = rope(kr)
    kr_ref[0] = kr_rot.astype(BF16)
    kr_ref[1] = pltpu.roll(kr_rot, QK_ROPE_DIM, 1).astype(BF16)

    q = _dot(cq, w_uq_b[...])
    for h in range(N_HEADS):
        qn_ref[h] = (q[:, h * LANES:(h + 1) * LANES] * scale).astype(BF16)
    for p in range(N_HEADS // 2):
        pair_cols = slice((N_HEADS + p) * LANES, (N_HEADS + p + 1) * LANES)
        qr_ref[p] = (rope(q[:, pair_cols]) * scale).astype(BF16)

    kv = _dot(ckv, w_ukv_b[...])
    for h in range(N_HEADS):
        kn_ref[h] = kv[:, 2 * h * LANES:(2 * h + 1) * LANES].astype(BF16)
        v_ref[h] = kv[:, (2 * h + 1) * LANES:(2 * h + 2) * LANES].astype(BF16)


def _mla_pre(x2, pos, g, w_in, gcq, gckv, w_uq, w_ukv, invf):
    t = x2.shape[0]
    tm = TOKEN_TILE
    scale = math.log2(math.e) / math.sqrt(QK_NOPE_DIM + QK_ROPE_DIM)

    def stacked(n):
        return (pl.BlockSpec((n, tm, LANES), lambda i: (0, i, 0)), jax.ShapeDtypeStruct((n, t, LANES), BF16))

    specs, shapes = zip(stacked(N_HEADS), stacked(N_HEADS // 2), stacked(N_HEADS), stacked(2), stacked(N_HEADS))
    return pl.pallas_call(
        functools.partial(_mla_pre_kernel, scale=scale),
        grid=(t // tm,),
        in_specs=[
            pl.BlockSpec((tm, D_MODEL), lambda i: (i, 0)),
            pl.BlockSpec((tm, 1), lambda i: (i, 0)),
            _const_spec(g.shape), _layer0_spec(w_in), _const_spec(gcq.shape), _const_spec(gckv.shape),
            _layer0_spec(w_uq), _layer0_spec(w_ukv), _const_spec(invf.shape),
        ],
        out_specs=list(specs),
        out_shape=list(shapes),
        scratch_shapes=[
            pltpu.VMEM((D_MODEL, Q_LORA_RANK + KV_LORA_RANK + LANES), BF16),
            pltpu.VMEM(w_uq.shape[1:], BF16),
            pltpu.VMEM(w_ukv.shape[1:], BF16),
        ],
        compiler_params=_params(("arbitrary",)),
        name="mla_pre",
    )(x2, pos, g, w_in, gcq, gckv, w_uq, w_ukv, invf)


def _attn_kernel(*refs, cast_periods):
    n_cast = len(cast_periods)
    qn_ref, qr_ref, kn_ref, kr_ref, v_ref = refs[:5]
    o_ref = refs[5 + n_cast]
    m_ref, acc_ref = refs[-2:]
    step_id = pl.program_id(0) * pl.num_programs(1) + pl.program_id(1)
    _cast_side_work(step_id, refs[5:5 + n_cast], refs[6 + n_cast:-2], cast_periods)

    seq = qr_ref.shape[0]
    ta = ATTN_TILE
    groups = ta // LANES
    row_chunk = lax.broadcasted_iota(jnp.int32, (ta, ta), 0) // CHUNK
    col_chunk = lax.broadcasted_iota(jnp.int32, (ta, ta), 1) // CHUNK
    diag_mask = col_chunk <= row_chunk
    ones = jnp.ones((ta, LANES), BF16)

    def tile_slice(i):
        return pl.ds(i * ta, ta)

    def kv_tile(e, j):
        ks = tile_slice(j)
        k = jnp.concatenate([kn_ref[e, ks, :], kr_ref[e, ks, :]], axis=1)
        v_ext = jnp.concatenate([v_ref[e, ks, :], ones], axis=1)
        return k, v_ext

    def step(e, c, qs, k, v_ext, masked):
        q = jnp.concatenate([qn_ref[e, qs, :], qr_ref[qs, :]], axis=1)
        s = lax.dot_general(q, k, (((1,), (1,)), ((), ())), preferred_element_type=F32)
        if masked:
            s = jnp.where(diag_mask, s, -1e30)
        parts = [s[:, g * LANES:(g + 1) * LANES] for g in range(groups)]
        m_prev = m_ref[c]
        m_new = jnp.maximum(m_prev, jnp.max(functools.reduce(jnp.maximum, parts), axis=1, keepdims=True))
        alpha = jnp.exp2(m_prev - m_new)
        p = jnp.concatenate([jnp.exp2(part - m_new).astype(BF16) for part in parts], axis=1)
        acc_ref[c] = acc_ref[c] * jnp.concatenate([alpha, alpha], axis=1) + _dot(p, v_ext)
        m_ref[c] = m_new

    def finish(e, c, qs):
        acc = acc_ref[c]
        o_ref[e, qs, :] = (acc[:, :V_HEAD_DIM] / acc[:, V_HEAD_DIM:]).astype(o_ref.dtype)

    for e in range(2):
        for group in range(seq // (ATTN_CHAINS * ta)):
            first = group * ATTN_CHAINS
            m_ref[...] = jnp.full(m_ref.shape, -1e30, F32)
            acc_ref[...] = jnp.zeros(acc_ref.shape, F32)
            for j in range(first + ATTN_CHAINS):
                k, v_ext = kv_tile(e, j)
                for c in range(max(0, j - first), ATTN_CHAINS):
                    step(e, c, tile_slice(first + c), k, v_ext, masked=(j == first + c))
                if j >= first:
                    finish(e, j - first, tile_slice(j))


def _attention(qn, qr, kn, kr, v, bsz, seq, casts):
    t = bsz * seq
    ta = ATTN_TILE
    pairs = N_HEADS // 2
    pair_spec = pl.BlockSpec((2, seq, LANES), lambda b, p: (p, b, 0))
    cast_specs = [_cast_specs(w, layer, bsz * pairs, lambda b, p: b * pairs + p) for w, layer in casts]
    return pl.pallas_call(
        functools.partial(_attn_kernel, cast_periods=tuple(c[3] for c in cast_specs)),
        grid=(bsz, pairs),
        in_specs=[pair_spec,
                  pl.BlockSpec((None, seq, LANES), lambda b, p: (p, b, 0)),
                  pair_spec,
                  pl.BlockSpec((2, seq, LANES), lambda b, p: (0, b, 0)),
                  pair_spec] + [c[0] for c in cast_specs],
        out_specs=[pair_spec] + [c[1] for c in cast_specs],
        out_shape=[jax.ShapeDtypeStruct((N_HEADS, t, LANES), BF16)] + [c[2] for c in cast_specs],
        scratch_shapes=[pltpu.VMEM((ATTN_CHAINS, ta, LANES), F32),
                        pltpu.VMEM((ATTN_CHAINS, ta, 2 * LANES), F32)],
        compiler_params=_params(("arbitrary", "arbitrary")),
        name="attention",
    )(qn, qr, kn, kr, v, *[w for w, _ in casts])


def _swiglu(h1, g, wg_ref, wu_ref, wd_ref, act_ref):
    d_ff = wg_ref.shape[1]
    hn = _rms(h1, g).astype(BF16)
    for c in range(d_ff // FF_CHUNK):
        cs = slice(c * FF_CHUNK, (c + 1) * FF_CHUNK)
        gate = _dot(hn, wg_ref[:, cs])
        up = _dot(hn, wu_ref[:, cs])
        act_ref[:, cs] = (jax.nn.silu(gate) * up).astype(BF16)
    return _dot(act_ref[...], wd_ref[...])


def _wo_ffn_kernel(a_ref, h_ref, wo_ref, g_ref, wg_ref, wu_ref, wd_ref, out_ref, act_ref):
    a = jnp.concatenate([a_ref[h] for h in range(N_HEADS)], axis=1)
    h1 = h_ref[...] + _dot(a, wo_ref[...])
    out_ref[...] = h1 + _swiglu(h1, g_ref[0:1, :], wg_ref, wu_ref, wd_ref, act_ref)


def _wo_ffn(attn, h, wo, g, wg, wu, wd):
    t = h.shape[0]
    tm = TOKEN_TILE
    return pl.pallas_call(
        _wo_ffn_kernel,
        grid=(t // tm,),
        in_specs=[
            pl.BlockSpec((N_HEADS, tm, LANES), lambda i: (0, i, 0)),
            pl.BlockSpec((tm, D_MODEL), lambda i: (i, 0)),
            _const_spec(wo.shape), _const_spec(g.shape), _const_spec(wg.shape),
            _const_spec(wu.shape), _const_spec(wd.shape),
        ],
        out_specs=pl.BlockSpec((tm, D_MODEL), lambda i: (i, 0)),
        out_shape=jax.ShapeDtypeStruct((t, D_MODEL), F32),
        scratch_shapes=[pltpu.VMEM((tm, wg.shape[1]), BF16)],
        compiler_params=_params(("parallel",)),
        name="wo_ffn",
    )(attn, h, wo, g, wg, wu, wd)


def _conv_ffn_kernel(h_ref, gc_ref, win_ref, cw_ref, wout_ref, gf_ref, wg_ref, wu_ref, wd_ref,
                     gfin_ref, out_ref, ubuf_ref, hist_ref, act_ref, *, tiles_per_seq):
    tm = h_ref.shape[0]
    pad = SUBLANES

    @pl.when(pl.program_id(0) % tiles_per_seq == 0)
    def _():
        hist_ref[...] = jnp.zeros(hist_ref.shape, F32)

    hn = _rms(h_ref[...], gc_ref[...]).astype(BF16)
    for c in range(D_MODEL // CONV_CHUNK):
        cs = slice(c * CONV_CHUNK, (c + 1) * CONV_CHUNK)
        b_gate = _dot(hn, win_ref[:, c * CONV_CHUNK:(c + 1) * CONV_CHUNK])
        c_gate = _dot(hn, win_ref[:, D_MODEL + c * CONV_CHUNK:D_MODEL + (c + 1) * CONV_CHUNK])
        xp = _dot(hn, win_ref[:, 2 * D_MODEL + c * CONV_CHUNK:2 * D_MODEL + (c + 1) * CONV_CHUNK])
        u = c_gate * xp
        ubuf = ubuf_ref.at[c % 2]
        ubuf[0:pad, :] = hist_ref[:, cs]
        ubuf[pad:pad + tm, :] = u
        hist_ref[:, cs] = ubuf[tm:tm + pad, :]
        u_conv = (cw_ref[0:1, cs] * ubuf[pad - 2:pad - 2 + tm, :]
                  + cw_ref[1:2, cs] * ubuf[pad - 1:pad - 1 + tm, :]
                  + cw_ref[2:3, cs] * u)
        act_ref[:, cs] = (b_gate * u_conv).astype(BF16)

    out_ref[...] = h_ref[...] + _dot(act_ref[:, :D_MODEL], wout_ref[...])
    h2 = out_ref[...] + _swiglu(out_ref[...], gf_ref[1:2, :], wg_ref, wu_ref, wd_ref, act_ref)
    out_ref[...] = _rms(h2, gfin_ref[...])


def _conv_ffn(h, gc, win, cw, wout, gf, wg, wu, wd, gfin, seq):
    t = h.shape[0]
    tm = TOKEN_TILE
    return pl.pallas_call(
        functools.partial(_conv_ffn_kernel, tiles_per_seq=seq // tm),
        grid=(t // tm,),
        in_specs=[
            pl.BlockSpec((tm, D_MODEL), lambda i: (i, 0)),
            _const_spec(gc.shape), _const_spec(win.shape), _layer0_spec(cw),
            _const_spec(wout.shape), _const_spec(gf.shape), _const_spec(wg.shape),
            _const_spec(wu.shape), _const_spec(wd.shape), _const_spec(gfin.shape),
        ],
        out_specs=pl.BlockSpec((tm, D_MODEL), lambda i: (i, 0)),
        out_shape=jax.ShapeDtypeStruct((t, D_MODEL), F32),
        scratch_shapes=[
            pltpu.VMEM((2, tm + SUBLANES, CONV_CHUNK), F32),
            pltpu.VMEM((SUBLANES, D_MODEL), F32),
            pltpu.VMEM((tm, wg.shape[1]), BF16),
        ],
        compiler_params=_params(("arbitrary",)),
        name="conv_ffn",
    )(h, gc, win, cw, wout, gf, wg, wu, wd, gfin)


def _rope_inv_freq_lanes():
    inv = 1.0 / (ROPE_THETA ** (np.arange(0, QK_ROPE_DIM, 2, dtype=np.float32) / np.float32(QK_ROPE_DIM)))
    inv = inv.astype(np.float32)
    return jnp.asarray(np.tile(inv, LANES // (QK_ROPE_DIM // 2))[None, :])


def kernel(x, positions, mla_norm, mla_w_in, mla_g_cq, mla_g_ckv, mla_w_uq, mla_w_ukv, mla_w_o,
           conv_norm, conv_w_in, conv_w, conv_w_out, ffn_norm, ffn_w_gate, ffn_w_up, ffn_w_down,
           final_norm):
    bsz, seq, d = x.shape
    t = bsz * seq
    assert d == D_MODEL and seq % TOKEN_TILE == 0 and seq % (ATTN_CHAINS * ATTN_TILE) == 0 and ATTN_TILE % CHUNK == 0
    x2 = x.reshape(t, d)
    qn, qr, kn, kr, v = _mla_pre(
        x2, positions.reshape(t, 1), mla_norm, mla_w_in, mla_g_cq, mla_g_ckv, mla_w_uq, mla_w_ukv,
        _rope_inv_freq_lanes())
    later_weights = [(mla_w_o, 0), (ffn_w_gate, 0), (ffn_w_up, 0), (ffn_w_down, 0),
                     (conv_w_in, 0), (conv_w_out, 0), (ffn_w_gate, 1), (ffn_w_up, 1), (ffn_w_down, 1)]
    (attn, wo_b, wg0_b, wu0_b, wd0_b, cwin_b, cwout_b, wg1_b, wu1_b, wd1_b) = _attention(
        qn, qr, kn, kr, v, bsz, seq, later_weights)
    h = _wo_ffn(attn, x2, wo_b, ffn_norm, wg0_b, wu0_b, wd0_b)
    out = _conv_ffn(h, conv_norm, cwin_b, conv_w, cwout_b, ffn_norm,
                    wg1_b, wu1_b, wd1_b, final_norm[None, :], seq)
    return out.reshape(bsz, seq, d)
```

```python
import functools
import math

import jax
import jax.numpy as jnp
import numpy as np
from jax import lax
from jax.experimental import pallas as pl
from jax.experimental.pallas import tpu as pltpu

D_MODEL = 1024
CHUNK = 64
N_HEADS = 8
QK_NOPE_DIM = 128
QK_ROPE_DIM = 64
V_HEAD_DIM = 128
Q_LORA_RANK = 512
KV_LORA_RANK = 256
ROPE_THETA = 10000.0
CONV_WIDTH = 3
RMS_EPS = 1e-6

LANES = 128
SUBLANES = 8
BF16_SUBLANES = 16
MXU_COLS = 256
VMEM_LIMIT_BYTES = 56 * 1024 * 1024

TOKEN_TILE = 1024
ATTN_TILE = 256
ATTN_CHAINS = 2
FF_CHUNK = MXU_COLS
CONV_CHUNK = MXU_COLS

BF16 = jnp.bfloat16
F32 = jnp.float32


def _dot(a, b):
    return jnp.dot(a, b, preferred_element_type=F32)


def _rms(x, g):
    return x * lax.rsqrt(jnp.mean(x * x, axis=-1, keepdims=True) + RMS_EPS) * g


def _const_spec(shape):
    return pl.BlockSpec(shape, lambda *_: (0,) * len(shape), pipeline_mode=pl.Buffered(1))


def _layer0_spec(w):
    return pl.BlockSpec((None,) + w.shape[1:], lambda *_: (0,) * w.ndim, pipeline_mode=pl.Buffered(1))


def _params(semantics):
    return pltpu.CompilerParams(dimension_semantics=semantics, vmem_limit_bytes=VMEM_LIMIT_BYTES)


def _cast_specs(w, layer, steps, step_of):
    _, rows, cols = w.shape
    out_shape = jax.ShapeDtypeStruct((rows, cols), BF16)
    if rows % (steps * BF16_SUBLANES) == 0:
        rb = rows // steps
        return (pl.BlockSpec((None, rb, cols), lambda *g: (layer, step_of(*g), 0)),
                pl.BlockSpec((rb, cols), lambda *g: (step_of(*g), 0)), out_shape, 1)
    per = steps // (cols // LANES)
    return (pl.BlockSpec((None, rows, LANES), lambda *g: (layer, 0, step_of(*g) // per)),
            pl.BlockSpec((rows, LANES), lambda *g: (0, step_of(*g) // per)), out_shape, per)


def _cast_side_work(step, cast_in_refs, cast_out_refs, periods):
    for src, dst, period in zip(cast_in_refs, cast_out_refs, periods):
        if period == 1:
            dst[...] = src[...].astype(BF16)
        else:
            @pl.when(step % period == 0)
            def _(src=src, dst=dst):
                dst[...] = src[...].astype(BF16)


def _mla_pre_kernel(x_ref, pos_ref, g_ref, w_in_ref, gcq_ref, gckv_ref, w_uq_ref, w_ukv_ref, invf_ref,
                    qn_ref, qr_ref, kn_ref, kr_ref, v_ref, w_in_b, w_uq_b, w_ukv_b, *, scale):
    tm = x_ref.shape[0]
    kr_col = Q_LORA_RANK + KV_LORA_RANK

    @pl.when(pl.program_id(0) == 0)
    def _():
        w_in_b[:, :kr_col] = w_in_ref[:, :kr_col].astype(BF16)
        w_in_b[:, kr_col:] = jnp.zeros((D_MODEL, LANES), BF16)
        w_in_b[:, kr_col:kr_col + QK_ROPE_DIM] = w_in_ref[:, kr_col:].astype(BF16)
        pair = 2 * (QK_NOPE_DIM + QK_ROPE_DIM)
        dst = lax.broadcasted_iota(jnp.int32, (pair, pair), 1)
        src = jnp.where(dst < QK_NOPE_DIM, dst,
                        jnp.where(dst < 2 * QK_NOPE_DIM, dst + QK_ROPE_DIM,
                                  jnp.where(dst < 2 * QK_NOPE_DIM + QK_ROPE_DIM, dst - QK_NOPE_DIM, dst)))
        select = (lax.broadcasted_iota(jnp.int32, (pair, pair), 0) == src).astype(BF16)
        n_nope = N_HEADS * QK_NOPE_DIM
        for p in range(N_HEADS // 2):
            wp = _dot(w_uq_ref[:, p * pair:(p + 1) * pair].astype(BF16), select).astype(BF16)
            w_uq_b[:, 2 * p * QK_NOPE_DIM:2 * (p + 1) * QK_NOPE_DIM] = wp[:, :2 * QK_NOPE_DIM]
            w_uq_b[:, n_nope + p * LANES:n_nope + (p + 1) * LANES] = wp[:, 2 * QK_NOPE_DIM:]
        w_ukv_b[...] = w_ukv_ref[...].astype(BF16)

    xn = _rms(x_ref[...], g_ref[...]).astype(BF16)
    proj = _dot(xn, w_in_b[...])
    cq = _rms(proj[:, :Q_LORA_RANK], gcq_ref[...]).astype(BF16)
    ckv = _rms(proj[:, Q_LORA_RANK:kr_col], gckv_ref[...]).astype(BF16)
    kr = proj[:, kr_col:]

    n_freq = QK_ROPE_DIM // 2
    n_grp = LANES // n_freq
    blk = tm // n_grp
    grp_c = lax.broadcasted_iota(jnp.int32, (blk, LANES), 1) // n_freq
    pos = pos_ref[...].astype(F32)
    pos_c = pos[(n_grp - 1) * blk:, :]
    for a in reversed(range(n_grp - 1)):
        pos_c = jnp.where(grp_c == a, pos[a * blk:(a + 1) * blk, :], pos_c)
    ang_c = pos_c * invf_ref[...]
    cos_c = jnp.cos(ang_c)
    sin_c = jnp.sin(ang_c)
    cos_r = [cos_c] + [pltpu.roll(cos_c, n_freq * k, 1) for k in range(1, n_grp)]
    sin_r = [sin_c] + [pltpu.roll(sin_c, n_freq * k, 1) for k in range(1, n_grp)]
    nsin_r = [-t for t in sin_r]

    def expand(a, even_tabs, odd_tabs):
        out = None
        for g in reversed(range(n_grp)):
            tab = (even_tabs if g % 2 == 0 else odd_tabs)[(g - a) % n_grp]
            out = tab if out is None else jnp.where(grp_c == g, tab, out)
        return out

    cos = jnp.concatenate([expand(a, cos_r, cos_r) for a in range(n_grp)], axis=0)
    sin_signed = jnp.concatenate([expand(a, nsin_r, sin_r) for a in range(n_grp)], axis=0)
    lane = lax.broadcasted_iota(jnp.int32, (tm, LANES), 1)
    first_half = (lane % QK_ROPE_DIM) < n_freq

    def rope(t):
        partner = jnp.where(first_half,
                            pltpu.roll(t, LANES - n_freq, 1),
                            pltpu.roll(t, n_freq, 1))
        return t * cos + partner * sin_signed

    kr_rot = rope(kr)
    kr_ref[0] = kr_rot.astype(BF16)
    kr_ref[1] = pltpu.roll(kr_rot, QK_ROPE_DIM, 1).astype(BF16)

    q = _dot(cq, w_uq_b[...])
    for h in range(N_HEADS):
        qn_ref[h] = (q[:, h * LANES:(h + 1) * LANES] * scale).astype(BF16)
    for p in range(N_HEADS // 2):
        pair_cols = slice((N_HEADS + p) * LANES, (N_HEADS + p + 1) * LANES)
        qr_ref[p] = (rope(q[:, pair_cols]) * scale).astype(BF16)

    kv = _dot(ckv, w_ukv_b[...])
    for h in range(N_HEADS):
        kn_ref[h] = kv[:, 2 * h * LANES:(2 * h + 1) * LANES].astype(BF16)
        v_ref[h] = kv[:, (2 * h + 1) * LANES:(2 * h + 2) * LANES].astype(BF16)


def _mla_pre(x2, pos, g, w_in, gcq, gckv, w_uq, w_ukv, invf):
    t = x2.shape[0]
    tm = TOKEN_TILE
    scale = math.log2(math.e) / math.sqrt(QK_NOPE_DIM + QK_ROPE_DIM)

    def stacked(n):
        return (pl.BlockSpec((n, tm, LANES), lambda i: (0, i, 0)), jax.ShapeDtypeStruct((n, t, LANES), BF16))

    specs, shapes = zip(stacked(N_HEADS), stacked(N_HEADS // 2), stacked(N_HEADS), stacked(2), stacked(N_HEADS))
    return pl.pallas_call(
        functools.partial(_mla_pre_kernel, scale=scale),
        grid=(t // tm,),
        in_specs=[
            pl.BlockSpec((tm, D_MODEL), lambda i: (i, 0)),
            pl.BlockSpec((tm, 1), lambda i: (i, 0)),
            _const_spec(g.shape), _layer0_spec(w_in), _const_spec(gcq.shape), _const_spec(gckv.shape),
            _layer0_spec(w_uq), _layer0_spec(w_ukv), _const_spec(invf.shape),
        ],
        out_specs=list(specs),
        out_shape=list(shapes),
        scratch_shapes=[
            pltpu.VMEM((D_MODEL, Q_LORA_RANK + KV_LORA_RANK + LANES), BF16),
            pltpu.VMEM(w_uq.shape[1:], BF16),
            pltpu.VMEM(w_ukv.shape[1:], BF16),
        ],
        compiler_params=_params(("arbitrary",)),
        name="mla_pre",
    )(x2, pos, g, w_in, gcq, gckv, w_uq, w_ukv, invf)


def _attn_kernel(*refs, cast_periods):
    n_cast = len(cast_periods)
    qn_ref, qr_ref, kn_ref, kr_ref, v_ref = refs[:5]
    o_ref = refs[5 + n_cast]
    m_ref, acc_ref = refs[-2:]
    step_id = pl.program_id(0) * pl.num_programs(1) + pl.program_id(1)
    _cast_side_work(step_id, refs[5:5 + n_cast], refs[6 + n_cast:-2], cast_periods)

    seq = qn_ref.shape[0]
    ta = ATTN_TILE
    groups = ta // LANES
    row_chunk = lax.broadcasted_iota(jnp.int32, (ta, ta), 0) // CHUNK
    col_chunk = lax.broadcasted_iota(jnp.int32, (ta, ta), 1) // CHUNK
    diag_mask = col_chunk <= row_chunk
    ones = jnp.ones((ta, LANES), BF16)

    def tile_slice(i):
        return pl.ds(i * ta, ta)

    def kv_tile(j):
        ks = tile_slice(j)
        k = jnp.concatenate([kn_ref[ks, :], kr_ref[ks, :]], axis=1)
        v_ext = jnp.concatenate([v_ref[ks, :], ones], axis=1)
        return k, v_ext

    def step(c, qs, k, v_ext, masked):
        q = jnp.concatenate([qn_ref[qs, :], qr_ref[qs, :]], axis=1)
        s = lax.dot_general(q, k, (((1,), (1,)), ((), ())), preferred_element_type=F32)
        if masked:
            s = jnp.where(diag_mask, s, -1e30)
        parts = [s[:, g * LANES:(g + 1) * LANES] for g in range(groups)]
        m_prev = m_ref[c]
        m_new = jnp.maximum(m_prev, jnp.max(functools.reduce(jnp.maximum, parts), axis=1, keepdims=True))
        alpha = jnp.exp2(m_prev - m_new)
        p = jnp.concatenate([jnp.exp2(part - m_new).astype(BF16) for part in parts], axis=1)
        acc_ref[c] = acc_ref[c] * jnp.concatenate([alpha, alpha], axis=1) + _dot(p, v_ext)
        m_ref[c] = m_new

    def finish(c, qs):
        acc = acc_ref[c]
        o_ref[qs, :] = (acc[:, :V_HEAD_DIM] / acc[:, V_HEAD_DIM:]).astype(o_ref.dtype)

    for group in range(seq // (ATTN_CHAINS * ta)):
        first = group * ATTN_CHAINS
        m_ref[...] = jnp.full(m_ref.shape, -1e30, F32)
        acc_ref[...] = jnp.zeros(acc_ref.shape, F32)
        for j in range(first + ATTN_CHAINS):
            k, v_ext = kv_tile(j)
            for c in range(max(0, j - first), ATTN_CHAINS):
                step(c, tile_slice(first + c), k, v_ext, masked=(j == first + c))
            if j >= first:
                finish(j - first, tile_slice(j))


def _attention(qn, qr, kn, kr, v, bsz, seq, casts):
    t = bsz * seq
    ta = ATTN_TILE
    head_spec = pl.BlockSpec((None, seq, LANES), lambda b, h: (h, b, 0))
    cast_specs = [_cast_specs(w, layer, bsz * N_HEADS, lambda b, h: b * N_HEADS + h) for w, layer in casts]
    return pl.pallas_call(
        functools.partial(_attn_kernel, cast_periods=tuple(c[3] for c in cast_specs)),
        grid=(bsz, N_HEADS),
        in_specs=[head_spec,
                  pl.BlockSpec((None, seq, LANES), lambda b, h: (h // 2, b, 0)),
                  head_spec,
                  pl.BlockSpec((None, seq, LANES), lambda b, h: (h % 2, b, 0)),
                  head_spec] + [c[0] for c in cast_specs],
        out_specs=[head_spec] + [c[1] for c in cast_specs],
        out_shape=[jax.ShapeDtypeStruct((N_HEADS, t, LANES), BF16)] + [c[2] for c in cast_specs],
        scratch_shapes=[pltpu.VMEM((ATTN_CHAINS, ta, LANES), F32),
                        pltpu.VMEM((ATTN_CHAINS, ta, 2 * LANES), F32)],
        compiler_params=_params(("arbitrary", "arbitrary")),
        name="attention",
    )(qn, qr, kn, kr, v, *[w for w, _ in casts])


def _row_halves(tm):
    return [pl.ds(r * (tm // 2), tm // 2) for r in range(2)]


def _swiglu_halves(h1s, g, wg_ref, wu_ref, wd_ref, act_ref, halves):
    d_ff = wg_ref.shape[1]
    hns = [_rms(h1, g).astype(BF16) for h1 in h1s]
    for c in range(d_ff // FF_CHUNK):
        cs = slice(c * FF_CHUNK, (c + 1) * FF_CHUNK)
        for rs, hn in zip(halves, hns):
            gate = _dot(hn, wg_ref[:, cs])
            up = _dot(hn, wu_ref[:, cs])
            act_ref[rs, cs] = (jax.nn.silu(gate) * up).astype(BF16)
    return [_dot(act_ref[rs, :], wd_ref[...]) for rs in halves]


def _wo_ffn_kernel(a_ref, h_ref, wo_ref, g_ref, wg_ref, wu_ref, wd_ref, out_ref, act_ref):
    halves = _row_halves(h_ref.shape[0])
    for rs in halves:
        a = jnp.concatenate([a_ref[h, rs, :] for h in range(N_HEADS)], axis=1)
        out_ref[rs, :] = h_ref[rs, :] + _dot(a, wo_ref[...])
    ys = _swiglu_halves([out_ref[rs, :] for rs in halves], g_ref[0:1, :], wg_ref, wu_ref, wd_ref, act_ref, halves)
    for rs, y in zip(halves, ys):
        out_ref[rs, :] = out_ref[rs, :] + y


def _wo_ffn(attn, h, wo, g, wg, wu, wd):
    t = h.shape[0]
    tm = TOKEN_TILE
    return pl.pallas_call(
        _wo_ffn_kernel,
        grid=(t // tm,),
        in_specs=[
            pl.BlockSpec((N_HEADS, tm, LANES), lambda i: (0, i, 0)),
            pl.BlockSpec((tm, D_MODEL), lambda i: (i, 0)),
            _const_spec(wo.shape), _const_spec(g.shape), _const_spec(wg.shape),
            _const_spec(wu.shape), _const_spec(wd.shape),
        ],
        out_specs=pl.BlockSpec((tm, D_MODEL), lambda i: (i, 0)),
        out_shape=jax.ShapeDtypeStruct((t, D_MODEL), F32),
        scratch_shapes=[pltpu.VMEM((tm, wg.shape[1]), BF16)],
        compiler_params=_params(("parallel",)),
        name="wo_ffn",
    )(attn, h, wo, g, wg, wu, wd)


def _conv_ffn_kernel(h_ref, gc_ref, win_ref, cw_ref, wout_ref, gf_ref, wg_ref, wu_ref, wd_ref,
                     gfin_ref, out_ref, ubuf_ref, hist_ref, act_ref, *, tiles_per_seq):
    tm = h_ref.shape[0]
    pad = SUBLANES

    @pl.when(pl.program_id(0) % tiles_per_seq == 0)
    def _():
        hist_ref[...] = jnp.zeros(hist_ref.shape, F32)

    halves = _row_halves(tm)
    half = tm // 2
    hns = [_rms(h_ref[rs, :], gc_ref[...]).astype(BF16) for rs in halves]
    for c in range(D_MODEL // CONV_CHUNK):
        cs = slice(c * CONV_CHUNK, (c + 1) * CONV_CHUNK)
        ubuf = ubuf_ref.at[c % 2]
        ubuf[0:pad, :] = hist_ref[:, cs]
        b_gates = []
        for r, hn in enumerate(hns):
            b_gates.append(_dot(hn, win_ref[:, c * CONV_CHUNK:(c + 1) * CONV_CHUNK]))
            c_gate = _dot(hn, win_ref[:, D_MODEL + c * CONV_CHUNK:D_MODEL + (c + 1) * CONV_CHUNK])
            xp = _dot(hn, win_ref[:, 2 * D_MODEL + c * CONV_CHUNK:2 * D_MODEL + (c + 1) * CONV_CHUNK])
            ubuf[pad + r * half:pad + (r + 1) * half, :] = c_gate * xp
        hist_ref[:, cs] = ubuf[tm:tm + pad, :]
        for r, (rs, b_gate) in enumerate(zip(halves, b_gates)):
            lo = pad + r * half
            u_conv = (cw_ref[0:1, cs] * ubuf[lo - 2:lo - 2 + half, :]
                      + cw_ref[1:2, cs] * ubuf[lo - 1:lo - 1 + half, :]
                      + cw_ref[2:3, cs] * ubuf[lo:lo + half, :])
            act_ref[rs, cs] = (b_gate * u_conv).astype(BF16)

    for rs in halves:
        out_ref[rs, :] = h_ref[rs, :] + _dot(act_ref[rs, :D_MODEL], wout_ref[...])
    ys = _swiglu_halves([out_ref[rs, :] for rs in halves], gf_ref[1:2, :], wg_ref, wu_ref, wd_ref, act_ref, halves)
    for rs, y in zip(halves, ys):
        out_ref[rs, :] = _rms(out_ref[rs, :] + y, gfin_ref[...])


def _conv_ffn(h, gc, win, cw, wout, gf, wg, wu, wd, gfin, seq):
    t = h.shape[0]
    tm = TOKEN_TILE
    return pl.pallas_call(
        functools.partial(_conv_ffn_kernel, tiles_per_seq=seq // tm),
        grid=(t // tm,),
        in_specs=[
            pl.BlockSpec((tm, D_MODEL), lambda i: (i, 0)),
            _const_spec(gc.shape), _const_spec(win.shape), _layer0_spec(cw),
            _const_spec(wout.shape), _const_spec(gf.shape), _const_spec(wg.shape),
            _const_spec(wu.shape), _const_spec(wd.shape), _const_spec(gfin.shape),
        ],
        out_specs=pl.BlockSpec((tm, D_MODEL), lambda i: (i, 0)),
        out_shape=jax.ShapeDtypeStruct((t, D_MODEL), F32),
        scratch_shapes=[
            pltpu.VMEM((2, tm + SUBLANES, CONV_CHUNK), F32),
            pltpu.VMEM((SUBLANES, D_MODEL), F32),
            pltpu.VMEM((tm, wg.shape[1]), BF16),
        ],
        compiler_params=_params(("arbitrary",)),
        name="conv_ffn",
    )(h, gc, win, cw, wout, gf, wg, wu, wd, gfin)


def _rope_inv_freq_lanes():
    inv = 1.0 / (ROPE_THETA ** (np.arange(0, QK_ROPE_DIM, 2, dtype=np.float32) / np.float32(QK_ROPE_DIM)))
    inv = inv.astype(np.float32)
    return jnp.asarray(np.tile(inv, LANES // (QK_ROPE_DIM // 2))[None, :])


def kernel(x, positions, mla_norm, mla_w_in, mla_g_cq, mla_g_ckv, mla_w_uq, mla_w_ukv, mla_w_o,
           conv_norm, conv_w_in, conv_w, conv_w_out, ffn_norm, ffn_w_gate, ffn_w_up, ffn_w_down,
           final_norm):
    bsz, seq, d = x.shape
    t = bsz * seq
    assert d == D_MODEL and seq % TOKEN_TILE == 0 and seq % (ATTN_CHAINS * ATTN_TILE) == 0 and ATTN_TILE % CHUNK == 0
    x2 = x.reshape(t, d)
    qn, qr, kn, kr, v = _mla_pre(
        x2, positions.reshape(t, 1), mla_norm, mla_w_in, mla_g_cq, mla_g_ckv, mla_w_uq, mla_w_ukv,
        _rope_inv_freq_lanes())
    later_weights = [(mla_w_o, 0), (ffn_w_gate, 0), (ffn_w_up, 0), (ffn_w_down, 0),
                     (conv_w_in, 0), (conv_w_out, 0), (ffn_w_gate, 1), (ffn_w_up, 1), (ffn_w_down, 1)]
    (attn, wo_b, wg0_b, wu0_b, wd0_b, cwin_b, cwout_b, wg1_b, wu1_b, wd1_b) = _attention(
        qn, qr, kn, kr, v, bsz, seq, later_weights)
    h = _wo_ffn(attn, x2, wo_b, ffn_norm, wg0_b, wu0_b, wd0_b)
    out = _conv_ffn(h, conv_norm, cwin_b, conv_w, cwout_b, ffn_norm,
                    wg1_b, wu1_b, wd1_b, final_norm[None, :], seq)
    return out.reshape(bsz, seq, d)
```

```python
import functools
import math

import jax
import jax.numpy as jnp
import numpy as np
from jax import lax
from jax.experimental import pallas as pl
from jax.experimental.pallas import tpu as pltpu

D_MODEL = 1024
CHUNK = 64
N_HEADS = 8
QK_NOPE_DIM = 128
QK_ROPE_DIM = 64
V_HEAD_DIM = 128
Q_LORA_RANK = 512
KV_LORA_RANK = 256
ROPE_THETA = 10000.0
CONV_WIDTH = 3
RMS_EPS = 1e-6

LANES = 128
SUBLANES = 8
BF16_SUBLANES = 16
MXU_COLS = 256
VMEM_LIMIT_BYTES = 56 * 1024 * 1024

TOKEN_TILE = 1024
ATTN_TILE = 256
ATTN_CHAINS = 2
FF_CHUNK = MXU_COLS
CONV_CHUNK = MXU_COLS

BF16 = jnp.bfloat16
F32 = jnp.float32


def _dot(a, b):
    return jnp.dot(a, b, preferred_element_type=F32)


def _rms(x, g):
    return x * lax.rsqrt(jnp.mean(x * x, axis=-1, keepdims=True) + RMS_EPS) * g


def _const_spec(shape):
    return pl.BlockSpec(shape, lambda *_: (0,) * len(shape), pipeline_mode=pl.Buffered(1))


def _layer0_spec(w):
    return pl.BlockSpec((None,) + w.shape[1:], lambda *_: (0,) * w.ndim, pipeline_mode=pl.Buffered(1))


def _params(semantics):
    return pltpu.CompilerParams(dimension_semantics=semantics, vmem_limit_bytes=VMEM_LIMIT_BYTES)


def _cast_specs(w, layer, steps, step_of):
    _, rows, cols = w.shape
    out_shape = jax.ShapeDtypeStruct((rows, cols), BF16)
    if rows % (steps * BF16_SUBLANES) == 0:
        rb = rows // steps
        return (pl.BlockSpec((None, rb, cols), lambda *g: (layer, step_of(*g), 0)),
                pl.BlockSpec((rb, cols), lambda *g: (step_of(*g), 0)), out_shape, 1)
    per = steps // (cols // LANES)
    return (pl.BlockSpec((None, rows, LANES), lambda *g: (layer, 0, step_of(*g) // per)),
            pl.BlockSpec((rows, LANES), lambda *g: (0, step_of(*g) // per)), out_shape, per)


def _cast_side_work(step, cast_in_refs, cast_out_refs, periods):
    for src, dst, period in zip(cast_in_refs, cast_out_refs, periods):
        if period == 1:
            dst[...] = src[...].astype(BF16)
        else:
            @pl.when(step % period == 0)
            def _(src=src, dst=dst):
                dst[...] = src[...].astype(BF16)


def _mla_pre_kernel(x_ref, pos_ref, g_ref, w_in_ref, gcq_ref, gckv_ref, w_uq_ref, w_ukv_ref, invf_ref,
                    qn_ref, qr_ref, kn_ref, kr_ref, v_ref, w_in_b, w_uq_b, w_ukv_b, *, scale):
    tm = x_ref.shape[0]
    kr_col = Q_LORA_RANK + KV_LORA_RANK

    @pl.when(pl.program_id(0) == 0)
    def _():
        w_in_b[:, :kr_col] = w_in_ref[:, :kr_col].astype(BF16)
        w_in_b[:, kr_col:] = jnp.zeros((D_MODEL, LANES), BF16)
        w_in_b[:, kr_col:kr_col + QK_ROPE_DIM] = w_in_ref[:, kr_col:].astype(BF16)
        pair = 2 * (QK_NOPE_DIM + QK_ROPE_DIM)
        dst = lax.broadcasted_iota(jnp.int32, (pair, pair), 1)
        src = jnp.where(dst < QK_NOPE_DIM, dst,
                        jnp.where(dst < 2 * QK_NOPE_DIM, dst + QK_ROPE_DIM,
                                  jnp.where(dst < 2 * QK_NOPE_DIM + QK_ROPE_DIM, dst - QK_NOPE_DIM, dst)))
        select = (lax.broadcasted_iota(jnp.int32, (pair, pair), 0) == src).astype(BF16)
        n_nope = N_HEADS * QK_NOPE_DIM
        for p in range(N_HEADS // 2):
            wp = _dot(w_uq_ref[:, p * pair:(p + 1) * pair].astype(BF16), select).astype(BF16)
            w_uq_b[:, 2 * p * QK_NOPE_DIM:2 * (p + 1) * QK_NOPE_DIM] = wp[:, :2 * QK_NOPE_DIM]
            w_uq_b[:, n_nope + p * LANES:n_nope + (p + 1) * LANES] = wp[:, 2 * QK_NOPE_DIM:]
        w_ukv_b[...] = w_ukv_ref[...].astype(BF16)

    xn = _rms(x_ref[...], g_ref[...]).astype(BF16)
    proj = _dot(xn, w_in_b[...])
    cq = _rms(proj[:, :Q_LORA_RANK], gcq_ref[...]).astype(BF16)
    ckv = _rms(proj[:, Q_LORA_RANK:kr_col], gckv_ref[...]).astype(BF16)
    kr = proj[:, kr_col:]

    n_freq = QK_ROPE_DIM // 2
    n_grp = LANES // n_freq
    blk = tm // n_grp
    grp_c = lax.broadcasted_iota(jnp.int32, (blk, LANES), 1) // n_freq
    pos = pos_ref[...].astype(F32)
    pos_c = pos[(n_grp - 1) * blk:, :]
    for a in reversed(range(n_grp - 1)):
        pos_c = jnp.where(grp_c == a, pos[a * blk:(a + 1) * blk, :], pos_c)
    ang_c = pos_c * invf_ref[...]
    cos_c = jnp.cos(ang_c)
    sin_c = jnp.sin(ang_c)
    cos_r = [cos_c] + [pltpu.roll(cos_c, n_freq * k, 1) for k in range(1, n_grp)]
    sin_r = [sin_c] + [pltpu.roll(sin_c, n_freq * k, 1) for k in range(1, n_grp)]
    nsin_r = [-t for t in sin_r]

    def expand(a, even_tabs, odd_tabs):
        out = None
        for g in reversed(range(n_grp)):
            tab = (even_tabs if g % 2 == 0 else odd_tabs)[(g - a) % n_grp]
            out = tab if out is None else jnp.where(grp_c == g, tab, out)
        return out

    cos = jnp.concatenate([expand(a, cos_r, cos_r) for a in range(n_grp)], axis=0)
    sin_signed = jnp.concatenate([expand(a, nsin_r, sin_r) for a in range(n_grp)], axis=0)
    lane = lax.broadcasted_iota(jnp.int32, (tm, LANES), 1)
    first_half = (lane % QK_ROPE_DIM) < n_freq

    def rope(t):
        partner = jnp.where(first_half,
                            pltpu.roll(t, LANES - n_freq, 1),
                            pltpu.roll(t, n_freq, 1))
        return t * cos + partner * sin_signed

    kr_rot = rope(kr)
    kr_ref[0] = kr_rot.astype(BF16)
    kr_ref[1] = pltpu.roll(kr_rot, QK_ROPE_DIM, 1).astype(BF16)

    q = _dot(cq, w_uq_b[...])
    for h in range(N_HEADS):
        qn_ref[h] = (q[:, h * LANES:(h + 1) * LANES] * scale).astype(BF16)
    for p in range(N_HEADS // 2):
        pair_cols = slice((N_HEADS + p) * LANES, (N_HEADS + p + 1) * LANES)
        qr_ref[p] = (rope(q[:, pair_cols]) * scale).astype(BF16)

    kv = _dot(ckv, w_ukv_b[...])
    for h in range(N_HEADS):
        kn_ref[h] = kv[:, 2 * h * LANES:(2 * h + 1) * LANES].astype(BF16)
        v_ref[h] = kv[:, (2 * h + 1) * LANES:(2 * h + 2) * LANES].astype(BF16)


def _mla_pre(x2, pos, g, w_in, gcq, gckv, w_uq, w_ukv, invf):
    t = x2.shape[0]
    tm = TOKEN_TILE
    scale = math.log2(math.e) / math.sqrt(QK_NOPE_DIM + QK_ROPE_DIM)

    def stacked(n):
        return (pl.BlockSpec((n, tm, LANES), lambda i: (0, i, 0)), jax.ShapeDtypeStruct((n, t, LANES), BF16))

    specs, shapes = zip(stacked(N_HEADS), stacked(N_HEADS // 2), stacked(N_HEADS), stacked(2), stacked(N_HEADS))
    return pl.pallas_call(
        functools.partial(_mla_pre_kernel, scale=scale),
        grid=(t // tm,),
        in_specs=[
            pl.BlockSpec((tm, D_MODEL), lambda i: (i, 0)),
            pl.BlockSpec((tm, 1), lambda i: (i, 0)),
            _const_spec(g.shape), _layer0_spec(w_in), _const_spec(gcq.shape), _const_spec(gckv.shape),
            _layer0_spec(w_uq), _layer0_spec(w_ukv), _const_spec(invf.shape),
        ],
        out_specs=list(specs),
        out_shape=list(shapes),
        scratch_shapes=[
            pltpu.VMEM((D_MODEL, Q_LORA_RANK + KV_LORA_RANK + LANES), BF16),
            pltpu.VMEM(w_uq.shape[1:], BF16),
            pltpu.VMEM(w_ukv.shape[1:], BF16),
        ],
        compiler_params=_params(("arbitrary",)),
        name="mla_pre",
    )(x2, pos, g, w_in, gcq, gckv, w_uq, w_ukv, invf)


def _attn_kernel(*refs, cast_periods):
    n_cast = len(cast_periods)
    qn_ref, qr_ref, kn_ref, kr_ref, v_ref = refs[:5]
    o_ref = refs[5 + n_cast]
    m_ref, acc_ref = refs[-2:]
    step_id = pl.program_id(0) * pl.num_programs(1) + pl.program_id(1)
    _cast_side_work(step_id, refs[5:5 + n_cast], refs[6 + n_cast:-2], cast_periods)

    seq = qn_ref.shape[0]
    ta = ATTN_TILE
    groups = ta // LANES
    row_chunk = lax.broadcasted_iota(jnp.int32, (ta, ta), 0) // CHUNK
    col_chunk = lax.broadcasted_iota(jnp.int32, (ta, ta), 1) // CHUNK
    diag_mask = col_chunk <= row_chunk
    ones = jnp.ones((ta, LANES), BF16)

    def tile_slice(i):
        return pl.ds(i * ta, ta)

    def kv_tile(j):
        ks = tile_slice(j)
        k = jnp.concatenate([kn_ref[ks, :], kr_ref[ks, :]], axis=1)
        v_ext = jnp.concatenate([v_ref[ks, :], ones], axis=1)
        return k, v_ext

    def step(c, qs, k, v_ext, masked):
        q = jnp.concatenate([qn_ref[qs, :], qr_ref[qs, :]], axis=1)
        s = lax.dot_general(q, k, (((1,), (1,)), ((), ())), preferred_element_type=F32)
        if masked:
            s = jnp.where(diag_mask, s, -1e30)
        parts = [s[:, g * LANES:(g + 1) * LANES] for g in range(groups)]
        m_prev = m_ref[c]
        m_new = jnp.maximum(m_prev, jnp.max(functools.reduce(jnp.maximum, parts), axis=1, keepdims=True))
        alpha = jnp.exp2(m_prev - m_new)
        p = jnp.concatenate([jnp.exp2(part - m_new).astype(BF16) for part in parts], axis=1)
        acc_ref[c] = acc_ref[c] * jnp.concatenate([alpha, alpha], axis=1) + _dot(p, v_ext)
        m_ref[c] = m_new

    def finish(c, qs):
        acc = acc_ref[c]
        o_ref[qs, :] = (acc[:, :V_HEAD_DIM] / acc[:, V_HEAD_DIM:]).astype(o_ref.dtype)

    for group in range(seq // (ATTN_CHAINS * ta)):
        first = group * ATTN_CHAINS
        m_ref[...] = jnp.full(m_ref.shape, -1e30, F32)
        acc_ref[...] = jnp.zeros(acc_ref.shape, F32)
        for j in range(first + ATTN_CHAINS):
            k, v_ext = kv_tile(j)
            for c in range(max(0, j - first), ATTN_CHAINS):
                step(c, tile_slice(first + c), k, v_ext, masked=(j == first + c))
            if j >= first:
                finish(j - first, tile_slice(j))


def _attention(qn, qr, kn, kr, v, bsz, seq, casts):
    t = bsz * seq
    ta = ATTN_TILE
    head_spec = pl.BlockSpec((None, seq, LANES), lambda b, h: (h, b, 0))
    cast_specs = [_cast_specs(w, layer, bsz * N_HEADS, lambda b, h: b * N_HEADS + h) for w, layer in casts]
    return pl.pallas_call(
        functools.partial(_attn_kernel, cast_periods=tuple(c[3] for c in cast_specs)),
        grid=(bsz, N_HEADS),
        in_specs=[head_spec,
                  pl.BlockSpec((None, seq, LANES), lambda b, h: (h // 2, b, 0)),
                  head_spec,
                  pl.BlockSpec((None, seq, LANES), lambda b, h: (h % 2, b, 0)),
                  head_spec] + [c[0] for c in cast_specs],
        out_specs=[head_spec] + [c[1] for c in cast_specs],
        out_shape=[jax.ShapeDtypeStruct((N_HEADS, t, LANES), BF16)] + [c[2] for c in cast_specs],
        scratch_shapes=[pltpu.VMEM((ATTN_CHAINS, ta, LANES), F32),
                        pltpu.VMEM((ATTN_CHAINS, ta, 2 * LANES), F32)],
        compiler_params=_params(("arbitrary", "arbitrary")),
        name="attention",
    )(qn, qr, kn, kr, v, *[w for w, _ in casts])


def _row_halves(tm):
    return [pl.ds(r * (tm // 2), tm // 2) for r in range(2)]


def _swiglu_halves(h1s, g, wg_ref, wu_ref, wd_ref, act_ref, halves):
    d_ff = wg_ref.shape[1]
    hns = [_rms(h1, g).astype(BF16) for h1 in h1s]
    for c in range(d_ff // FF_CHUNK):
        cs = slice(c * FF_CHUNK, (c + 1) * FF_CHUNK)
        for rs, hn in zip(halves, hns):
            gate = _dot(hn, wg_ref[:, cs])
            up = _dot(hn, wu_ref[:, cs])
            act_ref[rs, cs] = (jax.nn.silu(gate) * up).astype(BF16)
    return [_dot(act_ref[rs, :], wd_ref[...]) for rs in halves]


def _wo_ffn_kernel(a_ref, h_ref, wo_ref, g_ref, wg_ref, wu_ref, wd_ref, out_ref, act_ref):
    halves = _row_halves(h_ref.shape[0])
    for rs in halves:
        a = jnp.concatenate([a_ref[h, rs, :] for h in range(N_HEADS)], axis=1)
        out_ref[rs, :] = h_ref[rs, :] + _dot(a, wo_ref[...])
    ys = _swiglu_halves([out_ref[rs, :] for rs in halves], g_ref[0:1, :], wg_ref, wu_ref, wd_ref, act_ref, halves)
    for rs, y in zip(halves, ys):
        out_ref[rs, :] = out_ref[rs, :] + y


def _wo_ffn(attn, h, wo, g, wg, wu, wd):
    t = h.shape[0]
    tm = TOKEN_TILE
    return pl.pallas_call(
        _wo_ffn_kernel,
        grid=(t // tm,),
        in_specs=[
            pl.BlockSpec((N_HEADS, tm, LANES), lambda i: (0, i, 0)),
            pl.BlockSpec((tm, D_MODEL), lambda i: (i, 0)),
            _const_spec(wo.shape), _const_spec(g.shape), _const_spec(wg.shape),
            _const_spec(wu.shape), _const_spec(wd.shape),
        ],
        out_specs=pl.BlockSpec((tm, D_MODEL), lambda i: (i, 0)),
        out_shape=jax.ShapeDtypeStruct((t, D_MODEL), F32),
        scratch_shapes=[pltpu.VMEM((tm, wg.shape[1]), BF16)],
        compiler_params=_params(("parallel",)),
        name="wo_ffn",
    )(attn, h, wo, g, wg, wu, wd)


def _conv_ffn_kernel(h_ref, gc_ref, win_ref, cw_ref, wout_ref, gf_ref, wg_ref, wu_ref, wd_ref,
                     gfin_ref, out_ref, ubuf_ref, hist_ref, act_ref, *, tiles_per_seq):
    tm = h_ref.shape[0]
    pad = SUBLANES

    @pl.when(pl.program_id(0) % tiles_per_seq == 0)
    def _():
        hist_ref[...] = jnp.zeros(hist_ref.shape, F32)

    hn = _rms(h_ref[...], gc_ref[...]).astype(BF16)
    for c in range(D_MODEL // CONV_CHUNK):
        cs = slice(c * CONV_CHUNK, (c + 1) * CONV_CHUNK)
        b_gate = _dot(hn, win_ref[:, c * CONV_CHUNK:(c + 1) * CONV_CHUNK])
        c_gate = _dot(hn, win_ref[:, D_MODEL + c * CONV_CHUNK:D_MODEL + (c + 1) * CONV_CHUNK])
        xp = _dot(hn, win_ref[:, 2 * D_MODEL + c * CONV_CHUNK:2 * D_MODEL + (c + 1) * CONV_CHUNK])
        u = c_gate * xp
        ubuf = ubuf_ref.at[c % 2]
        ubuf[0:pad, :] = hist_ref[:, cs]
        ubuf[pad:pad + tm, :] = u
        hist_ref[:, cs] = ubuf[tm:tm + pad, :]
        u_conv = (cw_ref[0:1, cs] * ubuf[pad - 2:pad - 2 + tm, :]
                  + cw_ref[1:2, cs] * ubuf[pad - 1:pad - 1 + tm, :]
                  + cw_ref[2:3, cs] * u)
        act_ref[:, cs] = (b_gate * u_conv).astype(BF16)

    halves = _row_halves(tm)
    for rs in halves:
        out_ref[rs, :] = h_ref[rs, :] + _dot(act_ref[rs, :D_MODEL], wout_ref[...])
    ys = _swiglu_halves([out_ref[rs, :] for rs in halves], gf_ref[1:2, :], wg_ref, wu_ref, wd_ref, act_ref, halves)
    for rs, y in zip(halves, ys):
        out_ref[rs, :] = _rms(out_ref[rs, :] + y, gfin_ref[...])


def _conv_ffn(h, gc, win, cw, wout, gf, wg, wu, wd, gfin, seq):
    t = h.shape[0]
    tm = TOKEN_TILE
    return pl.pallas_call(
        functools.partial(_conv_ffn_kernel, tiles_per_seq=seq // tm),
        grid=(t // tm,),
        in_specs=[
            pl.BlockSpec((tm, D_MODEL), lambda i: (i, 0)),
            _const_spec(gc.shape), _const_spec(win.shape), _layer0_spec(cw),
            _const_spec(wout.shape), _const_spec(gf.shape), _const_spec(wg.shape),
            _const_spec(wu.shape), _const_spec(wd.shape), _const_spec(gfin.shape),
        ],
        out_specs=pl.BlockSpec((tm, D_MODEL), lambda i: (i, 0)),
        out_shape=jax.ShapeDtypeStruct((t, D_MODEL), F32),
        scratch_shapes=[
            pltpu.VMEM((2, tm + SUBLANES, CONV_CHUNK), F32),
            pltpu.VMEM((SUBLANES, D_MODEL), F32),
            pltpu.VMEM((tm, wg.shape[1]), BF16),
        ],
        compiler_params=_params(("arbitrary",)),
        name="conv_ffn",
    )(h, gc, win, cw, wout, gf, wg, wu, wd, gfin)


def _rope_inv_freq_lanes():
    inv = 1.0 / (ROPE_THETA ** (np.arange(0, QK_ROPE_DIM, 2, dtype=np.float32) / np.float32(QK_ROPE_DIM)))
    inv = inv.astype(np.float32)
    return jnp.asarray(np.tile(inv, LANES // (QK_ROPE_DIM // 2))[None, :])


def kernel(x, positions, mla_norm, mla_w_in, mla_g_cq, mla_g_ckv, mla_w_uq, mla_w_ukv, mla_w_o,
           conv_norm, conv_w_in, conv_w, conv_w_out, ffn_norm, ffn_w_gate, ffn_w_up, ffn_w_down,
           final_norm):
    bsz, seq, d = x.shape
    t = bsz * seq
    assert d == D_MODEL and seq % TOKEN_TILE == 0 and seq % (ATTN_CHAINS * ATTN_TILE) == 0 and ATTN_TILE % CHUNK == 0
    x2 = x.reshape(t, d)
    qn, qr, kn, kr, v = _mla_pre(
        x2, positions.reshape(t, 1), mla_norm, mla_w_in, mla_g_cq, mla_g_ckv, mla_w_uq, mla_w_ukv,
        _rope_inv_freq_lanes())
    later_weights = [(mla_w_o, 0), (ffn_w_gate, 0), (ffn_w_up, 0), (ffn_w_down, 0),
                     (conv_w_in, 0), (conv_w_out, 0), (ffn_w_gate, 1), (ffn_w_up, 1), (ffn_w_down, 1)]
    (attn, wo_b, wg0_b, wu0_b, wd0_b, cwin_b, cwout_b, wg1_b, wu1_b, wd1_b) = _attention(
        qn, qr, kn, kr, v, bsz, seq, later_weights)
    h = _wo_ffn(attn, x2, wo_b, ffn_norm, wg0_b, wu0_b, wd0_b)
    out = _conv_ffn(h, conv_norm, cwin_b, conv_w, cwout_b, ffn_norm,
                    wg1_b, wu1_b, wd1_b, final_norm[None, :], seq)
    return out.reshape(bsz, seq, d)
```

```python
import functools
import math

import jax
import jax.numpy as jnp
import numpy as np
from jax import lax
from jax.experimental import pallas as pl
from jax.experimental.pallas import tpu as pltpu

D_MODEL = 1024
CHUNK = 64
N_HEADS = 8
QK_NOPE_DIM = 128
QK_ROPE_DIM = 64
V_HEAD_DIM = 128
Q_LORA_RANK = 512
KV_LORA_RANK = 256
ROPE_THETA = 10000.0
CONV_WIDTH = 3
RMS_EPS = 1e-6

LANES = 128
SUBLANES = 8
BF16_SUBLANES = 16
MXU_COLS = 256
VMEM_LIMIT_BYTES = 56 * 1024 * 1024

TOKEN_TILE = 1024
ATTN_TILE = 256
ATTN_CHAINS = 2
FF_CHUNK = MXU_COLS
CONV_CHUNK = MXU_COLS

BF16 = jnp.bfloat16
F32 = jnp.float32


def _dot(a, b):
    return jnp.dot(a, b, preferred_element_type=F32)


def _rms(x, g):
    return x * lax.rsqrt(jnp.mean(x * x, axis=-1, keepdims=True) + RMS_EPS) * g


def _const_spec(shape):
    return pl.BlockSpec(shape, lambda *_: (0,) * len(shape), pipeline_mode=pl.Buffered(1))


def _layer0_spec(w):
    return pl.BlockSpec((None,) + w.shape[1:], lambda *_: (0,) * w.ndim, pipeline_mode=pl.Buffered(1))


def _params(semantics):
    return pltpu.CompilerParams(dimension_semantics=semantics, vmem_limit_bytes=VMEM_LIMIT_BYTES)


def _cast_specs(w, layer, steps, step_of):
    _, rows, cols = w.shape
    out_shape = jax.ShapeDtypeStruct((rows, cols), BF16)
    if rows % (steps * BF16_SUBLANES) == 0:
        rb = rows // steps
        return (pl.BlockSpec((None, rb, cols), lambda *g: (layer, step_of(*g), 0)),
                pl.BlockSpec((rb, cols), lambda *g: (step_of(*g), 0)), out_shape, 1)
    per = steps // (cols // LANES)
    return (pl.BlockSpec((None, rows, LANES), lambda *g: (layer, 0, step_of(*g) // per)),
            pl.BlockSpec((rows, LANES), lambda *g: (0, step_of(*g) // per)), out_shape, per)


def _cast_side_work(step, cast_in_refs, cast_out_refs, periods):
    for src, dst, period in zip(cast_in_refs, cast_out_refs, periods):
        if period == 1:
            dst[...] = src[...].astype(BF16)
        else:
            @pl.when(step % period == 0)
            def _(src=src, dst=dst):
                dst[...] = src[...].astype(BF16)


def _mla_pre_kernel(x_ref, pos_ref, g_ref, w_in_ref, gcq_ref, gckv_ref, w_uq_ref, w_ukv_ref, invf_ref,
                    qn_ref, qr_ref, kn_ref, kr_ref, v_ref, w_in_b, w_uq_b, w_ukv_b, *, scale):
    tm = x_ref.shape[0]
    kr_col = Q_LORA_RANK + KV_LORA_RANK

    @pl.when(pl.program_id(0) == 0)
    def _():
        w_in_b[:, :kr_col] = w_in_ref[:, :kr_col].astype(BF16)
        w_in_b[:, kr_col:] = jnp.zeros((D_MODEL, LANES), BF16)
        w_in_b[:, kr_col:kr_col + QK_ROPE_DIM] = w_in_ref[:, kr_col:].astype(BF16)
        pair = 2 * (QK_NOPE_DIM + QK_ROPE_DIM)
        dst = lax.broadcasted_iota(jnp.int32, (pair, pair), 1)
        src = jnp.where(dst < QK_NOPE_DIM, dst,
                        jnp.where(dst < 2 * QK_NOPE_DIM, dst + QK_ROPE_DIM,
                                  jnp.where(dst < 2 * QK_NOPE_DIM + QK_ROPE_DIM, dst - QK_NOPE_DIM, dst)))
        select = (lax.broadcasted_iota(jnp.int32, (pair, pair), 0) == src).astype(BF16)
        n_nope = N_HEADS * QK_NOPE_DIM
        for p in range(N_HEADS // 2):
            wp = _dot(w_uq_ref[:, p * pair:(p + 1) * pair].astype(BF16), select).astype(BF16)
            w_uq_b[:, 2 * p * QK_NOPE_DIM:2 * (p + 1) * QK_NOPE_DIM] = wp[:, :2 * QK_NOPE_DIM]
            w_uq_b[:, n_nope + p * LANES:n_nope + (p + 1) * LANES] = wp[:, 2 * QK_NOPE_DIM:]
        w_ukv_b[...] = w_ukv_ref[...].astype(BF16)

    xn = _rms(x_ref[...], g_ref[...]).astype(BF16)
    proj = _dot(xn, w_in_b[...])
    cq = _rms(proj[:, :Q_LORA_RANK], gcq_ref[...]).astype(BF16)
    ckv = _rms(proj[:, Q_LORA_RANK:kr_col], gckv_ref[...]).astype(BF16)
    kr = proj[:, kr_col:]

    n_freq = QK_ROPE_DIM // 2
    n_grp = LANES // n_freq
    blk = tm // n_grp
    grp_c = lax.broadcasted_iota(jnp.int32, (blk, LANES), 1) // n_freq
    pos = pos_ref[...].astype(F32)
    pos_c = pos[(n_grp - 1) * blk:, :]
    for a in reversed(range(n_grp - 1)):
        pos_c = jnp.where(grp_c == a, pos[a * blk:(a + 1) * blk, :], pos_c)
    ang_c = pos_c * invf_ref[...]
    cos_c = jnp.cos(ang_c)
    sin_c = jnp.sin(ang_c)
    cos_r = [cos_c] + [pltpu.roll(cos_c, n_freq * k, 1) for k in range(1, n_grp)]
    sin_r = [sin_c] + [pltpu.roll(sin_c, n_freq * k, 1) for k in range(1, n_grp)]
    nsin_r = [-t for t in sin_r]

    def expand(a, even_tabs, odd_tabs):
        out = None
        for g in reversed(range(n_grp)):
            tab = (even_tabs if g % 2 == 0 else odd_tabs)[(g - a) % n_grp]
            out = tab if out is None else jnp.where(grp_c == g, tab, out)
        return out

    cos = jnp.concatenate([expand(a, cos_r, cos_r) for a in range(n_grp)], axis=0)
    sin_signed = jnp.concatenate([expand(a, nsin_r, sin_r) for a in range(n_grp)], axis=0)
    lane = lax.broadcasted_iota(jnp.int32, (tm, LANES), 1)
    first_half = (lane % QK_ROPE_DIM) < n_freq

    def rope(t):
        partner = jnp.where(first_half,
                            pltpu.roll(t, LANES - n_freq, 1),
                            pltpu.roll(t, n_freq, 1))
        return t * cos + partner * sin_signed

    kr_rot = rope(kr)
    kr_ref[0] = kr_rot.astype(BF16)
    kr_ref[1] = pltpu.roll(kr_rot, QK_ROPE_DIM, 1).astype(BF16)

    q = _dot(cq, w_uq_b[...])
    for h in range(N_HEADS):
        qn_ref[h] = (q[:, h * LANES:(h + 1) * LANES] * scale).astype(BF16)
    for p in range(N_HEADS // 2):
        pair_cols = slice((N_HEADS + p) * LANES, (N_HEADS + p + 1) * LANES)
        qr_ref[p] = (rope(q[:, pair_cols]) * scale).astype(BF16)

    kv = _dot(ckv, w_ukv_b[...])
    for h in range(N_HEADS):
        kn_ref[h] = kv[:, 2 * h * LANES:(2 * h + 1) * LANES].astype(BF16)
        v_ref[h] = kv[:, (2 * h + 1) * LANES:(2 * h + 2) * LANES].astype(BF16)


def _mla_pre(x2, pos, g, w_in, gcq, gckv, w_uq, w_ukv, invf):
    t = x2.shape[0]
    tm = TOKEN_TILE
    scale = math.log2(math.e) / math.sqrt(QK_NOPE_DIM + QK_ROPE_DIM)

    def stacked(n):
        return (pl.BlockSpec((n, tm, LANES), lambda i: (0, i, 0)), jax.ShapeDtypeStruct((n, t, LANES), BF16))

    specs, shapes = zip(stacked(N_HEADS), stacked(N_HEADS // 2), stacked(N_HEADS), stacked(2), stacked(N_HEADS))
    return pl.pallas_call(
        functools.partial(_mla_pre_kernel, scale=scale),
        grid=(t // tm,),
        in_specs=[
            pl.BlockSpec((tm, D_MODEL), lambda i: (i, 0)),
            pl.BlockSpec((tm, 1), lambda i: (i, 0)),
            _const_spec(g.shape), _layer0_spec(w_in), _const_spec(gcq.shape), _const_spec(gckv.shape),
            _layer0_spec(w_uq), _layer0_spec(w_ukv), _const_spec(invf.shape),
        ],
        out_specs=list(specs),
        out_shape=list(shapes),
        scratch_shapes=[
            pltpu.VMEM((D_MODEL, Q_LORA_RANK + KV_LORA_RANK + LANES), BF16),
            pltpu.VMEM(w_uq.shape[1:], BF16),
            pltpu.VMEM(w_ukv.shape[1:], BF16),
        ],
        compiler_params=_params(("arbitrary",)),
        name="mla_pre",
    )(x2, pos, g, w_in, gcq, gckv, w_uq, w_ukv, invf)


def _attn_kernel(*refs, cast_periods):
    n_cast = len(cast_periods)
    qn_ref, qr_ref, kn_ref, kr_ref, v_ref = refs[:5]
    o_ref = refs[5 + n_cast]
    m_ref, acc_ref = refs[-2:]
    step_id = pl.program_id(0) * pl.num_programs(1) + pl.program_id(1)
    _cast_side_work(step_id, refs[5:5 + n_cast], refs[6 + n_cast:-2], cast_periods)

    seq = qn_ref.shape[0]
    ta = ATTN_TILE
    groups = ta // LANES
    row_chunk = lax.broadcasted_iota(jnp.int32, (ta, ta), 0) // CHUNK
    col_chunk = lax.broadcasted_iota(jnp.int32, (ta, ta), 1) // CHUNK
    diag_mask = col_chunk <= row_chunk
    ones = jnp.ones((ta, LANES), BF16)

    def tile_slice(i):
        return pl.ds(i * ta, ta)

    def kv_tile(j):
        ks = tile_slice(j)
        k = jnp.concatenate([kn_ref[ks, :], kr_ref[ks, :]], axis=1)
        v_ext = jnp.concatenate([v_ref[ks, :], ones], axis=1)
        return k, v_ext

    def step(c, qs, k, v_ext, masked):
        q = jnp.concatenate([qn_ref[qs, :], qr_ref[qs, :]], axis=1)
        s = lax.dot_general(q, k, (((1,), (1,)), ((), ())), preferred_element_type=F32)
        if masked:
            s = jnp.where(diag_mask, s, -1e30)
        parts = [s[:, g * LANES:(g + 1) * LANES] for g in range(groups)]
        m_prev = m_ref[c]
        m_new = jnp.maximum(m_prev, jnp.max(functools.reduce(jnp.maximum, parts), axis=1, keepdims=True))
        alpha = jnp.exp2(m_prev - m_new)
        p = jnp.concatenate([jnp.exp2(part - m_new).astype(BF16) for part in parts], axis=1)
        acc_ref[c] = acc_ref[c] * jnp.concatenate([alpha, alpha], axis=1) + _dot(p, v_ext)
        m_ref[c] = m_new

    def finish(c, qs):
        acc = acc_ref[c]
        o_ref[qs, :] = (acc[:, :V_HEAD_DIM] / acc[:, V_HEAD_DIM:]).astype(o_ref.dtype)

    for group in range(seq // (ATTN_CHAINS * ta)):
        first = group * ATTN_CHAINS
        m_ref[...] = jnp.full(m_ref.shape, -1e30, F32)
        acc_ref[...] = jnp.zeros(acc_ref.shape, F32)
        for j in range(first + ATTN_CHAINS):
            k, v_ext = kv_tile(j)
            for c in range(max(0, j - first), ATTN_CHAINS):
                step(c, tile_slice(first + c), k, v_ext, masked=(j == first + c))
            if j >= first:
                finish(j - first, tile_slice(j))


def _attention(qn, qr, kn, kr, v, bsz, seq, casts):
    t = bsz * seq
    ta = ATTN_TILE
    head_spec = pl.BlockSpec((None, seq, LANES), lambda b, h: (h, b, 0))
    cast_specs = [_cast_specs(w, layer, bsz * N_HEADS, lambda b, h: b * N_HEADS + h) for w, layer in casts]
    return pl.pallas_call(
        functools.partial(_attn_kernel, cast_periods=tuple(c[3] for c in cast_specs)),
        grid=(bsz, N_HEADS),
        in_specs=[head_spec,
                  pl.BlockSpec((None, seq, LANES), lambda b, h: (h // 2, b, 0)),
                  head_spec,
                  pl.BlockSpec((None, seq, LANES), lambda b, h: (h % 2, b, 0)),
                  head_spec] + [c[0] for c in cast_specs],
        out_specs=[head_spec] + [c[1] for c in cast_specs],
        out_shape=[jax.ShapeDtypeStruct((N_HEADS, t, LANES), BF16)] + [c[2] for c in cast_specs],
        scratch_shapes=[pltpu.VMEM((ATTN_CHAINS, ta, LANES), F32),
                        pltpu.VMEM((ATTN_CHAINS, ta, 2 * LANES), F32)],
        compiler_params=_params(("arbitrary", "arbitrary")),
        name="attention",
    )(qn, qr, kn, kr, v, *[w for w, _ in casts])


def _row_halves(tm):
    return [pl.ds(r * (tm // 2), tm // 2) for r in range(2)]


def _swiglu_halves(h1s, g, wg_ref, wu_ref, wd_ref, act_ref, halves):
    d_ff = wg_ref.shape[1]
    hns = [_rms(h1, g).astype(BF16) for h1 in h1s]
    for c in range(d_ff // FF_CHUNK):
        cs = slice(c * FF_CHUNK, (c + 1) * FF_CHUNK)
        for rs, hn in zip(halves, hns):
            gate = _dot(hn, wg_ref[:, cs])
            up = _dot(hn, wu_ref[:, cs])
            act_ref[rs, cs] = (jax.nn.silu(gate) * up).astype(BF16)
    return [_dot(act_ref[rs, :], wd_ref[...]) for rs in halves]


def _wo_ffn_kernel(a_ref, h_ref, wo_ref, g_ref, wg_ref, wu_ref, wd_ref, out_ref, act_ref):
    halves = _row_halves(h_ref.shape[0])
    for rs in halves:
        a = jnp.concatenate([a_ref[h, rs, :] for h in range(N_HEADS)], axis=1)
        out_ref[rs, :] = h_ref[rs, :] + _dot(a, wo_ref[...])
    ys = _swiglu_halves([out_ref[rs, :] for rs in halves], g_ref[0:1, :], wg_ref, wu_ref, wd_ref, act_ref, halves)
    for rs, y in zip(halves, ys):
        out_ref[rs, :] = out_ref[rs, :] + y


def _wo_ffn(attn, h, wo, g, wg, wu, wd):
    t = h.shape[0]
    tm = TOKEN_TILE
    return pl.pallas_call(
        _wo_ffn_kernel,
        grid=(t // tm,),
        in_specs=[
            pl.BlockSpec((N_HEADS, tm, LANES), lambda i: (0, i, 0)),
            pl.BlockSpec((tm, D_MODEL), lambda i: (i, 0)),
            _const_spec(wo.shape), _const_spec(g.shape), _const_spec(wg.shape),
            _const_spec(wu.shape), _const_spec(wd.shape),
        ],
        out_specs=pl.BlockSpec((tm, D_MODEL), lambda i: (i, 0)),
        out_shape=jax.ShapeDtypeStruct((t, D_MODEL), F32),
        scratch_shapes=[pltpu.VMEM((tm, wg.shape[1]), BF16)],
        compiler_params=_params(("parallel",)),
        name="wo_ffn",
    )(attn, h, wo, g, wg, wu, wd)


def _conv_ffn_kernel(h_ref, gc_ref, win_ref, cw_ref, wout_ref, gf_ref, wg_ref, wu_ref, wd_ref,
                     gfin_ref, out_ref, ubuf_ref, hist_ref, act_ref, *, tiles_per_seq):
    tm = h_ref.shape[0]
    pad = SUBLANES

    @pl.when(pl.program_id(0) % tiles_per_seq == 0)
    def _():
        hist_ref[...] = jnp.zeros(hist_ref.shape, F32)

    hn = _rms(h_ref[...], gc_ref[...]).astype(BF16)
    for c in range(D_MODEL // CONV_CHUNK):
        cs = slice(c * CONV_CHUNK, (c + 1) * CONV_CHUNK)
        b_gate = _dot(hn, win_ref[:, c * CONV_CHUNK:(c + 1) * CONV_CHUNK])
        c_gate = _dot(hn, win_ref[:, D_MODEL + c * CONV_CHUNK:D_MODEL + (c + 1) * CONV_CHUNK])
        xp = _dot(hn, win_ref[:, 2 * D_MODEL + c * CONV_CHUNK:2 * D_MODEL + (c + 1) * CONV_CHUNK])
        u = c_gate * xp
        ubuf = ubuf_ref.at[c % 2]
        ubuf[0:pad, :] = hist_ref[:, cs]
        ubuf[pad:pad + tm, :] = u
        hist_ref[:, cs] = ubuf[tm:tm + pad, :]
        u_conv = (cw_ref[0:1, cs] * ubuf[pad - 2:pad - 2 + tm, :]
                  + cw_ref[1:2, cs] * ubuf[pad - 1:pad - 1 + tm, :]
                  + cw_ref[2:3, cs] * u)
        act_ref[:, cs] = (b_gate * u_conv).astype(BF16)

    whole = [pl.ds(0, tm)]
    out_ref[...] = h_ref[...] + _dot(act_ref[:, :D_MODEL], wout_ref[...])
    y, = _swiglu_halves([out_ref[...]], gf_ref[1:2, :], wg_ref, wu_ref, wd_ref, act_ref, whole)
    out_ref[...] = _rms(out_ref[...] + y, gfin_ref[...])


def _conv_ffn(h, gc, win, cw, wout, gf, wg, wu, wd, gfin, seq):
    t = h.shape[0]
    tm = TOKEN_TILE
    return pl.pallas_call(
        functools.partial(_conv_ffn_kernel, tiles_per_seq=seq // tm),
        grid=(t // tm,),
        in_specs=[
            pl.BlockSpec((tm, D_MODEL), lambda i: (i, 0)),
            _const_spec(gc.shape), _const_spec(win.shape), _layer0_spec(cw),
            _const_spec(wout.shape), _const_spec(gf.shape), _const_spec(wg.shape),
            _const_spec(wu.shape), _const_spec(wd.shape), _const_spec(gfin.shape),
        ],
        out_specs=pl.BlockSpec((tm, D_MODEL), lambda i: (i, 0)),
        out_shape=jax.ShapeDtypeStruct((t, D_MODEL), F32),
        scratch_shapes=[
            pltpu.VMEM((2, tm + SUBLANES, CONV_CHUNK), F32),
            pltpu.VMEM((SUBLANES, D_MODEL), F32),
            pltpu.VMEM((tm, wg.shape[1]), BF16),
        ],
        compiler_params=_params(("arbitrary",)),
        name="conv_ffn",
    )(h, gc, win, cw, wout, gf, wg, wu, wd, gfin)


def _rope_inv_freq_lanes():
    inv = 1.0 / (ROPE_THETA ** (np.arange(0, QK_ROPE_DIM, 2, dtype=np.float32) / np.float32(QK_ROPE_DIM)))
    inv = inv.astype(np.float32)
    return jnp.asarray(np.tile(inv, LANES // (QK_ROPE_DIM // 2))[None, :])


def kernel(x, positions, mla_norm, mla_w_in, mla_g_cq, mla_g_ckv, mla_w_uq, mla_w_ukv, mla_w_o,
           conv_norm, conv_w_in, conv_w, conv_w_out, ffn_norm, ffn_w_gate, ffn_w_up, ffn_w_down,
           final_norm):
    bsz, seq, d = x.shape
    t = bsz * seq
    assert d == D_MODEL and seq % TOKEN_TILE == 0 and seq % (ATTN_CHAINS * ATTN_TILE) == 0 and ATTN_TILE % CHUNK == 0
    x2 = x.reshape(t, d)
    qn, qr, kn, kr, v = _mla_pre(
        x2, positions.reshape(t, 1), mla_norm, mla_w_in, mla_g_cq, mla_g_ckv, mla_w_uq, mla_w_ukv,
        _rope_inv_freq_lanes())
    later_weights = [(mla_w_o, 0), (ffn_w_gate, 0), (ffn_w_up, 0), (ffn_w_down, 0),
                     (conv_w_in, 0), (conv_w_out, 0), (ffn_w_gate, 1), (ffn_w_up, 1), (ffn_w_down, 1)]
    (attn, wo_b, wg0_b, wu0_b, wd0_b, cwin_b, cwout_b, wg1_b, wu1_b, wd1_b) = _attention(
        qn, qr, kn, kr, v, bsz, seq, later_weights)
    h = _wo_ffn(attn, x2, wo_b, ffn_norm, wg0_b, wu0_b, wd0_b)
    out = _conv_ffn(h, conv_norm, cwin_b, conv_w, cwout_b, ffn_norm,
                    wg1_b, wu1_b, wd1_b, final_norm[None, :], seq)
    return out.reshape(bsz, seq, d)
```

```python
import functools
import math

import jax
import jax.numpy as jnp
import numpy as np
from jax import lax
from jax.experimental import pallas as pl
from jax.experimental.pallas import tpu as pltpu

D_MODEL = 1024
CHUNK = 64
N_HEADS = 8
QK_NOPE_DIM = 128
QK_ROPE_DIM = 64
V_HEAD_DIM = 128
Q_LORA_RANK = 512
KV_LORA_RANK = 256
ROPE_THETA = 10000.0
CONV_WIDTH = 3
RMS_EPS = 1e-6

LANES = 128
SUBLANES = 8
BF16_SUBLANES = 16
MXU_COLS = 256
VMEM_LIMIT_BYTES = 56 * 1024 * 1024

TOKEN_TILE = 1024
ATTN_TILE = 256
ATTN_CHAINS = 2
FF_CHUNK = MXU_COLS
CONV_CHUNK = MXU_COLS

BF16 = jnp.bfloat16
F32 = jnp.float32


def _dot(a, b):
    return jnp.dot(a, b, preferred_element_type=F32)


def _rms(x, g):
    return x * lax.rsqrt(jnp.mean(x * x, axis=-1, keepdims=True) + RMS_EPS) * g


def _const_spec(shape):
    return pl.BlockSpec(shape, lambda *_: (0,) * len(shape), pipeline_mode=pl.Buffered(1))


def _layer0_spec(w):
    return pl.BlockSpec((None,) + w.shape[1:], lambda *_: (0,) * w.ndim, pipeline_mode=pl.Buffered(1))


def _params(semantics):
    return pltpu.CompilerParams(dimension_semantics=semantics, vmem_limit_bytes=VMEM_LIMIT_BYTES)


def _cast_specs(w, layer, steps, step_of):
    _, rows, cols = w.shape
    out_shape = jax.ShapeDtypeStruct((rows, cols), BF16)
    if rows % (steps * BF16_SUBLANES) == 0:
        rb = rows // steps
        return (pl.BlockSpec((None, rb, cols), lambda *g: (layer, step_of(*g), 0)),
                pl.BlockSpec((rb, cols), lambda *g: (step_of(*g), 0)), out_shape, 1)
    per = steps // (cols // LANES)
    return (pl.BlockSpec((None, rows, LANES), lambda *g: (layer, 0, step_of(*g) // per)),
            pl.BlockSpec((rows, LANES), lambda *g: (0, step_of(*g) // per)), out_shape, per)


def _cast_side_work(step, cast_in_refs, cast_out_refs, periods):
    for src, dst, period in zip(cast_in_refs, cast_out_refs, periods):
        if period == 1:
            dst[...] = src[...].astype(BF16)
        else:
            @pl.when(step % period == 0)
            def _(src=src, dst=dst):
                dst[...] = src[...].astype(BF16)


def _mla_pre_kernel(x_ref, pos_ref, g_ref, w_in_ref, gcq_ref, gckv_ref, w_uq_ref, w_ukv_ref, invf_ref,
                    qn_ref, qr_ref, kn_ref, kr_ref, v_ref, w_in_b, w_uq_b, w_ukv_b, *, scale):
    tm = x_ref.shape[0]
    kr_col = Q_LORA_RANK + KV_LORA_RANK

    @pl.when(pl.program_id(0) == 0)
    def _():
        w_in_b[:, :kr_col] = w_in_ref[:, :kr_col].astype(BF16)
        w_in_b[:, kr_col:] = jnp.zeros((D_MODEL, LANES), BF16)
        w_in_b[:, kr_col:kr_col + QK_ROPE_DIM] = w_in_ref[:, kr_col:].astype(BF16)
        pair = 2 * (QK_NOPE_DIM + QK_ROPE_DIM)
        dst = lax.broadcasted_iota(jnp.int32, (pair, pair), 1)
        src = jnp.where(dst < QK_NOPE_DIM, dst,
                        jnp.where(dst < 2 * QK_NOPE_DIM, dst + QK_ROPE_DIM,
                                  jnp.where(dst < 2 * QK_NOPE_DIM + QK_ROPE_DIM, dst - QK_NOPE_DIM, dst)))
        select = (lax.broadcasted_iota(jnp.int32, (pair, pair), 0) == src).astype(BF16)
        n_nope = N_HEADS * QK_NOPE_DIM
        for p in range(N_HEADS // 2):
            wp = _dot(w_uq_ref[:, p * pair:(p + 1) * pair].astype(BF16), select).astype(BF16)
            w_uq_b[:, 2 * p * QK_NOPE_DIM:2 * (p + 1) * QK_NOPE_DIM] = wp[:, :2 * QK_NOPE_DIM]
            w_uq_b[:, n_nope + p * LANES:n_nope + (p + 1) * LANES] = wp[:, 2 * QK_NOPE_DIM:]
        w_ukv_b[...] = w_ukv_ref[...].astype(BF16)

    xn = _rms(x_ref[...], g_ref[...]).astype(BF16)
    proj = _dot(xn, w_in_b[...])
    cq = _rms(proj[:, :Q_LORA_RANK], gcq_ref[...]).astype(BF16)
    ckv = _rms(proj[:, Q_LORA_RANK:kr_col], gckv_ref[...]).astype(BF16)
    kr = proj[:, kr_col:]

    n_freq = QK_ROPE_DIM // 2
    n_grp = LANES // n_freq
    blk = tm // n_grp
    grp_c = lax.broadcasted_iota(jnp.int32, (blk, LANES), 1) // n_freq
    pos = pos_ref[...].astype(F32)
    pos_c = pos[(n_grp - 1) * blk:, :]
    for a in reversed(range(n_grp - 1)):
        pos_c = jnp.where(grp_c == a, pos[a * blk:(a + 1) * blk, :], pos_c)
    ang_c = pos_c * invf_ref[...]
    cos_c = jnp.cos(ang_c)
    sin_c = jnp.sin(ang_c)
    cos_r = [cos_c] + [pltpu.roll(cos_c, n_freq * k, 1) for k in range(1, n_grp)]
    sin_r = [sin_c] + [pltpu.roll(sin_c, n_freq * k, 1) for k in range(1, n_grp)]
    nsin_r = [-t for t in sin_r]

    def expand(a, even_tabs, odd_tabs):
        out = None
        for g in reversed(range(n_grp)):
            tab = (even_tabs if g % 2 == 0 else odd_tabs)[(g - a) % n_grp]
            out = tab if out is None else jnp.where(grp_c == g, tab, out)
        return out

    cos = jnp.concatenate([expand(a, cos_r, cos_r) for a in range(n_grp)], axis=0)
    sin_signed = jnp.concatenate([expand(a, nsin_r, sin_r) for a in range(n_grp)], axis=0)
    lane = lax.broadcasted_iota(jnp.int32, (tm, LANES), 1)
    first_half = (lane % QK_ROPE_DIM) < n_freq

    def rope(t):
        partner = jnp.where(first_half,
                            pltpu.roll(t, LANES - n_freq, 1),
                            pltpu.roll(t, n_freq, 1))
        return t * cos + partner * sin_signed

    kr_rot = rope(kr)
    kr_ref[0] = kr_rot.astype(BF16)
    kr_ref[1] = pltpu.roll(kr_rot, QK_ROPE_DIM, 1).astype(BF16)

    q = _dot(cq, w_uq_b[...])
    for h in range(N_HEADS):
        qn_ref[h] = (q[:, h * LANES:(h + 1) * LANES] * scale).astype(BF16)
    for p in range(N_HEADS // 2):
        pair_cols = slice((N_HEADS + p) * LANES, (N_HEADS + p + 1) * LANES)
        qr_ref[p] = (rope(q[:, pair_cols]) * scale).astype(BF16)

    kv = _dot(ckv, w_ukv_b[...])
    for h in range(N_HEADS):
        kn_ref[h] = kv[:, 2 * h * LANES:(2 * h + 1) * LANES].astype(BF16)
        v_ref[h] = kv[:, (2 * h + 1) * LANES:(2 * h + 2) * LANES].astype(BF16)


def _mla_pre(x2, pos, g, w_in, gcq, gckv, w_uq, w_ukv, invf):
    t = x2.shape[0]
    tm = TOKEN_TILE
    scale = math.log2(math.e) / math.sqrt(QK_NOPE_DIM + QK_ROPE_DIM)

    def stacked(n):
        return (pl.BlockSpec((n, tm, LANES), lambda i: (0, i, 0)), jax.ShapeDtypeStruct((n, t, LANES), BF16))

    specs, shapes = zip(stacked(N_HEADS), stacked(N_HEADS // 2), stacked(N_HEADS), stacked(2), stacked(N_HEADS))
    return pl.pallas_call(
        functools.partial(_mla_pre_kernel, scale=scale),
        grid=(t // tm,),
        in_specs=[
            pl.BlockSpec((tm, D_MODEL), lambda i: (i, 0)),
            pl.BlockSpec((tm, 1), lambda i: (i, 0)),
            _const_spec(g.shape), _layer0_spec(w_in), _const_spec(gcq.shape), _const_spec(gckv.shape),
            _layer0_spec(w_uq), _layer0_spec(w_ukv), _const_spec(invf.shape),
        ],
        out_specs=list(specs),
        out_shape=list(shapes),
        scratch_shapes=[
            pltpu.VMEM((D_MODEL, Q_LORA_RANK + KV_LORA_RANK + LANES), BF16),
            pltpu.VMEM(w_uq.shape[1:], BF16),
            pltpu.VMEM(w_ukv.shape[1:], BF16),
        ],
        compiler_params=_params(("arbitrary",)),
        name="mla_pre",
    )(x2, pos, g, w_in, gcq, gckv, w_uq, w_ukv, invf)


def _attn_kernel(*refs, cast_periods):
    n_cast = len(cast_periods)
    qn_ref, qr_ref, kn_ref, kr_ref, v_ref = refs[:5]
    o_ref = refs[5 + n_cast]
    m_ref, acc_ref = refs[-2:]
    step_id = pl.program_id(0) * pl.num_programs(1) + pl.program_id(1)
    _cast_side_work(step_id, refs[5:5 + n_cast], refs[6 + n_cast:-2], cast_periods)

    seq = qn_ref.shape[0]
    ta = ATTN_TILE
    groups = ta // LANES
    row_chunk = lax.broadcasted_iota(jnp.int32, (ta, ta), 0) // CHUNK
    col_chunk = lax.broadcasted_iota(jnp.int32, (ta, ta), 1) // CHUNK
    diag_mask = col_chunk <= row_chunk
    ones = jnp.ones((ta, LANES), BF16)

    def tile_slice(i):
        return pl.ds(i * ta, ta)

    def kv_tile(j):
        ks = tile_slice(j)
        k = jnp.concatenate([kn_ref[ks, :], kr_ref[ks, :]], axis=1)
        v_ext = jnp.concatenate([v_ref[ks, :], ones], axis=1)
        return k, v_ext

    def step(c, qs, k, v_ext, masked):
        q = jnp.concatenate([qn_ref[qs, :], qr_ref[qs, :]], axis=1)
        s = lax.dot_general(q, k, (((1,), (1,)), ((), ())), preferred_element_type=F32)
        if masked:
            s = jnp.where(diag_mask, s, -1e30)
        parts = [s[:, g * LANES:(g + 1) * LANES] for g in range(groups)]
        m_prev = m_ref[c]
        m_new = jnp.maximum(m_prev, jnp.max(functools.reduce(jnp.maximum, parts), axis=1, keepdims=True))
        alpha = jnp.exp2(m_prev - m_new)
        p = jnp.concatenate([jnp.exp2(part - m_new).astype(BF16) for part in parts], axis=1)
        acc_ref[c] = acc_ref[c] * jnp.concatenate([alpha, alpha], axis=1) + _dot(p, v_ext)
        m_ref[c] = m_new

    def finish(c, qs):
        acc = acc_ref[c]
        o_ref[qs, :] = (acc[:, :V_HEAD_DIM] / acc[:, V_HEAD_DIM:]).astype(o_ref.dtype)

    for group in range(seq // (ATTN_CHAINS * ta)):
        first = group * ATTN_CHAINS
        m_ref[...] = jnp.full(m_ref.shape, -1e30, F32)
        acc_ref[...] = jnp.zeros(acc_ref.shape, F32)
        for j in range(first + ATTN_CHAINS):
            k, v_ext = kv_tile(j)
            for c in range(max(0, j - first), ATTN_CHAINS):
                step(c, tile_slice(first + c), k, v_ext, masked=(j == first + c))
            if j >= first:
                finish(j - first, tile_slice(j))


def _attention(qn, qr, kn, kr, v, bsz, seq, casts):
    t = bsz * seq
    ta = ATTN_TILE
    head_spec = pl.BlockSpec((None, seq, LANES), lambda b, h: (h, b, 0))
    cast_specs = [_cast_specs(w, layer, bsz * N_HEADS, lambda b, h: b * N_HEADS + h) for w, layer in casts]
    return pl.pallas_call(
        functools.partial(_attn_kernel, cast_periods=tuple(c[3] for c in cast_specs)),
        grid=(bsz, N_HEADS),
        in_specs=[head_spec,
                  pl.BlockSpec((None, seq, LANES), lambda b, h: (h // 2, b, 0)),
                  head_spec,
                  pl.BlockSpec((None, seq, LANES), lambda b, h: (h % 2, b, 0)),
                  head_spec] + [c[0] for c in cast_specs],
        out_specs=[head_spec] + [c[1] for c in cast_specs],
        out_shape=[jax.ShapeDtypeStruct((N_HEADS, t, LANES), BF16)] + [c[2] for c in cast_specs],
        scratch_shapes=[pltpu.VMEM((ATTN_CHAINS, ta, LANES), F32),
                        pltpu.VMEM((ATTN_CHAINS, ta, 2 * LANES), F32)],
        compiler_params=_params(("arbitrary", "arbitrary")),
        name="attention",
    )(qn, qr, kn, kr, v, *[w for w, _ in casts])


def _row_halves(tm):
    return [pl.ds(r * (tm // 2), tm // 2) for r in range(2)]


def _swiglu_halves(h1s, g, wg_ref, wu_ref, wd_ref, act_ref, halves):
    d_ff = wg_ref.shape[1]
    hns = [_rms(h1, g).astype(BF16) for h1 in h1s]
    for c in range(d_ff // FF_CHUNK):
        cs = slice(c * FF_CHUNK, (c + 1) * FF_CHUNK)
        for rs, hn in zip(halves, hns):
            gate = _dot(hn, wg_ref[:, cs])
            up = _dot(hn, wu_ref[:, cs])
            act_ref[rs, cs] = (jax.nn.silu(gate) * up).astype(BF16)
    return [_dot(act_ref[rs, :], wd_ref[...]) for rs in halves]


def _wo_ffn_kernel(a_ref, h_ref, wo_ref, g_ref, wg_ref, wu_ref, wd_ref, out_ref, act_ref):
    halves = _row_halves(h_ref.shape[0])
    for rs in halves:
        a = jnp.concatenate([a_ref[h, rs, :] for h in range(N_HEADS)], axis=1)
        out_ref[rs, :] = h_ref[rs, :] + _dot(a, wo_ref[...])
    ys = _swiglu_halves([out_ref[rs, :] for rs in halves], g_ref[0:1, :], wg_ref, wu_ref, wd_ref, act_ref, halves)
    for rs, y in zip(halves, ys):
        out_ref[rs, :] = out_ref[rs, :] + y


def _wo_ffn(attn, h, wo, g, wg, wu, wd):
    t = h.shape[0]
    tm = TOKEN_TILE
    return pl.pallas_call(
        _wo_ffn_kernel,
        grid=(t // tm,),
        in_specs=[
            pl.BlockSpec((N_HEADS, tm, LANES), lambda i: (0, i, 0)),
            pl.BlockSpec((tm, D_MODEL), lambda i: (i, 0)),
            _const_spec(wo.shape), _const_spec(g.shape), _const_spec(wg.shape),
            _const_spec(wu.shape), _const_spec(wd.shape),
        ],
        out_specs=pl.BlockSpec((tm, D_MODEL), lambda i: (i, 0)),
        out_shape=jax.ShapeDtypeStruct((t, D_MODEL), F32),
        scratch_shapes=[pltpu.VMEM((tm, wg.shape[1]), BF16)],
        compiler_params=_params(("parallel",)),
        name="wo_ffn",
    )(attn, h, wo, g, wg, wu, wd)


def _conv_ffn_kernel(h_ref, gc_ref, win_ref, cw_ref, wout_ref, gf_ref, wg_ref, wu_ref, wd_ref,
                     gfin_ref, out_ref, ubuf_ref, hist_ref, act_ref, *, tiles_per_seq):
    tm = h_ref.shape[0]
    pad = SUBLANES

    @pl.when(pl.program_id(0) % tiles_per_seq == 0)
    def _():
        hist_ref[...] = jnp.zeros(hist_ref.shape, F32)

    hn = _rms(h_ref[...], gc_ref[...]).astype(BF16)
    for c in range(D_MODEL // CONV_CHUNK):
        cs = slice(c * CONV_CHUNK, (c + 1) * CONV_CHUNK)
        b_gate = _dot(hn, win_ref[:, c * CONV_CHUNK:(c + 1) * CONV_CHUNK])
        c_gate = _dot(hn, win_ref[:, D_MODEL + c * CONV_CHUNK:D_MODEL + (c + 1) * CONV_CHUNK])
        xp = _dot(hn, win_ref[:, 2 * D_MODEL + c * CONV_CHUNK:2 * D_MODEL + (c + 1) * CONV_CHUNK])
        u = c_gate * xp
        ubuf = ubuf_ref.at[c % 2]
        ubuf[0:pad, :] = hist_ref[:, cs]
        ubuf[pad:pad + tm, :] = u
        hist_ref[:, cs] = ubuf[tm:tm + pad, :]
        u_conv = (cw_ref[0:1, cs] * ubuf[pad - 2:pad - 2 + tm, :]
                  + cw_ref[1:2, cs] * ubuf[pad - 1:pad - 1 + tm, :]
                  + cw_ref[2:3, cs] * u)
        act_ref[:, cs] = (b_gate * u_conv).astype(BF16)

    halves = _row_halves(tm)
    out_ref[...] = h_ref[...] + _dot(act_ref[:, :D_MODEL], wout_ref[...])
    ys = _swiglu_halves([out_ref[rs, :] for rs in halves], gf_ref[1:2, :], wg_ref, wu_ref, wd_ref, act_ref, halves)
    out_ref[...] = _rms(out_ref[...] + jnp.concatenate(ys, axis=0), gfin_ref[...])


def _conv_ffn(h, gc, win, cw, wout, gf, wg, wu, wd, gfin, seq):
    t = h.shape[0]
    tm = TOKEN_TILE
    return pl.pallas_call(
        functools.partial(_conv_ffn_kernel, tiles_per_seq=seq // tm),
        grid=(t // tm,),
        in_specs=[
            pl.BlockSpec((tm, D_MODEL), lambda i: (i, 0)),
            _const_spec(gc.shape), _const_spec(win.shape), _layer0_spec(cw),
            _const_spec(wout.shape), _const_spec(gf.shape), _const_spec(wg.shape),
            _const_spec(wu.shape), _const_spec(wd.shape), _const_spec(gfin.shape),
        ],
        out_specs=pl.BlockSpec((tm, D_MODEL), lambda i: (i, 0)),
        out_shape=jax.ShapeDtypeStruct((t, D_MODEL), F32),
        scratch_shapes=[
            pltpu.VMEM((2, tm + SUBLANES, CONV_CHUNK), F32),
            pltpu.VMEM((SUBLANES, D_MODEL), F32),
            pltpu.VMEM((tm, wg.shape[1]), BF16),
        ],
        compiler_params=_params(("arbitrary",)),
        name="conv_ffn",
    )(h, gc, win, cw, wout, gf, wg, wu, wd, gfin)


def _rope_inv_freq_lanes():
    inv = 1.0 / (ROPE_THETA ** (np.arange(0, QK_ROPE_DIM, 2, dtype=np.float32) / np.float32(QK_ROPE_DIM)))
    inv = inv.astype(np.float32)
    return jnp.asarray(np.tile(inv, LANES // (QK_ROPE_DIM // 2))[None, :])


def kernel(x, positions, mla_norm, mla_w_in, mla_g_cq, mla_g_ckv, mla_w_uq, mla_w_ukv, mla_w_o,
           conv_norm, conv_w_in, conv_w, conv_w_out, ffn_norm, ffn_w_gate, ffn_w_up, ffn_w_down,
           final_norm):
    bsz, seq, d = x.shape
    t = bsz * seq
    assert d == D_MODEL and seq % TOKEN_TILE == 0 and seq % (ATTN_CHAINS * ATTN_TILE) == 0 and ATTN_TILE % CHUNK == 0
    x2 = x.reshape(t, d)
    qn, qr, kn, kr, v = _mla_pre(
        x2, positions.reshape(t, 1), mla_norm, mla_w_in, mla_g_cq, mla_g_ckv, mla_w_uq, mla_w_ukv,
        _rope_inv_freq_lanes())
    later_weights = [(mla_w_o, 0), (ffn_w_gate, 0), (ffn_w_up, 0), (ffn_w_down, 0),
                     (conv_w_in, 0), (conv_w_out, 0), (ffn_w_gate, 1), (ffn_w_up, 1), (ffn_w_down, 1)]
    (attn, wo_b, wg0_b, wu0_b, wd0_b, cwin_b, cwout_b, wg1_b, wu1_b, wd1_b) = _attention(
        qn, qr, kn, kr, v, bsz, seq, later_weights)
    h = _wo_ffn(attn, x2, wo_b, ffn_norm, wg0_b, wu0_b, wd0_b)
    out = _conv_ffn(h, conv_norm, cwin_b, conv_w, cwout_b, ffn_norm,
                    wg1_b, wu1_b, wd1_b, final_norm[None, :], seq)
    return out.reshape(bsz, seq, d)
```

```python
import functools
import math

import jax
import jax.numpy as jnp
import numpy as np
from jax import lax
from jax.experimental import pallas as pl
from jax.experimental.pallas import tpu as pltpu

D_MODEL = 1024
CHUNK = 64
N_HEADS = 8
QK_NOPE_DIM = 128
QK_ROPE_DIM = 64
V_HEAD_DIM = 128
Q_LORA_RANK = 512
KV_LORA_RANK = 256
ROPE_THETA = 10000.0
CONV_WIDTH = 3
RMS_EPS = 1e-6

LANES = 128
SUBLANES = 8
BF16_SUBLANES = 16
MXU_COLS = 256
VMEM_LIMIT_BYTES = 56 * 1024 * 1024

TOKEN_TILE = 1024
ATTN_TILE = 256
ATTN_CHAINS = 2
FF_CHUNK = MXU_COLS
CONV_CHUNK = MXU_COLS

BF16 = jnp.bfloat16
F32 = jnp.float32


def _dot(a, b):
    return jnp.dot(a, b, preferred_element_type=F32)


def _rms(x, g):
    return x * lax.rsqrt(jnp.mean(x * x, axis=-1, keepdims=True) + RMS_EPS) * g


def _const_spec(shape):
    return pl.BlockSpec(shape, lambda *_: (0,) * len(shape), pipeline_mode=pl.Buffered(1))


def _layer0_spec(w):
    return pl.BlockSpec((None,) + w.shape[1:], lambda *_: (0,) * w.ndim, pipeline_mode=pl.Buffered(1))


def _params(semantics):
    return pltpu.CompilerParams(dimension_semantics=semantics, vmem_limit_bytes=VMEM_LIMIT_BYTES)


def _cast_specs(w, layer, steps, step_of):
    _, rows, cols = w.shape
    out_shape = jax.ShapeDtypeStruct((rows, cols), BF16)
    if rows % (steps * BF16_SUBLANES) == 0:
        rb = rows // steps
        return (pl.BlockSpec((None, rb, cols), lambda *g: (layer, step_of(*g), 0)),
                pl.BlockSpec((rb, cols), lambda *g: (step_of(*g), 0)), out_shape, 1)
    per = steps // (cols // LANES)
    return (pl.BlockSpec((None, rows, LANES), lambda *g: (layer, 0, step_of(*g) // per)),
            pl.BlockSpec((rows, LANES), lambda *g: (0, step_of(*g) // per)), out_shape, per)


def _cast_side_work(step, cast_in_refs, cast_out_refs, periods):
    for src, dst, period in zip(cast_in_refs, cast_out_refs, periods):
        if period == 1:
            dst[...] = src[...].astype(BF16)
        else:
            @pl.when(step % period == 0)
            def _(src=src, dst=dst):
                dst[...] = src[...].astype(BF16)


def _mla_pre_kernel(x_ref, pos_ref, g_ref, w_in_ref, gcq_ref, gckv_ref, w_uq_ref, w_ukv_ref, invf_ref,
                    qn_ref, qr_ref, kn_ref, kr_ref, v_ref, w_in_b, w_uq_b, w_ukv_b, *, scale):
    tm = x_ref.shape[0]
    kr_col = Q_LORA_RANK + KV_LORA_RANK

    @pl.when(pl.program_id(0) == 0)
    def _():
        w_in_b[:, :kr_col] = w_in_ref[:, :kr_col].astype(BF16)
        w_in_b[:, kr_col:] = jnp.zeros((D_MODEL, LANES), BF16)
        w_in_b[:, kr_col:kr_col + QK_ROPE_DIM] = w_in_ref[:, kr_col:].astype(BF16)
        pair = 2 * (QK_NOPE_DIM + QK_ROPE_DIM)
        dst = lax.broadcasted_iota(jnp.int32, (pair, pair), 1)
        src = jnp.where(dst < QK_NOPE_DIM, dst,
                        jnp.where(dst < 2 * QK_NOPE_DIM, dst + QK_ROPE_DIM,
                                  jnp.where(dst < 2 * QK_NOPE_DIM + QK_ROPE_DIM, dst - QK_NOPE_DIM, dst)))
        select = (lax.broadcasted_iota(jnp.int32, (pair, pair), 0) == src).astype(BF16)
        n_nope = N_HEADS * QK_NOPE_DIM
        for p in range(N_HEADS // 2):
            wp = _dot(w_uq_ref[:, p * pair:(p + 1) * pair].astype(BF16), select).astype(BF16)
            w_uq_b[:, 2 * p * QK_NOPE_DIM:2 * (p + 1) * QK_NOPE_DIM] = wp[:, :2 * QK_NOPE_DIM]
            w_uq_b[:, n_nope + p * LANES:n_nope + (p + 1) * LANES] = wp[:, 2 * QK_NOPE_DIM:]
        w_ukv_b[...] = w_ukv_ref[...].astype(BF16)

    xn = _rms(x_ref[...], g_ref[...]).astype(BF16)
    proj = _dot(xn, w_in_b[...])
    cq = _rms(proj[:, :Q_LORA_RANK], gcq_ref[...]).astype(BF16)
    ckv = _rms(proj[:, Q_LORA_RANK:kr_col], gckv_ref[...]).astype(BF16)
    kr = proj[:, kr_col:]

    n_freq = QK_ROPE_DIM // 2
    n_grp = LANES // n_freq
    blk = tm // n_grp
    grp_c = lax.broadcasted_iota(jnp.int32, (blk, LANES), 1) // n_freq
    pos = pos_ref[...].astype(F32)
    pos_c = pos[(n_grp - 1) * blk:, :]
    for a in reversed(range(n_grp - 1)):
        pos_c = jnp.where(grp_c == a, pos[a * blk:(a + 1) * blk, :], pos_c)
    ang_c = pos_c * invf_ref[...]
    cos_c = jnp.cos(ang_c)
    sin_c = jnp.sin(ang_c)
    cos_r = [cos_c] + [pltpu.roll(cos_c, n_freq * k, 1) for k in range(1, n_grp)]
    sin_r = [sin_c] + [pltpu.roll(sin_c, n_freq * k, 1) for k in range(1, n_grp)]
    nsin_r = [-t for t in sin_r]

    def expand(a, even_tabs, odd_tabs):
        out = None
        for g in reversed(range(n_grp)):
            tab = (even_tabs if g % 2 == 0 else odd_tabs)[(g - a) % n_grp]
            out = tab if out is None else jnp.where(grp_c == g, tab, out)
        return out

    cos = jnp.concatenate([expand(a, cos_r, cos_r) for a in range(n_grp)], axis=0)
    sin_signed = jnp.concatenate([expand(a, nsin_r, sin_r) for a in range(n_grp)], axis=0)
    lane = lax.broadcasted_iota(jnp.int32, (tm, LANES), 1)
    first_half = (lane % QK_ROPE_DIM) < n_freq

    def rope(t):
        partner = jnp.where(first_half,
                            pltpu.roll(t, LANES - n_freq, 1),
                            pltpu.roll(t, n_freq, 1))
        return t * cos + partner * sin_signed

    kr_rot = rope(kr)
    kr_ref[0] = kr_rot.astype(BF16)
    kr_ref[1] = pltpu.roll(kr_rot, QK_ROPE_DIM, 1).astype(BF16)

    q = _dot(cq, w_uq_b[...])
    for h in range(N_HEADS):
        qn_ref[h] = (q[:, h * LANES:(h + 1) * LANES] * scale).astype(BF16)
    for p in range(N_HEADS // 2):
        pair_cols = slice((N_HEADS + p) * LANES, (N_HEADS + p + 1) * LANES)
        qr_ref[p] = (rope(q[:, pair_cols]) * scale).astype(BF16)

    kv = _dot(ckv, w_ukv_b[...])
    for h in range(N_HEADS):
        kn_ref[h] = kv[:, 2 * h * LANES:(2 * h + 1) * LANES].astype(BF16)
        v_ref[h] = kv[:, (2 * h + 1) * LANES:(2 * h + 2) * LANES].astype(BF16)


def _mla_pre(x2, pos, g, w_in, gcq, gckv, w_uq, w_ukv, invf):
    t = x2.shape[0]
    tm = TOKEN_TILE
    scale = math.log2(math.e) / math.sqrt(QK_NOPE_DIM + QK_ROPE_DIM)

    def stacked(n):
        return (pl.BlockSpec((n, tm, LANES), lambda i: (0, i, 0)), jax.ShapeDtypeStruct((n, t, LANES), BF16))

    specs, shapes = zip(stacked(N_HEADS), stacked(N_HEADS // 2), stacked(N_HEADS), stacked(2), stacked(N_HEADS))
    return pl.pallas_call(
        functools.partial(_mla_pre_kernel, scale=scale),
        grid=(t // tm,),
        in_specs=[
            pl.BlockSpec((tm, D_MODEL), lambda i: (i, 0)),
            pl.BlockSpec((tm, 1), lambda i: (i, 0)),
            _const_spec(g.shape), _layer0_spec(w_in), _const_spec(gcq.shape), _const_spec(gckv.shape),
            _layer0_spec(w_uq), _layer0_spec(w_ukv), _const_spec(invf.shape),
        ],
        out_specs=list(specs),
        out_shape=list(shapes),
        scratch_shapes=[
            pltpu.VMEM((D_MODEL, Q_LORA_RANK + KV_LORA_RANK + LANES), BF16),
            pltpu.VMEM(w_uq.shape[1:], BF16),
            pltpu.VMEM(w_ukv.shape[1:], BF16),
        ],
        compiler_params=_params(("arbitrary",)),
        name="mla_pre",
    )(x2, pos, g, w_in, gcq, gckv, w_uq, w_ukv, invf)


def _attn_kernel(*refs, cast_periods):
    n_cast = len(cast_periods)
    qn_ref, qr_ref, kn_ref, kr_ref, v_ref = refs[:5]
    o_ref = refs[5 + n_cast]
    m_ref, acc_ref = refs[-2:]
    step_id = pl.program_id(0) * pl.num_programs(1) + pl.program_id(1)
    _cast_side_work(step_id, refs[5:5 + n_cast], refs[6 + n_cast:-2], cast_periods)

    seq = qn_ref.shape[0]
    ta = ATTN_TILE
    groups = ta // LANES
    row_chunk = lax.broadcasted_iota(jnp.int32, (ta, ta), 0) // CHUNK
    col_chunk = lax.broadcasted_iota(jnp.int32, (ta, ta), 1) // CHUNK
    diag_mask = col_chunk <= row_chunk
    ones = jnp.ones((ta, LANES), BF16)

    def tile_slice(i):
        return pl.ds(i * ta, ta)

    def kv_tile(j):
        ks = tile_slice(j)
        k = jnp.concatenate([kn_ref[ks, :], kr_ref[ks, :]], axis=1)
        v_ext = jnp.concatenate([v_ref[ks, :], ones], axis=1)
        return k, v_ext

    def step(c, qs, k, v_ext, masked):
        q = jnp.concatenate([qn_ref[qs, :], qr_ref[qs, :]], axis=1)
        s = lax.dot_general(q, k, (((1,), (1,)), ((), ())), preferred_element_type=F32)
        if masked:
            s = jnp.where(diag_mask, s, -1e30)
        parts = [s[:, g * LANES:(g + 1) * LANES] for g in range(groups)]
        m_prev = m_ref[c]
        m_new = jnp.maximum(m_prev, jnp.max(functools.reduce(jnp.maximum, parts), axis=1, keepdims=True))
        alpha = jnp.exp2(m_prev - m_new)
        p = jnp.concatenate([jnp.exp2(part - m_new).astype(BF16) for part in parts], axis=1)
        acc_ref[c] = acc_ref[c] * jnp.concatenate([alpha, alpha], axis=1) + _dot(p, v_ext)
        m_ref[c] = m_new

    def finish(c, qs):
        acc = acc_ref[c]
        o_ref[qs, :] = (acc[:, :V_HEAD_DIM] / acc[:, V_HEAD_DIM:]).astype(o_ref.dtype)

    for group in range(seq // (ATTN_CHAINS * ta)):
        first = group * ATTN_CHAINS
        m_ref[...] = jnp.full(m_ref.shape, -1e30, F32)
        acc_ref[...] = jnp.zeros(acc_ref.shape, F32)
        for j in range(first + ATTN_CHAINS):
            k, v_ext = kv_tile(j)
            for c in range(max(0, j - first), ATTN_CHAINS):
                step(c, tile_slice(first + c), k, v_ext, masked=(j == first + c))
            if j >= first:
                finish(j - first, tile_slice(j))


def _attention(qn, qr, kn, kr, v, bsz, seq, casts):
    t = bsz * seq
    ta = ATTN_TILE
    head_spec = pl.BlockSpec((None, seq, LANES), lambda b, h: (h, b, 0))
    cast_specs = [_cast_specs(w, layer, bsz * N_HEADS, lambda b, h: b * N_HEADS + h) for w, layer in casts]
    return pl.pallas_call(
        functools.partial(_attn_kernel, cast_periods=tuple(c[3] for c in cast_specs)),
        grid=(bsz, N_HEADS),
        in_specs=[head_spec,
                  pl.BlockSpec((None, seq, LANES), lambda b, h: (h // 2, b, 0)),
                  head_spec,
                  pl.BlockSpec((None, seq, LANES), lambda b, h: (h % 2, b, 0)),
                  head_spec] + [c[0] for c in cast_specs],
        out_specs=[head_spec] + [c[1] for c in cast_specs],
        out_shape=[jax.ShapeDtypeStruct((N_HEADS, t, LANES), BF16)] + [c[2] for c in cast_specs],
        scratch_shapes=[pltpu.VMEM((ATTN_CHAINS, ta, LANES), F32),
                        pltpu.VMEM((ATTN_CHAINS, ta, 2 * LANES), F32)],
        compiler_params=_params(("arbitrary", "arbitrary")),
        name="attention",
    )(qn, qr, kn, kr, v, *[w for w, _ in casts])


def _row_halves(tm):
    return [pl.ds(r * (tm // 2), tm // 2) for r in range(2)]


def _swiglu_halves(h1s, g, wg_ref, wu_ref, wd_ref, act_ref, halves):
    d_ff = wg_ref.shape[1]
    hns = [_rms(h1, g).astype(BF16) for h1 in h1s]
    for c in range(d_ff // FF_CHUNK):
        cs = slice(c * FF_CHUNK, (c + 1) * FF_CHUNK)
        for rs, hn in zip(halves, hns):
            gate = _dot(hn, wg_ref[:, cs])
            up = _dot(hn, wu_ref[:, cs])
            act_ref[rs, cs] = (jax.nn.silu(gate) * up).astype(BF16)
    return [_dot(act_ref[rs, :], wd_ref[...]) for rs in halves]


def _wo_ffn_kernel(a_ref, h_ref, wo_ref, g_ref, wg_ref, wu_ref, wd_ref, out_ref, act_ref):
    halves = _row_halves(h_ref.shape[0])
    for rs in halves:
        a = jnp.concatenate([a_ref[h, rs, :] for h in range(N_HEADS)], axis=1)
        out_ref[rs, :] = h_ref[rs, :] + _dot(a, wo_ref[...])
    ys = _swiglu_halves([out_ref[rs, :] for rs in halves], g_ref[0:1, :], wg_ref, wu_ref, wd_ref, act_ref, halves)
    for rs, y in zip(halves, ys):
        out_ref[rs, :] = out_ref[rs, :] + y


def _wo_ffn(attn, h, wo, g, wg, wu, wd):
    t = h.shape[0]
    tm = TOKEN_TILE
    return pl.pallas_call(
        _wo_ffn_kernel,
        grid=(t // tm,),
        in_specs=[
            pl.BlockSpec((N_HEADS, tm, LANES), lambda i: (0, i, 0)),
            pl.BlockSpec((tm, D_MODEL), lambda i: (i, 0)),
            _const_spec(wo.shape), _const_spec(g.shape), _const_spec(wg.shape),
            _const_spec(wu.shape), _const_spec(wd.shape),
        ],
        out_specs=pl.BlockSpec((tm, D_MODEL), lambda i: (i, 0)),
        out_shape=jax.ShapeDtypeStruct((t, D_MODEL), F32),
        scratch_shapes=[pltpu.VMEM((tm, wg.shape[1]), BF16)],
        compiler_params=_params(("parallel",)),
        name="wo_ffn",
    )(attn, h, wo, g, wg, wu, wd)


def _conv_ffn_kernel(h_ref, gc_ref, win_ref, cw_ref, wout_ref, gf_ref, wg_ref, wu_ref, wd_ref,
                     gfin_ref, out_ref, ubuf_ref, hist_ref, act_ref, *, tiles_per_seq):
    tm = h_ref.shape[0]
    pad = SUBLANES

    @pl.when(pl.program_id(0) % tiles_per_seq == 0)
    def _():
        hist_ref[...] = jnp.zeros(hist_ref.shape, F32)

    hn = _rms(h_ref[...], gc_ref[...]).astype(BF16)
    for c in range(D_MODEL // CONV_CHUNK):
        cs = slice(c * CONV_CHUNK, (c + 1) * CONV_CHUNK)
        b_gate = _dot(hn, win_ref[:, c * CONV_CHUNK:(c + 1) * CONV_CHUNK])
        c_gate = _dot(hn, win_ref[:, D_MODEL + c * CONV_CHUNK:D_MODEL + (c + 1) * CONV_CHUNK])
        xp = _dot(hn, win_ref[:, 2 * D_MODEL + c * CONV_CHUNK:2 * D_MODEL + (c + 1) * CONV_CHUNK])
        u = c_gate * xp
        ubuf = ubuf_ref.at[0]
        ubuf[0:pad, :] = hist_ref[:, cs]
        ubuf[pad:pad + tm, :] = u
        hist_ref[:, cs] = ubuf[tm:tm + pad, :]
        u_conv = (cw_ref[0:1, cs] * ubuf[pad - 2:pad - 2 + tm, :]
                  + cw_ref[1:2, cs] * ubuf[pad - 1:pad - 1 + tm, :]
                  + cw_ref[2:3, cs] * u)
        act_ref[:, cs] = (b_gate * u_conv).astype(BF16)

    whole = [pl.ds(0, tm)]
    out_ref[...] = h_ref[...] + _dot(act_ref[:, :D_MODEL], wout_ref[...])
    y, = _swiglu_halves([out_ref[...]], gf_ref[1:2, :], wg_ref, wu_ref, wd_ref, act_ref, whole)
    out_ref[...] = _rms(out_ref[...] + y, gfin_ref[...])


def _conv_ffn(h, gc, win, cw, wout, gf, wg, wu, wd, gfin, seq):
    t = h.shape[0]
    tm = TOKEN_TILE
    return pl.pallas_call(
        functools.partial(_conv_ffn_kernel, tiles_per_seq=seq // tm),
        grid=(t // tm,),
        in_specs=[
            pl.BlockSpec((tm, D_MODEL), lambda i: (i, 0)),
            _const_spec(gc.shape), _const_spec(win.shape), _layer0_spec(cw),
            _const_spec(wout.shape), _const_spec(gf.shape), _const_spec(wg.shape),
            _const_spec(wu.shape), _const_spec(wd.shape), _const_spec(gfin.shape),
        ],
        out_specs=pl.BlockSpec((tm, D_MODEL), lambda i: (i, 0)),
        out_shape=jax.ShapeDtypeStruct((t, D_MODEL), F32),
        scratch_shapes=[
            pltpu.VMEM((1, tm + SUBLANES, CONV_CHUNK), F32),
            pltpu.VMEM((SUBLANES, D_MODEL), F32),
            pltpu.VMEM((tm, wg.shape[1]), BF16),
        ],
        compiler_params=_params(("arbitrary",)),
        name="conv_ffn",
    )(h, gc, win, cw, wout, gf, wg, wu, wd, gfin)


def _rope_inv_freq_lanes():
    inv = 1.0 / (ROPE_THETA ** (np.arange(0, QK_ROPE_DIM, 2, dtype=np.float32) / np.float32(QK_ROPE_DIM)))
    inv = inv.astype(np.float32)
    return jnp.asarray(np.tile(inv, LANES // (QK_ROPE_DIM // 2))[None, :])


def kernel(x, positions, mla_norm, mla_w_in, mla_g_cq, mla_g_ckv, mla_w_uq, mla_w_ukv, mla_w_o,
           conv_norm, conv_w_in, conv_w, conv_w_out, ffn_norm, ffn_w_gate, ffn_w_up, ffn_w_down,
           final_norm):
    bsz, seq, d = x.shape
    t = bsz * seq
    assert d == D_MODEL and seq % TOKEN_TILE == 0 and seq % (ATTN_CHAINS * ATTN_TILE) == 0 and ATTN_TILE % CHUNK == 0
    x2 = x.reshape(t, d)
    qn, qr, kn, kr, v = _mla_pre(
        x2, positions.reshape(t, 1), mla_norm, mla_w_in, mla_g_cq, mla_g_ckv, mla_w_uq, mla_w_ukv,
        _rope_inv_freq_lanes())
    later_weights = [(mla_w_o, 0), (ffn_w_gate, 0), (ffn_w_up, 0), (ffn_w_down, 0),
                     (conv_w_in, 0), (conv_w_out, 0), (ffn_w_gate, 1), (ffn_w_up, 1), (ffn_w_down, 1)]
    (attn, wo_b, wg0_b, wu0_b, wd0_b, cwin_b, cwout_b, wg1_b, wu1_b, wd1_b) = _attention(
        qn, qr, kn, kr, v, bsz, seq, later_weights)
    h = _wo_ffn(attn, x2, wo_b, ffn_norm, wg0_b, wu0_b, wd0_b)
    out = _conv_ffn(h, conv_norm, cwin_b, conv_w, cwout_b, ffn_norm,
                    wg1_b, wu1_b, wd1_b, final_norm[None, :], seq)
    return out.reshape(bsz, seq, d)
```

```python
import functools
import math

import jax
import jax.numpy as jnp
import numpy as np
from jax import lax
from jax.experimental import pallas as pl
from jax.experimental.pallas import tpu as pltpu

D_MODEL = 1024
CHUNK = 64
N_HEADS = 8
QK_NOPE_DIM = 128
QK_ROPE_DIM = 64
V_HEAD_DIM = 128
Q_LORA_RANK = 512
KV_LORA_RANK = 256
ROPE_THETA = 10000.0
CONV_WIDTH = 3
RMS_EPS = 1e-6

LANES = 128
SUBLANES = 8
BF16_SUBLANES = 16
MXU_COLS = 256
VMEM_LIMIT_BYTES = 56 * 1024 * 1024

TOKEN_TILE = 1024
ATTN_TILE = 256
ATTN_CHAINS = 2
FF_CHUNK = MXU_COLS
CONV_CHUNK = MXU_COLS

BF16 = jnp.bfloat16
F32 = jnp.float32


def _dot(a, b):
    return jnp.dot(a, b, preferred_element_type=F32)


def _rms(x, g):
    return x * lax.rsqrt(jnp.mean(x * x, axis=-1, keepdims=True) + RMS_EPS) * g


def _const_spec(shape):
    return pl.BlockSpec(shape, lambda *_: (0,) * len(shape), pipeline_mode=pl.Buffered(1))


def _layer0_spec(w):
    return pl.BlockSpec((None,) + w.shape[1:], lambda *_: (0,) * w.ndim, pipeline_mode=pl.Buffered(1))


def _params(semantics):
    return pltpu.CompilerParams(dimension_semantics=semantics, vmem_limit_bytes=VMEM_LIMIT_BYTES)


def _cast_specs(w, layer, steps, step_of):
    _, rows, cols = w.shape
    out_shape = jax.ShapeDtypeStruct((rows, cols), BF16)
    if rows % (steps * BF16_SUBLANES) == 0:
        rb = rows // steps
        return (pl.BlockSpec((None, rb, cols), lambda *g: (layer, step_of(*g), 0)),
                pl.BlockSpec((rb, cols), lambda *g: (step_of(*g), 0)), out_shape, 1)
    per = steps // (cols // LANES)
    return (pl.BlockSpec((None, rows, LANES), lambda *g: (layer, 0, step_of(*g) // per)),
            pl.BlockSpec((rows, LANES), lambda *g: (0, step_of(*g) // per)), out_shape, per)


def _cast_side_work(step, cast_in_refs, cast_out_refs, periods):
    for src, dst, period in zip(cast_in_refs, cast_out_refs, periods):
        if period == 1:
            dst[...] = src[...].astype(BF16)
        else:
            @pl.when(step % period == 0)
            def _(src=src, dst=dst):
                dst[...] = src[...].astype(BF16)


def _mla_pre_kernel(x_ref, pos_ref, g_ref, w_in_ref, gcq_ref, gckv_ref, w_uq_ref, w_ukv_ref, invf_ref,
                    qn_ref, qr_ref, kn_ref, kr_ref, v_ref, w_in_b, w_uq_b, w_ukv_b, *, scale):
    tm = x_ref.shape[0]
    kr_col = Q_LORA_RANK + KV_LORA_RANK

    @pl.when(pl.program_id(0) == 0)
    def _():
        w_in_b[:, :kr_col] = w_in_ref[:, :kr_col].astype(BF16)
        w_in_b[:, kr_col:] = jnp.zeros((D_MODEL, LANES), BF16)
        w_in_b[:, kr_col:kr_col + QK_ROPE_DIM] = w_in_ref[:, kr_col:].astype(BF16)
        pair = 2 * (QK_NOPE_DIM + QK_ROPE_DIM)
        dst = lax.broadcasted_iota(jnp.int32, (pair, pair), 1)
        src = jnp.where(dst < QK_NOPE_DIM, dst,
                        jnp.where(dst < 2 * QK_NOPE_DIM, dst + QK_ROPE_DIM,
                                  jnp.where(dst < 2 * QK_NOPE_DIM + QK_ROPE_DIM, dst - QK_NOPE_DIM, dst)))
        select = (lax.broadcasted_iota(jnp.int32, (pair, pair), 0) == src).astype(BF16)
        n_nope = N_HEADS * QK_NOPE_DIM
        for p in range(N_HEADS // 2):
            wp = _dot(w_uq_ref[:, p * pair:(p + 1) * pair].astype(BF16), select).astype(BF16)
            w_uq_b[:, 2 * p * QK_NOPE_DIM:2 * (p + 1) * QK_NOPE_DIM] = wp[:, :2 * QK_NOPE_DIM]
            w_uq_b[:, n_nope + p * LANES:n_nope + (p + 1) * LANES] = wp[:, 2 * QK_NOPE_DIM:]
        w_ukv_b[...] = w_ukv_ref[...].astype(BF16)

    xn = _rms(x_ref[...], g_ref[...]).astype(BF16)
    proj = _dot(xn, w_in_b[...])
    cq = _rms(proj[:, :Q_LORA_RANK], gcq_ref[...]).astype(BF16)
    ckv = _rms(proj[:, Q_LORA_RANK:kr_col], gckv_ref[...]).astype(BF16)
    kr = proj[:, kr_col:]

    n_freq = QK_ROPE_DIM // 2
    n_grp = LANES // n_freq
    blk = tm // n_grp
    grp_c = lax.broadcasted_iota(jnp.int32, (blk, LANES), 1) // n_freq
    pos_c = pos_ref[...].astype(F32)
    ang_c = pos_c * invf_ref[...]
    cos_c = jnp.cos(ang_c)
    sin_c = jnp.sin(ang_c)
    cos_r = [cos_c] + [pltpu.roll(cos_c, n_freq * k, 1) for k in range(1, n_grp)]
    sin_r = [sin_c] + [pltpu.roll(sin_c, n_freq * k, 1) for k in range(1, n_grp)]
    nsin_r = [-t for t in sin_r]

    def expand(a, even_tabs, odd_tabs):
        out = None
        for g in reversed(range(n_grp)):
            tab = (even_tabs if g % 2 == 0 else odd_tabs)[(g - a) % n_grp]
            out = tab if out is None else jnp.where(grp_c == g, tab, out)
        return out

    cos = jnp.concatenate([expand(a, cos_r, cos_r) for a in range(n_grp)], axis=0)
    sin_signed = jnp.concatenate([expand(a, nsin_r, sin_r) for a in range(n_grp)], axis=0)
    lane = lax.broadcasted_iota(jnp.int32, (tm, LANES), 1)
    first_half = (lane % QK_ROPE_DIM) < n_freq

    def rope(t):
        partner = jnp.where(first_half,
                            pltpu.roll(t, LANES - n_freq, 1),
                            pltpu.roll(t, n_freq, 1))
        return t * cos + partner * sin_signed

    kr_rot = rope(kr)
    kr_ref[0] = kr_rot.astype(BF16)
    kr_ref[1] = pltpu.roll(kr_rot, QK_ROPE_DIM, 1).astype(BF16)

    q = _dot(cq, w_uq_b[...])
    for h in range(N_HEADS):
        qn_ref[h] = (q[:, h * LANES:(h + 1) * LANES] * scale).astype(BF16)
    for p in range(N_HEADS // 2):
        pair_cols = slice((N_HEADS + p) * LANES, (N_HEADS + p + 1) * LANES)
        qr_ref[p] = (rope(q[:, pair_cols]) * scale).astype(BF16)

    kv = _dot(ckv, w_ukv_b[...])
    for h in range(N_HEADS):
        kn_ref[h] = kv[:, 2 * h * LANES:(2 * h + 1) * LANES].astype(BF16)
        v_ref[h] = kv[:, (2 * h + 1) * LANES:(2 * h + 2) * LANES].astype(BF16)


def _mla_pre(x2, pos, g, w_in, gcq, gckv, w_uq, w_ukv, invf):
    t = x2.shape[0]
    tm = TOKEN_TILE
    scale = math.log2(math.e) / math.sqrt(QK_NOPE_DIM + QK_ROPE_DIM)

    def stacked(n):
        return (pl.BlockSpec((n, tm, LANES), lambda i: (0, i, 0)), jax.ShapeDtypeStruct((n, t, LANES), BF16))

    specs, shapes = zip(stacked(N_HEADS), stacked(N_HEADS // 2), stacked(N_HEADS), stacked(2), stacked(N_HEADS))
    return pl.pallas_call(
        functools.partial(_mla_pre_kernel, scale=scale),
        grid=(t // tm,),
        in_specs=[
            pl.BlockSpec((tm, D_MODEL), lambda i: (i, 0)),
            pl.BlockSpec((tm // (LANES // (QK_ROPE_DIM // 2)), LANES), lambda i: (i, 0)),
            _const_spec(g.shape), _layer0_spec(w_in), _const_spec(gcq.shape), _const_spec(gckv.shape),
            _layer0_spec(w_uq), _layer0_spec(w_ukv), _const_spec(invf.shape),
        ],
        out_specs=list(specs),
        out_shape=list(shapes),
        scratch_shapes=[
            pltpu.VMEM((D_MODEL, Q_LORA_RANK + KV_LORA_RANK + LANES), BF16),
            pltpu.VMEM(w_uq.shape[1:], BF16),
            pltpu.VMEM(w_ukv.shape[1:], BF16),
        ],
        compiler_params=_params(("arbitrary",)),
        name="mla_pre",
    )(x2, pos, g, w_in, gcq, gckv, w_uq, w_ukv, invf)


def _attn_kernel(*refs, cast_periods):
    n_cast = len(cast_periods)
    qn_ref, qr_ref, kn_ref, kr_ref, v_ref = refs[:5]
    o_ref = refs[5 + n_cast]
    m_ref, acc_ref = refs[-2:]
    step_id = pl.program_id(0) * pl.num_programs(1) + pl.program_id(1)
    _cast_side_work(step_id, refs[5:5 + n_cast], refs[6 + n_cast:-2], cast_periods)

    seq = qn_ref.shape[0]
    ta = ATTN_TILE
    groups = ta // LANES
    row_chunk = lax.broadcasted_iota(jnp.int32, (ta, ta), 0) // CHUNK
    col_chunk = lax.broadcasted_iota(jnp.int32, (ta, ta), 1) // CHUNK
    diag_mask = col_chunk <= row_chunk
    ones = jnp.ones((ta, LANES), BF16)

    def tile_slice(i):
        return pl.ds(i * ta, ta)

    def kv_tile(j):
        ks = tile_slice(j)
        k = jnp.concatenate([kn_ref[ks, :], kr_ref[ks, :]], axis=1)
        v_ext = jnp.concatenate([v_ref[ks, :], ones], axis=1)
        return k, v_ext

    def step(c, qs, k, v_ext, masked):
        q = jnp.concatenate([qn_ref[qs, :], qr_ref[qs, :]], axis=1)
        s = lax.dot_general(q, k, (((1,), (1,)), ((), ())), preferred_element_type=F32)
        if masked:
            s = jnp.where(diag_mask, s, -1e30)
        parts = [s[:, g * LANES:(g + 1) * LANES] for g in range(groups)]
        m_prev = m_ref[c]
        m_new = jnp.maximum(m_prev, jnp.max(functools.reduce(jnp.maximum, parts), axis=1, keepdims=True))
        alpha = jnp.exp2(m_prev - m_new)
        p = jnp.concatenate([jnp.exp2(part - m_new).astype(BF16) for part in parts], axis=1)
        acc_ref[c] = acc_ref[c] * jnp.concatenate([alpha, alpha], axis=1) + _dot(p, v_ext)
        m_ref[c] = m_new

    def finish(c, qs):
        acc = acc_ref[c]
        o_ref[qs, :] = (acc[:, :V_HEAD_DIM] / acc[:, V_HEAD_DIM:]).astype(o_ref.dtype)

    for group in range(seq // (ATTN_CHAINS * ta)):
        first = group * ATTN_CHAINS
        m_ref[...] = jnp.full(m_ref.shape, -1e30, F32)
        acc_ref[...] = jnp.zeros(acc_ref.shape, F32)
        for j in range(first + ATTN_CHAINS):
            k, v_ext = kv_tile(j)
            for c in range(max(0, j - first), ATTN_CHAINS):
                step(c, tile_slice(first + c), k, v_ext, masked=(j == first + c))
            if j >= first:
                finish(j - first, tile_slice(j))


def _attention(qn, qr, kn, kr, v, bsz, seq, casts):
    t = bsz * seq
    ta = ATTN_TILE
    head_spec = pl.BlockSpec((None, seq, LANES), lambda b, h: (h, b, 0))
    cast_specs = [_cast_specs(w, layer, bsz * N_HEADS, lambda b, h: b * N_HEADS + h) for w, layer in casts]
    return pl.pallas_call(
        functools.partial(_attn_kernel, cast_periods=tuple(c[3] for c in cast_specs)),
        grid=(bsz, N_HEADS),
        in_specs=[head_spec,
                  pl.BlockSpec((None, seq, LANES), lambda b, h: (h // 2, b, 0)),
                  head_spec,
                  pl.BlockSpec((None, seq, LANES), lambda b, h: (h % 2, b, 0)),
                  head_spec] + [c[0] for c in cast_specs],
        out_specs=[head_spec] + [c[1] for c in cast_specs],
        out_shape=[jax.ShapeDtypeStruct((N_HEADS, t, LANES), BF16)] + [c[2] for c in cast_specs],
        scratch_shapes=[pltpu.VMEM((ATTN_CHAINS, ta, LANES), F32),
                        pltpu.VMEM((ATTN_CHAINS, ta, 2 * LANES), F32)],
        compiler_params=_params(("arbitrary", "arbitrary")),
        name="attention",
    )(qn, qr, kn, kr, v, *[w for w, _ in casts])


def _row_halves(tm):
    return [pl.ds(r * (tm // 2), tm // 2) for r in range(2)]


def _swiglu_halves(h1s, g, wg_ref, wu_ref, wd_ref, act_ref, halves):
    d_ff = wg_ref.shape[1]
    hns = [_rms(h1, g).astype(BF16) for h1 in h1s]
    for c in range(d_ff // FF_CHUNK):
        cs = slice(c * FF_CHUNK, (c + 1) * FF_CHUNK)
        for rs, hn in zip(halves, hns):
            gate = _dot(hn, wg_ref[:, cs])
            up = _dot(hn, wu_ref[:, cs])
            act_ref[rs, cs] = (jax.nn.silu(gate) * up).astype(BF16)
    return [_dot(act_ref[rs, :], wd_ref[...]) for rs in halves]


def _wo_ffn_kernel(a_ref, h_ref, wo_ref, g_ref, wg_ref, wu_ref, wd_ref, out_ref, act_ref):
    halves = _row_halves(h_ref.shape[0])
    for rs in halves:
        a = jnp.concatenate([a_ref[h, rs, :] for h in range(N_HEADS)], axis=1)
        out_ref[rs, :] = h_ref[rs, :] + _dot(a, wo_ref[...])
    ys = _swiglu_halves([out_ref[rs, :] for rs in halves], g_ref[0:1, :], wg_ref, wu_ref, wd_ref, act_ref, halves)
    for rs, y in zip(halves, ys):
        out_ref[rs, :] = out_ref[rs, :] + y


def _wo_ffn(attn, h, wo, g, wg, wu, wd):
    t = h.shape[0]
    tm = TOKEN_TILE
    return pl.pallas_call(
        _wo_ffn_kernel,
        grid=(t // tm,),
        in_specs=[
            pl.BlockSpec((N_HEADS, tm, LANES), lambda i: (0, i, 0)),
            pl.BlockSpec((tm, D_MODEL), lambda i: (i, 0)),
            _const_spec(wo.shape), _const_spec(g.shape), _const_spec(wg.shape),
            _const_spec(wu.shape), _const_spec(wd.shape),
        ],
        out_specs=pl.BlockSpec((tm, D_MODEL), lambda i: (i, 0)),
        out_shape=jax.ShapeDtypeStruct((t, D_MODEL), F32),
        scratch_shapes=[pltpu.VMEM((tm, wg.shape[1]), BF16)],
        compiler_params=_params(("parallel",)),
        name="wo_ffn",
    )(attn, h, wo, g, wg, wu, wd)


def _conv_ffn_kernel(h_ref, gc_ref, win_ref, cw_ref, wout_ref, gf_ref, wg_ref, wu_ref, wd_ref,
                     gfin_ref, out_ref, ubuf_ref, hist_ref, act_ref, *, tiles_per_seq):
    tm = h_ref.shape[0]
    pad = SUBLANES

    @pl.when(pl.program_id(0) % tiles_per_seq == 0)
    def _():
        hist_ref[...] = jnp.zeros(hist_ref.shape, F32)

    hn = _rms(h_ref[...], gc_ref[...]).astype(BF16)
    for c in range(D_MODEL // CONV_CHUNK):
        cs = slice(c * CONV_CHUNK, (c + 1) * CONV_CHUNK)
        b_gate = _dot(hn, win_ref[:, c * CONV_CHUNK:(c + 1) * CONV_CHUNK])
        c_gate = _dot(hn, win_ref[:, D_MODEL + c * CONV_CHUNK:D_MODEL + (c + 1) * CONV_CHUNK])
        xp = _dot(hn, win_ref[:, 2 * D_MODEL + c * CONV_CHUNK:2 * D_MODEL + (c + 1) * CONV_CHUNK])
        u = c_gate * xp
        ubuf = ubuf_ref.at[c % 2]
        ubuf[0:pad, :] = hist_ref[:, cs]
        ubuf[pad:pad + tm, :] = u
        hist_ref[:, cs] = ubuf[tm:tm + pad, :]
        u_conv = (cw_ref[0:1, cs] * ubuf[pad - 2:pad - 2 + tm, :]
                  + cw_ref[1:2, cs] * ubuf[pad - 1:pad - 1 + tm, :]
                  + cw_ref[2:3, cs] * u)
        act_ref[:, cs] = (b_gate * u_conv).astype(BF16)

    whole = [pl.ds(0, tm)]
    out_ref[...] = h_ref[...] + _dot(act_ref[:, :D_MODEL], wout_ref[...])
    y, = _swiglu_halves([out_ref[...]], gf_ref[1:2, :], wg_ref, wu_ref, wd_ref, act_ref, whole)
    out_ref[...] = _rms(out_ref[...] + y, gfin_ref[...])


def _conv_ffn(h, gc, win, cw, wout, gf, wg, wu, wd, gfin, seq):
    t = h.shape[0]
    tm = TOKEN_TILE
    return pl.pallas_call(
        functools.partial(_conv_ffn_kernel, tiles_per_seq=seq // tm),
        grid=(t // tm,),
        in_specs=[
            pl.BlockSpec((tm, D_MODEL), lambda i: (i, 0)),
            _const_spec(gc.shape), _const_spec(win.shape), _layer0_spec(cw),
            _const_spec(wout.shape), _const_spec(gf.shape), _const_spec(wg.shape),
            _const_spec(wu.shape), _const_spec(wd.shape), _const_spec(gfin.shape),
        ],
        out_specs=pl.BlockSpec((tm, D_MODEL), lambda i: (i, 0)),
        out_shape=jax.ShapeDtypeStruct((t, D_MODEL), F32),
        scratch_shapes=[
            pltpu.VMEM((2, tm + SUBLANES, CONV_CHUNK), F32),
            pltpu.VMEM((SUBLANES, D_MODEL), F32),
            pltpu.VMEM((tm, wg.shape[1]), BF16),
        ],
        compiler_params=_params(("arbitrary",)),
        name="conv_ffn",
    )(h, gc, win, cw, wout, gf, wg, wu, wd, gfin)


def _rope_inv_freq_lanes():
    inv = 1.0 / (ROPE_THETA ** (np.arange(0, QK_ROPE_DIM, 2, dtype=np.float32) / np.float32(QK_ROPE_DIM)))
    inv = inv.astype(np.float32)
    return jnp.asarray(np.tile(inv, LANES // (QK_ROPE_DIM // 2))[None, :])


def kernel(x, positions, mla_norm, mla_w_in, mla_g_cq, mla_g_ckv, mla_w_uq, mla_w_ukv, mla_w_o,
           conv_norm, conv_w_in, conv_w, conv_w_out, ffn_norm, ffn_w_gate, ffn_w_up, ffn_w_down,
           final_norm):
    bsz, seq, d = x.shape
    t = bsz * seq
    assert d == D_MODEL and seq % TOKEN_TILE == 0 and seq % (ATTN_CHAINS * ATTN_TILE) == 0 and ATTN_TILE % CHUNK == 0
    x2 = x.reshape(t, d)
    n_freq = QK_ROPE_DIM // 2
    n_grp = LANES // n_freq
    pos_c = positions.reshape(t // TOKEN_TILE, n_grp, TOKEN_TILE // n_grp).transpose(0, 2, 1)
    pos_c = jnp.repeat(pos_c, n_freq, axis=2).reshape(t // n_grp, LANES)
    qn, qr, kn, kr, v = _mla_pre(
        x2, pos_c, mla_norm, mla_w_in, mla_g_cq, mla_g_ckv, mla_w_uq, mla_w_ukv,
        _rope_inv_freq_lanes())
    later_weights = [(mla_w_o, 0), (ffn_w_gate, 0), (ffn_w_up, 0), (ffn_w_down, 0),
                     (conv_w_in, 0), (conv_w_out, 0), (ffn_w_gate, 1), (ffn_w_up, 1), (ffn_w_down, 1)]
    (attn, wo_b, wg0_b, wu0_b, wd0_b, cwin_b, cwout_b, wg1_b, wu1_b, wd1_b) = _attention(
        qn, qr, kn, kr, v, bsz, seq, later_weights)
    h = _wo_ffn(attn, x2, wo_b, ffn_norm, wg0_b, wu0_b, wd0_b)
    out = _conv_ffn(h, conv_norm, cwin_b, conv_w, cwout_b, ffn_norm,
                    wg1_b, wu1_b, wd1_b, final_norm[None, :], seq)
    return out.reshape(bsz, seq, d)
```

```python
import functools
import math

import jax
import jax.numpy as jnp
import numpy as np
from jax import lax
from jax.experimental import pallas as pl
from jax.experimental.pallas import tpu as pltpu

D_MODEL = 1024
CHUNK = 64
N_HEADS = 8
QK_NOPE_DIM = 128
QK_ROPE_DIM = 64
V_HEAD_DIM = 128
Q_LORA_RANK = 512
KV_LORA_RANK = 256
ROPE_THETA = 10000.0
CONV_WIDTH = 3
RMS_EPS = 1e-6

LANES = 128
SUBLANES = 8
BF16_SUBLANES = 16
MXU_COLS = 256
VMEM_LIMIT_BYTES = 56 * 1024 * 1024

TOKEN_TILE = 1024
ATTN_TILE = 256
ATTN_CHAINS = 2
FF_CHUNK = MXU_COLS
CONV_CHUNK = MXU_COLS

BF16 = jnp.bfloat16
F32 = jnp.float32


def _dot(a, b):
    return jnp.dot(a, b, preferred_element_type=F32)


def _rms(x, g):
    return x * lax.rsqrt(jnp.mean(x * x, axis=-1, keepdims=True) + RMS_EPS) * g


def _const_spec(shape):
    return pl.BlockSpec(shape, lambda *_: (0,) * len(shape), pipeline_mode=pl.Buffered(1))


def _layer0_spec(w):
    return pl.BlockSpec((None,) + w.shape[1:], lambda *_: (0,) * w.ndim, pipeline_mode=pl.Buffered(1))


def _params(semantics):
    return pltpu.CompilerParams(dimension_semantics=semantics, vmem_limit_bytes=VMEM_LIMIT_BYTES)


def _cast_specs(w, layer, steps, step_of):
    _, rows, cols = w.shape
    out_shape = jax.ShapeDtypeStruct((rows, cols), BF16)
    if rows % (steps * BF16_SUBLANES) == 0:
        rb = rows // steps
        return (pl.BlockSpec((None, rb, cols), lambda *g: (layer, step_of(*g), 0)),
                pl.BlockSpec((rb, cols), lambda *g: (step_of(*g), 0)), out_shape, 1)
    per = steps // (cols // LANES)
    return (pl.BlockSpec((None, rows, LANES), lambda *g: (layer, 0, step_of(*g) // per)),
            pl.BlockSpec((rows, LANES), lambda *g: (0, step_of(*g) // per)), out_shape, per)


def _cast_side_work(step, cast_in_refs, cast_out_refs, periods):
    for src, dst, period in zip(cast_in_refs, cast_out_refs, periods):
        if period == 1:
            dst[...] = src[...].astype(BF16)
        else:
            @pl.when(step % period == 0)
            def _(src=src, dst=dst):
                dst[...] = src[...].astype(BF16)


def _mla_pre_kernel(x_ref, pos_ref, g_ref, w_in_ref, gcq_ref, gckv_ref, w_uq_ref, w_ukv_ref, invf_ref,
                    qn_ref, qr_ref, kn_ref, kr_ref, v_ref, w_in_b, w_uq_b, w_ukv_b, *, scale):
    tm = x_ref.shape[0]
    kr_col = Q_LORA_RANK + KV_LORA_RANK

    @pl.when(pl.program_id(0) == 0)
    def _():
        w_in_b[:, :kr_col] = w_in_ref[:, :kr_col].astype(BF16)
        w_in_b[:, kr_col:] = jnp.zeros((D_MODEL, LANES), BF16)
        w_in_b[:, kr_col:kr_col + QK_ROPE_DIM] = w_in_ref[:, kr_col:].astype(BF16)
        pair = 2 * (QK_NOPE_DIM + QK_ROPE_DIM)
        dst = lax.broadcasted_iota(jnp.int32, (pair, pair), 1)
        src = jnp.where(dst < QK_NOPE_DIM, dst,
                        jnp.where(dst < 2 * QK_NOPE_DIM, dst + QK_ROPE_DIM,
                                  jnp.where(dst < 2 * QK_NOPE_DIM + QK_ROPE_DIM, dst - QK_NOPE_DIM, dst)))
        select = (lax.broadcasted_iota(jnp.int32, (pair, pair), 0) == src).astype(BF16)
        n_nope = N_HEADS * QK_NOPE_DIM
        for p in range(N_HEADS // 2):
            wp = _dot(w_uq_ref[:, p * pair:(p + 1) * pair].astype(BF16), select).astype(BF16)
            w_uq_b[:, 2 * p * QK_NOPE_DIM:2 * (p + 1) * QK_NOPE_DIM] = wp[:, :2 * QK_NOPE_DIM]
            w_uq_b[:, n_nope + p * LANES:n_nope + (p + 1) * LANES] = wp[:, 2 * QK_NOPE_DIM:]
        w_ukv_b[...] = w_ukv_ref[...].astype(BF16)

    xn = _rms(x_ref[...], g_ref[...]).astype(BF16)
    proj = _dot(xn, w_in_b[...])
    cq = _rms(proj[:, :Q_LORA_RANK], gcq_ref[...]).astype(BF16)
    ckv = _rms(proj[:, Q_LORA_RANK:kr_col], gckv_ref[...]).astype(BF16)
    kr = proj[:, kr_col:]

    n_freq = QK_ROPE_DIM // 2
    n_grp = LANES // n_freq
    blk = tm // n_grp
    grp_c = lax.broadcasted_iota(jnp.int32, (blk, LANES), 1) // n_freq
    pos = pos_ref[...].astype(F32)
    pos_c = pos[(n_grp - 1) * blk:, :]
    for a in reversed(range(n_grp - 1)):
        pos_c = jnp.where(grp_c == a, pos[a * blk:(a + 1) * blk, :], pos_c)
    ang_c = pos_c * invf_ref[...]
    cos_c = jnp.cos(ang_c)
    sin_c = jnp.sin(ang_c)
    cos_r = [cos_c] + [pltpu.roll(cos_c, n_freq * k, 1) for k in range(1, n_grp)]
    sin_r = [sin_c] + [pltpu.roll(sin_c, n_freq * k, 1) for k in range(1, n_grp)]
    nsin_r = [-t for t in sin_r]

    def expand(a, even_tabs, odd_tabs):
        out = None
        for g in reversed(range(n_grp)):
            tab = (even_tabs if g % 2 == 0 else odd_tabs)[(g - a) % n_grp]
            out = tab if out is None else jnp.where(grp_c == g, tab, out)
        return out

    cos = jnp.concatenate([expand(a, cos_r, cos_r) for a in range(n_grp)], axis=0)
    sin_signed = jnp.concatenate([expand(a, nsin_r, sin_r) for a in range(n_grp)], axis=0)
    lane = lax.broadcasted_iota(jnp.int32, (tm, LANES), 1)
    first_half = (lane % QK_ROPE_DIM) < n_freq

    def rope(t):
        partner = jnp.where(first_half,
                            pltpu.roll(t, LANES - n_freq, 1),
                            pltpu.roll(t, n_freq, 1))
        return t * cos + partner * sin_signed

    kr_rot = rope(kr)
    kr_ref[0] = kr_rot.astype(BF16)
    kr_ref[1] = pltpu.roll(kr_rot, QK_ROPE_DIM, 1).astype(BF16)

    q = _dot(cq, w_uq_b[...])
    for h in range(N_HEADS):
        qn_ref[h] = (q[:, h * LANES:(h + 1) * LANES] * scale).astype(BF16)
    for p in range(N_HEADS // 2):
        pair_cols = slice((N_HEADS + p) * LANES, (N_HEADS + p + 1) * LANES)
        qr_ref[p] = (rope(q[:, pair_cols]) * scale).astype(BF16)

    kv = _dot(ckv, w_ukv_b[...])
    for h in range(N_HEADS):
        kn_ref[h] = kv[:, 2 * h * LANES:(2 * h + 1) * LANES].astype(BF16)
        v_ref[h] = kv[:, (2 * h + 1) * LANES:(2 * h + 2) * LANES].astype(BF16)


def _mla_pre(x2, pos, g, w_in, gcq, gckv, w_uq, w_ukv, invf):
    t = x2.shape[0]
    tm = TOKEN_TILE
    scale = math.log2(math.e) / math.sqrt(QK_NOPE_DIM + QK_ROPE_DIM)

    def stacked(n):
        return (pl.BlockSpec((n, tm, LANES), lambda i: (0, i, 0)), jax.ShapeDtypeStruct((n, t, LANES), BF16))

    specs, shapes = zip(stacked(N_HEADS), stacked(N_HEADS // 2), stacked(N_HEADS), stacked(2), stacked(N_HEADS))
    return pl.pallas_call(
        functools.partial(_mla_pre_kernel, scale=scale),
        grid=(t // tm,),
        in_specs=[
            pl.BlockSpec((tm, D_MODEL), lambda i: (i, 0)),
            pl.BlockSpec((tm, 1), lambda i: (i, 0)),
            _const_spec(g.shape), _layer0_spec(w_in), _const_spec(gcq.shape), _const_spec(gckv.shape),
            _layer0_spec(w_uq), _layer0_spec(w_ukv), _const_spec(invf.shape),
        ],
        out_specs=list(specs),
        out_shape=list(shapes),
        scratch_shapes=[
            pltpu.VMEM((D_MODEL, Q_LORA_RANK + KV_LORA_RANK + LANES), BF16),
            pltpu.VMEM(w_uq.shape[1:], BF16),
            pltpu.VMEM(w_ukv.shape[1:], BF16),
        ],
        compiler_params=_params(("arbitrary",)),
        name="mla_pre",
    )(x2, pos, g, w_in, gcq, gckv, w_uq, w_ukv, invf)


def _attn_kernel(*refs, cast_periods):
    n_cast = len(cast_periods)
    qn_ref, qr_ref, kn_ref, kr_ref, v_ref = refs[:5]
    o_ref = refs[5 + n_cast]
    m_ref, acc_ref = refs[-2:]
    step_id = pl.program_id(0) * pl.num_programs(1) + pl.program_id(1)
    _cast_side_work(step_id, refs[5:5 + n_cast], refs[6 + n_cast:-2], cast_periods)

    seq = qn_ref.shape[0]
    ta = ATTN_TILE
    groups = ta // LANES
    row_chunk = lax.broadcasted_iota(jnp.int32, (ta, ta), 0) // CHUNK
    col_chunk = lax.broadcasted_iota(jnp.int32, (ta, ta), 1) // CHUNK
    diag_mask = col_chunk <= row_chunk
    ones = jnp.ones((ta, LANES), BF16)

    def tile_slice(i):
        return pl.ds(i * ta, ta)

    def kv_tile(j):
        ks = tile_slice(j)
        k = jnp.concatenate([kn_ref[ks, :], kr_ref[ks, :]], axis=1)
        v_ext = jnp.concatenate([v_ref[ks, :], ones], axis=1)
        return k, v_ext

    def step(c, qs, k, v_ext, masked):
        q = jnp.concatenate([qn_ref[qs, :], qr_ref[qs, :]], axis=1)
        s = lax.dot_general(q, k, (((1,), (1,)), ((), ())), preferred_element_type=F32)
        if masked:
            s = jnp.where(diag_mask, s, -1e30)
        parts = [s[:, g * LANES:(g + 1) * LANES] for g in range(groups)]
        m_prev = m_ref[c]
        m_new = jnp.maximum(m_prev, jnp.max(functools.reduce(jnp.maximum, parts), axis=1, keepdims=True))
        alpha = jnp.exp2(m_prev - m_new)
        p = jnp.concatenate([jnp.exp2(part - m_new).astype(BF16) for part in parts], axis=1)
        acc_ref[c] = acc_ref[c] * jnp.concatenate([alpha, alpha], axis=1) + _dot(p, v_ext)
        m_ref[c] = m_new

    def finish(c, qs):
        acc = acc_ref[c]
        o_ref[qs, :] = (acc[:, :V_HEAD_DIM] / acc[:, V_HEAD_DIM:]).astype(o_ref.dtype)

    for group in range(seq // (ATTN_CHAINS * ta)):
        first = group * ATTN_CHAINS
        m_ref[...] = jnp.full(m_ref.shape, -1e30, F32)
        acc_ref[...] = jnp.zeros(acc_ref.shape, F32)
        for j in range(first + ATTN_CHAINS):
            for c in range(max(0, j - first), ATTN_CHAINS):
                k, v_ext = kv_tile(j)
                step(c, tile_slice(first + c), k, v_ext, masked=(j == first + c))
            if j >= first:
                finish(j - first, tile_slice(j))


def _attention(qn, qr, kn, kr, v, bsz, seq, casts):
    t = bsz * seq
    ta = ATTN_TILE
    head_spec = pl.BlockSpec((None, seq, LANES), lambda b, h: (h, b, 0))
    cast_specs = [_cast_specs(w, layer, bsz * N_HEADS, lambda b, h: b * N_HEADS + h) for w, layer in casts]
    return pl.pallas_call(
        functools.partial(_attn_kernel, cast_periods=tuple(c[3] for c in cast_specs)),
        grid=(bsz, N_HEADS),
        in_specs=[head_spec,
                  pl.BlockSpec((None, seq, LANES), lambda b, h: (h // 2, b, 0)),
                  head_spec,
                  pl.BlockSpec((None, seq, LANES), lambda b, h: (h % 2, b, 0)),
                  head_spec] + [c[0] for c in cast_specs],
        out_specs=[head_spec] + [c[1] for c in cast_specs],
        out_shape=[jax.ShapeDtypeStruct((N_HEADS, t, LANES), BF16)] + [c[2] for c in cast_specs],
        scratch_shapes=[pltpu.VMEM((ATTN_CHAINS, ta, LANES), F32),
                        pltpu.VMEM((ATTN_CHAINS, ta, 2 * LANES), F32)],
        compiler_params=_params(("arbitrary", "arbitrary")),
        name="attention",
    )(qn, qr, kn, kr, v, *[w for w, _ in casts])


def _row_halves(tm):
    return [pl.ds(r * (tm // 2), tm // 2) for r in range(2)]


def _swiglu_halves(h1s, g, wg_ref, wu_ref, wd_ref, act_ref, halves):
    d_ff = wg_ref.shape[1]
    hns = [_rms(h1, g).astype(BF16) for h1 in h1s]
    for c in range(d_ff // FF_CHUNK):
        cs = slice(c * FF_CHUNK, (c + 1) * FF_CHUNK)
        for rs, hn in zip(halves, hns):
            gate = _dot(hn, wg_ref[:, cs])
            up = _dot(hn, wu_ref[:, cs])
            act_ref[rs, cs] = (jax.nn.silu(gate) * up).astype(BF16)
    return [_dot(act_ref[rs, :], wd_ref[...]) for rs in halves]


def _wo_ffn_kernel(a_ref, h_ref, wo_ref, g_ref, wg_ref, wu_ref, wd_ref, out_ref, act_ref):
    halves = _row_halves(h_ref.shape[0])
    for rs in halves:
        a = jnp.concatenate([a_ref[h, rs, :] for h in range(N_HEADS)], axis=1)
        out_ref[rs, :] = h_ref[rs, :] + _dot(a, wo_ref[...])
    ys = _swiglu_halves([out_ref[rs, :] for rs in halves], g_ref[0:1, :], wg_ref, wu_ref, wd_ref, act_ref, halves)
    for rs, y in zip(halves, ys):
        out_ref[rs, :] = out_ref[rs, :] + y


def _wo_ffn(attn, h, wo, g, wg, wu, wd):
    t = h.shape[0]
    tm = TOKEN_TILE
    return pl.pallas_call(
        _wo_ffn_kernel,
        grid=(t // tm,),
        in_specs=[
            pl.BlockSpec((N_HEADS, tm, LANES), lambda i: (0, i, 0)),
            pl.BlockSpec((tm, D_MODEL), lambda i: (i, 0)),
            _const_spec(wo.shape), _const_spec(g.shape), _const_spec(wg.shape),
            _const_spec(wu.shape), _const_spec(wd.shape),
        ],
        out_specs=pl.BlockSpec((tm, D_MODEL), lambda i: (i, 0)),
        out_shape=jax.ShapeDtypeStruct((t, D_MODEL), F32),
        scratch_shapes=[pltpu.VMEM((tm, wg.shape[1]), BF16)],
        compiler_params=_params(("parallel",)),
        name="wo_ffn",
    )(attn, h, wo, g, wg, wu, wd)


def _conv_ffn_kernel(h_ref, gc_ref, win_ref, cw_ref, wout_ref, gf_ref, wg_ref, wu_ref, wd_ref,
                     gfin_ref, out_ref, ubuf_ref, hist_ref, act_ref, *, tiles_per_seq):
    tm = h_ref.shape[0]
    pad = SUBLANES

    @pl.when(pl.program_id(0) % tiles_per_seq == 0)
    def _():
        hist_ref[...] = jnp.zeros(hist_ref.shape, F32)

    hn = _rms(h_ref[...], gc_ref[...]).astype(BF16)
    for c in range(D_MODEL // CONV_CHUNK):
        cs = slice(c * CONV_CHUNK, (c + 1) * CONV_CHUNK)
        b_gate = _dot(hn, win_ref[:, c * CONV_CHUNK:(c + 1) * CONV_CHUNK])
        c_gate = _dot(hn, win_ref[:, D_MODEL + c * CONV_CHUNK:D_MODEL + (c + 1) * CONV_CHUNK])
        xp = _dot(hn, win_ref[:, 2 * D_MODEL + c * CONV_CHUNK:2 * D_MODEL + (c + 1) * CONV_CHUNK])
        u = c_gate * xp
        ubuf = ubuf_ref.at[c % 2]
        ubuf[0:pad, :] = hist_ref[:, cs]
        ubuf[pad:pad + tm, :] = u
        hist_ref[:, cs] = ubuf[tm:tm + pad, :]
        u_conv = (cw_ref[0:1, cs] * ubuf[pad - 2:pad - 2 + tm, :]
                  + cw_ref[1:2, cs] * ubuf[pad - 1:pad - 1 + tm, :]
                  + cw_ref[2:3, cs] * u)
        act_ref[:, cs] = (b_gate * u_conv).astype(BF16)

    whole = [pl.ds(0, tm)]
    out_ref[...] = h_ref[...] + _dot(act_ref[:, :D_MODEL], wout_ref[...])
    y, = _swiglu_halves([out_ref[...]], gf_ref[1:2, :], wg_ref, wu_ref, wd_ref, act_ref, whole)
    out_ref[...] = _rms(out_ref[...] + y, gfin_ref[...])


def _conv_ffn(h, gc, win, cw, wout, gf, wg, wu, wd, gfin, seq):
    t = h.shape[0]
    tm = TOKEN_TILE
    return pl.pallas_call(
        functools.partial(_conv_ffn_kernel, tiles_per_seq=seq // tm),
        grid=(t // tm,),
        in_specs=[
            pl.BlockSpec((tm, D_MODEL), lambda i: (i, 0)),
            _const_spec(gc.shape), _const_spec(win.shape), _layer0_spec(cw),
            _const_spec(wout.shape), _const_spec(gf.shape), _const_spec(wg.shape),
            _const_spec(wu.shape), _const_spec(wd.shape), _const_spec(gfin.shape),
        ],
        out_specs=pl.BlockSpec((tm, D_MODEL), lambda i: (i, 0)),
        out_shape=jax.ShapeDtypeStruct((t, D_MODEL), F32),
        scratch_shapes=[
            pltpu.VMEM((2, tm + SUBLANES, CONV_CHUNK), F32),
            pltpu.VMEM((SUBLANES, D_MODEL), F32),
            pltpu.VMEM((tm, wg.shape[1]), BF16),
        ],
        compiler_params=_params(("arbitrary",)),
        name="conv_ffn",
    )(h, gc, win, cw, wout, gf, wg, wu, wd, gfin)


def _rope_inv_freq_lanes():
    inv = 1.0 / (ROPE_THETA ** (np.arange(0, QK_ROPE_DIM, 2, dtype=np.float32) / np.float32(QK_ROPE_DIM)))
    inv = inv.astype(np.float32)
    return jnp.asarray(np.tile(inv, LANES // (QK_ROPE_DIM // 2))[None, :])


def kernel(x, positions, mla_norm, mla_w_in, mla_g_cq, mla_g_ckv, mla_w_uq, mla_w_ukv, mla_w_o,
           conv_norm, conv_w_in, conv_w, conv_w_out, ffn_norm, ffn_w_gate, ffn_w_up, ffn_w_down,
           final_norm):
    bsz, seq, d = x.shape
    t = bsz * seq
    assert d == D_MODEL and seq % TOKEN_TILE == 0 and seq % (ATTN_CHAINS * ATTN_TILE) == 0 and ATTN_TILE % CHUNK == 0
    x2 = x.reshape(t, d)
    qn, qr, kn, kr, v = _mla_pre(
        x2, positions.reshape(t, 1), mla_norm, mla_w_in, mla_g_cq, mla_g_ckv, mla_w_uq, mla_w_ukv,
        _rope_inv_freq_lanes())
    later_weights = [(mla_w_o, 0), (ffn_w_gate, 0), (ffn_w_up, 0), (ffn_w_down, 0),
                     (conv_w_in, 0), (conv_w_out, 0), (ffn_w_gate, 1), (ffn_w_up, 1), (ffn_w_down, 1)]
    (attn, wo_b, wg0_b, wu0_b, wd0_b, cwin_b, cwout_b, wg1_b, wu1_b, wd1_b) = _attention(
        qn, qr, kn, kr, v, bsz, seq, later_weights)
    h = _wo_ffn(attn, x2, wo_b, ffn_norm, wg0_b, wu0_b, wd0_b)
    out = _conv_ffn(h, conv_norm, cwin_b, conv_w, cwout_b, ffn_norm,
                    wg1_b, wu1_b, wd1_b, final_norm[None, :], seq)
    return out.reshape(bsz, seq, d)
```
